```python
import jax, jax.numpy as jnp
from jax import lax
import numpy as np

D_MODEL = 1024
BATCH = 2
SEQ = 8192
DEPTH = 1

N_META = 16
BLOCK_Q = 128
META_PAD = BLOCK_Q - N_META
ATTN_HEADS = 8
HEAD_DIM = 64
ATTN_WIDTH = ATTN_HEADS * HEAD_DIM
CONV_GROUPS = 8
CONV_WIDTH = 512
CONV_K = 3
D_FF = 2816
NORM_EPS = 1e-6
IN_SPLITS = (ATTN_WIDTH, ATTN_WIDTH, ATTN_WIDTH, ATTN_HEADS,
             CONV_WIDTH, CONV_WIDTH, CONV_WIDTH, D_MODEL, D_MODEL)
IN_COLS = ATTN_WIDTH * 3 + ATTN_HEADS + CONV_WIDTH * 3 + D_MODEL * 2

kernel_name = "hybrid_fox_shortconv_macaron_layer"


def rms_norm(x, g):
    xf = x.astype(jnp.float32)
    y = xf * lax.rsqrt(jnp.mean(xf * xf, axis=-1, keepdims=True) + NORM_EPS)
    return (y * g.astype(jnp.float32)).astype(x.dtype)


def swiglu(x, w_in, w_out):
    a, b = jnp.split(x @ w_in, 2, axis=-1)
    return (jax.nn.silu(a) * b) @ w_out


def split_cols(z):
    idx, acc = [], 0
    for s in IN_SPLITS[:-1]:
        acc += s
        idx.append(acc)
    return jnp.split(z, idx, axis=-1)


def forgetting_attention(q, k, v, log_f):
    B, L, H, Dh = q.shape
    F = jnp.cumsum(log_f, axis=1)
    n_blocks = (L + META_PAD) // BLOCK_Q
    q_blocks = jnp.pad(q, ((0, 0), (META_PAD, 0), (0, 0), (0, 0)))
    q_blocks = q_blocks.reshape(B, n_blocks, BLOCK_Q, H, Dh).transpose(1, 0, 2, 3, 4)
    fq_blocks = jnp.pad(F, ((0, 0), (META_PAD, 0), (0, 0)))
    fq_blocks = fq_blocks.reshape(B, n_blocks, BLOCK_Q, H).transpose(1, 0, 3, 2)
    f_k = F.transpose(0, 2, 1)
    k_pos = jnp.arange(L)
    scale = HEAD_DIM ** -0.5

    def one_block(args):
        blk, qb, fqb = args
        q_pos = blk * BLOCK_Q + jnp.arange(BLOCK_Q) - META_PAD
        s = jnp.einsum('bqhd,bkhd->bhqk', qb, k).astype(jnp.float32) * scale
        s = s + fqb[..., None] - f_k[:, :, None, :]
        mask = k_pos[None, :] <= jnp.maximum(q_pos, 0)[:, None]
        s = jnp.where(mask[None, None], s, -jnp.inf)
        p = jax.nn.softmax(s, axis=-1).astype(v.dtype)
        return jnp.einsum('bhqk,bkhd->bqhd', p, v)

    out = lax.map(one_block, (jnp.arange(n_blocks), q_blocks, fq_blocks))
    out = out.transpose(1, 0, 2, 3, 4).reshape(B, n_blocks * BLOCK_Q, H, Dh)
    return out[:, META_PAD:]


def short_conv(u, w):
    L = u.shape[1]
    up = jnp.pad(u, ((0, 0), (CONV_K - 1, 0), (0, 0)))
    y = up[:, 0:L] * w[0]
    for j in range(1, CONV_K):
        y = y + up[:, j:j + L] * w[j]
    return y


def hybrid_layer(h, w_in, b_forget, conv_w, w_attn_branch, w_conv_branch, w_out,
                 g_ffn1_pre, g_ffn1_post, w_ffn1_in, w_ffn1_out,
                 g_mix_pre, g_mix_post, g_ffn2_pre, g_ffn2_post, w_ffn2_in, w_ffn2_out):
    B, L, _ = h.shape
    h = h + 0.5 * rms_norm(swiglu(rms_norm(h, g_ffn1_pre), w_ffn1_in, w_ffn1_out), g_ffn1_post)
    u = rms_norm(h, g_mix_pre)
    q, k, v, f_logit, c_b, c_c, c_in, gate_a, gate_c = split_cols(u @ w_in)
    q = q.reshape(B, L, ATTN_HEADS, HEAD_DIM)
    k = k.reshape(B, L, ATTN_HEADS, HEAD_DIM)
    v = v.reshape(B, L, ATTN_HEADS, HEAD_DIM)
    log_f = jax.nn.log_sigmoid((f_logit + b_forget).astype(jnp.float32))
    y_attn = forgetting_attention(q, k, v, log_f).reshape(B, L, ATTN_WIDTH) @ w_attn_branch
    y_conv = (c_b * short_conv(c_c * c_in, conv_w)) @ w_conv_branch
    mixed = (jax.nn.sigmoid(gate_a) * y_attn + jax.nn.sigmoid(gate_c) * y_conv) @ w_out
    h = h + rms_norm(mixed, g_mix_post)
    h = h + 0.5 * rms_norm(swiglu(rms_norm(h, g_ffn2_pre), w_ffn2_in, w_ffn2_out), g_ffn2_post)
    return h


def setup_inputs(seed: int = 0) -> dict:
    key = jax.random.key(seed)
    ks = jax.random.split(key, 24)
    nrm = lambda k, shape, scale: jax.random.normal(k, shape, jnp.float32) * scale
    gain = lambda k: 1.0 + 0.05 * jax.random.normal(k, (DEPTH, D_MODEL), jnp.float32)
    return {
        "x": nrm(ks[0], (BATCH, SEQ, D_MODEL), 1.0),
        "meta_tokens": nrm(ks[1], (N_META, D_MODEL), 1.0),
        "w_in": nrm(ks[2], (DEPTH, D_MODEL, IN_COLS), D_MODEL ** -0.5),
        "b_forget": nrm(ks[3], (DEPTH, ATTN_HEADS), 0.1),
        "conv_w": nrm(ks[4], (DEPTH, CONV_K, CONV_WIDTH), CONV_K ** -0.5),
        "w_attn_branch": nrm(ks[5], (DEPTH, ATTN_WIDTH, D_MODEL), ATTN_WIDTH ** -0.5),
        "w_conv_branch": nrm(ks[6], (DEPTH, CONV_WIDTH, D_MODEL), CONV_WIDTH ** -0.5),
        "w_out": nrm(ks[7], (DEPTH, D_MODEL, D_MODEL), D_MODEL ** -0.5),
        "g_ffn1_pre": gain(ks[8]),
        "g_ffn1_post": gain(ks[9]),
        "w_ffn1_in": nrm(ks[10], (DEPTH, D_MODEL, 2 * D_FF), D_MODEL ** -0.5),
        "w_ffn1_out": nrm(ks[11], (DEPTH, D_FF, D_MODEL), D_FF ** -0.5),
        "g_mix_pre": gain(ks[12]),
        "g_mix_post": gain(ks[13]),
        "g_ffn2_pre": gain(ks[14]),
        "g_ffn2_post": gain(ks[15]),
        "w_ffn2_in": nrm(ks[16], (DEPTH, D_MODEL, 2 * D_FF), D_MODEL ** -0.5),
        "w_ffn2_out": nrm(ks[17], (DEPTH, D_FF, D_MODEL), D_FF ** -0.5),
    }


def reference(x, meta_tokens, w_in, b_forget, conv_w, w_attn_branch, w_conv_branch, w_out,
              g_ffn1_pre, g_ffn1_post, w_ffn1_in, w_ffn1_out,
              g_mix_pre, g_mix_post, g_ffn2_pre, g_ffn2_post, w_ffn2_in, w_ffn2_out):
    B = x.shape[0]
    meta = jnp.broadcast_to(meta_tokens.astype(x.dtype)[None], (B, N_META, x.shape[-1]))
    h = jnp.concatenate([meta, x], axis=1)
    for l in range(DEPTH):
        h = hybrid_layer(h, w_in[l], b_forget[l], conv_w[l], w_attn_branch[l], w_conv_branch[l], w_out[l],
                         g_ffn1_pre[l], g_ffn1_post[l], w_ffn1_in[l], w_ffn1_out[l],
                         g_mix_pre[l], g_mix_post[l], g_ffn2_pre[l], g_ffn2_post[l],
                         w_ffn2_in[l], w_ffn2_out[l])
    return h[:, N_META:]
```

```python
import functools

import jax
import jax.numpy as jnp
from jax import lax
from jax.experimental import pallas as pl
from jax.experimental.pallas import tpu as pltpu

D_MODEL = 1024
D_FF = 2816
N_META = 16
ATTN_HEADS = 8
HEAD_DIM = 64
ATTN_WIDTH = ATTN_HEADS * HEAD_DIM
CONV_WIDTH = 512
CONV_K = 3
NORM_EPS = 1e-6

LANES = 128
SUBLANES = 8
HEADS_PER_BLOCK = LANES // HEAD_DIM
VMEM_LIMIT_BYTES = 56 * 1024 * 1024
MASK_VALUE = -1e30

ROW_TILE = 512
FF_CHUNK = 256
ATTN_TQ = 512
ATTN_TK = 512

C_Q, C_K, C_V = 0, 512, 1024
C_CB, C_CC, C_CIN = 1536, 2048, 2560
C_GA, C_GC = 3072, 4096
C_F = 5120
IN_COLS_PADDED = C_F + LANES

bf16 = jnp.bfloat16
f32 = jnp.float32


def _rms_norm(x, g):
    ms = jnp.mean(x * x, axis=-1, keepdims=True)
    return x * lax.rsqrt(ms + NORM_EPS) * g


def _resident(shape):
    nd = len(shape)
    return pl.BlockSpec(shape, lambda *_: (0,) * nd, pipeline_mode=pl.Buffered(1))


def _swiglu_residual(h, g_pre, g_post, w_in_ref, w_out_ref, act_ref):
    u = _rms_norm(h, g_pre).astype(bf16)
    for c in range(D_FF // FF_CHUNK):
        lo = c * FF_CHUNK
        a = jnp.dot(u, w_in_ref[:, lo:lo + FF_CHUNK], preferred_element_type=f32)
        b = jnp.dot(u, w_in_ref[:, D_FF + lo:D_FF + lo + FF_CHUNK], preferred_element_type=f32)
        act_ref[:, lo:lo + FF_CHUNK] = (a * jax.nn.sigmoid(a) * b).astype(bf16)
    y = jnp.dot(act_ref[...], w_out_ref[...], preferred_element_type=f32)
    return h + 0.5 * _rms_norm(y, g_post)


def _ffn_body(h_ref, g_pre_ref, g_post_ref, w_in_ref, w_out_ref, o_ref, act_ref):
    o_ref[...] = _swiglu_residual(h_ref[...], g_pre_ref[...], g_post_ref[...],
                                  w_in_ref, w_out_ref, act_ref)


def _ffn(h, g_pre, g_post, w_in, w_out, tm):
    rows = h.shape[0]
    return pl.pallas_call(
        _ffn_body,
        grid=(rows // tm,),
        in_specs=[
            pl.BlockSpec((tm, D_MODEL), lambda i: (i, 0)),
            _resident((1, D_MODEL)),
            _resident((1, D_MODEL)),
            _resident((D_MODEL, 2 * D_FF)),
            _resident((D_FF, D_MODEL)),
        ],
        out_specs=pl.BlockSpec((tm, D_MODEL), lambda i: (i, 0)),
        out_shape=jax.ShapeDtypeStruct((rows, D_MODEL), f32),
        scratch_shapes=[pltpu.VMEM((tm, D_FF), bf16)],
        compiler_params=pltpu.CompilerParams(
            dimension_semantics=("arbitrary",), vmem_limit_bytes=VMEM_LIMIT_BYTES),
        name="ffn_half_step",
    )(h, g_pre, g_post, w_in, w_out)


def _log_sigmoid(x):
    return jnp.minimum(x, 0.0) - jnp.log1p(jnp.exp(-jnp.abs(x)))


def _split3(x):
    hi = x.astype(bf16)
    r = x - hi.astype(f32)
    mid = r.astype(bf16)
    lo = (r - mid.astype(f32)).astype(bf16)
    return hi, mid, lo


def _mix_in_body(h_ref, g_ref, w_ref, bf_ref, cw_ref, wc_ref, p_init_ref, f_init_ref,
                 q_ref, k_ref, v_ref, fsum_ref, sga_ref, gcv_ref, p_tail_ref,
                 p_scr, f_carry, *, tm):
    @pl.when(pl.program_id(1) == 0)
    def _():
        p_scr[0:SUBLANES, :] = p_init_ref[...]
        f_carry[...] = f_init_ref[...]

    u = _rms_norm(h_ref[...], g_ref[...]).astype(bf16)

    def proj(lo, width):
        return jnp.dot(u, w_ref[:, lo:lo + width], preferred_element_type=f32)

    q_ref[...] = (proj(C_Q, ATTN_WIDTH) * (HEAD_DIM ** -0.5)).astype(bf16)
    k_ref[...] = proj(C_K, ATTN_WIDTH).astype(bf16)
    v_ref[...] = proj(C_V, ATTN_WIDTH).astype(bf16)

    p = proj(C_CC, CONV_WIDTH) * proj(C_CIN, CONV_WIDTH)
    p_scr[SUBLANES:SUBLANES + tm, :] = p
    conv = (p_scr[SUBLANES - 2:SUBLANES - 2 + tm, :] * cw_ref[0:1, :]
            + p_scr[SUBLANES - 1:SUBLANES - 1 + tm, :] * cw_ref[1:2, :]
            + p * cw_ref[2:3, :])
    tail = p_scr[tm:tm + SUBLANES, :]
    p_scr[0:SUBLANES, :] = tail
    p_tail_ref[...] = tail
    conv_in = (proj(C_CB, CONV_WIDTH) * conv).astype(bf16)
    y_conv = jnp.dot(conv_in, wc_ref[...], preferred_element_type=f32)
    sga_ref[...] = jax.nn.sigmoid(proj(C_GA, D_MODEL)).astype(bf16)
    gcv_ref[...] = (jax.nn.sigmoid(proj(C_GC, D_MODEL)) * y_conv).astype(bf16)

    log_f = _log_sigmoid(proj(C_F, LANES) + bf_ref[...])
    rb = min(tm, LANES)
    tri = (lax.broadcasted_iota(jnp.int32, (rb, rb), 0)
           >= lax.broadcasted_iota(jnp.int32, (rb, rb), 1)).astype(bf16)
    carry = f_carry[...]
    for blk in range(tm // rb):
        hi, mid, lo = _split3(log_f[blk * rb:(blk + 1) * rb, :])
        cs = (jnp.dot(tri, hi, preferred_element_type=f32)
              + jnp.dot(tri, mid, preferred_element_type=f32)
              + jnp.dot(tri, lo, preferred_element_type=f32)) + carry
        fsum_ref[blk * rb:(blk + 1) * rb, :] = cs[:, :ATTN_HEADS]
        carry = cs[rb - 1:rb, :]
    f_carry[...] = carry


def _mix_in(h, g, w_all, b_f, conv_w, w_conv_branch, p_init, f_init, batch, tm):
    rows = h.shape[0]
    nt = rows // (batch * tm)
    row_blk = lambda width: pl.BlockSpec((tm, width), lambda b, t: (b * nt + t, 0))
    out_rows = lambda width, dt: jax.ShapeDtypeStruct((rows, width), dt)
    return pl.pallas_call(
        functools.partial(_mix_in_body, tm=tm),
        grid=(batch, nt),
        in_specs=[
            row_blk(D_MODEL),
            _resident((1, D_MODEL)),
            _resident((D_MODEL, IN_COLS_PADDED)),
            _resident((1, LANES)),
            _resident((CONV_K, CONV_WIDTH)),
            _resident((CONV_WIDTH, D_MODEL)),
            _resident((SUBLANES, CONV_WIDTH)),
            _resident((1, LANES)),
        ],
        out_specs=[
            row_blk(ATTN_WIDTH), row_blk(ATTN_WIDTH), row_blk(ATTN_WIDTH),
            row_blk(ATTN_HEADS), row_blk(D_MODEL), row_blk(D_MODEL),
            pl.BlockSpec((SUBLANES, CONV_WIDTH), lambda b, t: (b * nt + t, 0)),
        ],
        out_shape=[
            out_rows(ATTN_WIDTH, bf16), out_rows(ATTN_WIDTH, bf16), out_rows(ATTN_WIDTH, bf16),
            out_rows(ATTN_HEADS, f32), out_rows(D_MODEL, bf16), out_rows(D_MODEL, bf16),
            jax.ShapeDtypeStruct((batch * nt * SUBLANES, CONV_WIDTH), f32),
        ],
        scratch_shapes=[pltpu.VMEM((tm + SUBLANES, CONV_WIDTH), f32),
                        pltpu.VMEM((1, LANES), f32)],
        compiler_params=pltpu.CompilerParams(
            dimension_semantics=("arbitrary", "arbitrary"), vmem_limit_bytes=VMEM_LIMIT_BYTES),
        name="mixer_input",
    )(h, g, w_all, b_f, conv_w, w_conv_branch, p_init, f_init)


def _attn_body(q_ref, k_ref, v_ref, fq_ref, fk_ref, km_ref, vm_ref, fkm_ref, o_ref,
               m_scr, l_scr, acc_scr, *, tq, tk):
    i = pl.program_id(2)
    low_half = lax.broadcasted_iota(jnp.int32, (tq, LANES), 1) < HEAD_DIM
    q2 = q_ref[...]
    zeros = jnp.zeros_like(q2)
    q_heads = (jnp.where(low_half, q2, zeros), jnp.where(low_half, zeros, q2))
    contract_last = (((1,), (1,)), ((), ()))

    def update(hh, s, v2, first):
        if first:
            m_new = jnp.max(s, axis=-1, keepdims=True)
            p = jnp.exp(s - m_new)
            l_scr[hh] = jnp.sum(p, axis=-1, keepdims=True)
            acc_scr[hh] = jnp.dot(p.astype(bf16), v2, preferred_element_type=f32)
        else:
            m_old = m_scr[hh]
            m_new = jnp.maximum(m_old, jnp.max(s, axis=-1, keepdims=True))
            alpha = jnp.exp(m_old - m_new)
            p = jnp.exp(s - m_new)
            l_scr[hh] = alpha * l_scr[hh] + jnp.sum(p, axis=-1, keepdims=True)
            acc_scr[hh] = alpha * acc_scr[hh] + jnp.dot(p.astype(bf16), v2,
                                                        preferred_element_type=f32)
        m_scr[hh] = m_new

    meta_valid = lax.broadcasted_iota(jnp.int32, (tq, LANES), 1) < N_META
    for hh in range(HEADS_PER_BLOCK):
        s = lax.dot_general(q_heads[hh], km_ref[...], contract_last, preferred_element_type=f32)
        s = s + fq_ref[hh] - fkm_ref[hh]
        update(hh, jnp.where(meta_valid, s, MASK_VALUE), vm_ref[...], first=True)

    def kv_step(j, valid):
        start = pl.multiple_of(j * tk, tk)
        k2 = k_ref[pl.ds(start, tk), :]
        v2 = v_ref[pl.ds(start, tk), :]
        for hh in range(HEADS_PER_BLOCK):
            s = lax.dot_general(q_heads[hh], k2, contract_last, preferred_element_type=f32)
            s = s + fq_ref[hh] - fk_ref[hh, :, pl.ds(start, tk)]
            if valid is not None:
                s = jnp.where(valid, s, MASK_VALUE)
            update(hh, s, v2, first=False)

    n_diag = tq // tk
    lax.fori_loop(0, i * n_diag, lambda j, c: (kv_step(j, None), c)[1], 0)
    row = lax.broadcasted_iota(jnp.int32, (tq, tk), 0)
    col = lax.broadcasted_iota(jnp.int32, (tq, tk), 1)
    for d in range(n_diag):
        kv_step(i * n_diag + d, col + d * tk <= row)

    out = jnp.where(low_half, acc_scr[0] / l_scr[0], acc_scr[1] / l_scr[1])
    o_ref[...] = out.astype(bf16)


def _attention(q, k, v, fq, fk, k_meta, v_meta, fk_meta, batch, seq, tq, tk):
    nq = seq // tq
    n_blk = ATTN_HEADS // HEADS_PER_BLOCK
    q_spec = pl.BlockSpec((tq, LANES), lambda b, p, i: (b * nq + i, p))
    kv_spec = pl.BlockSpec((seq, LANES), lambda b, p, i: (b, p))
    meta_spec = pl.BlockSpec((LANES, LANES), lambda b, p, i: (0, p))
    return pl.pallas_call(
        functools.partial(_attn_body, tq=tq, tk=tk),
        grid=(batch, n_blk, nq),
        in_specs=[
            q_spec, kv_spec, kv_spec,
            pl.BlockSpec((None, HEADS_PER_BLOCK, tq, 1), lambda b, p, i: (b, p, i, 0)),
            pl.BlockSpec((None, HEADS_PER_BLOCK, 1, seq), lambda b, p, i: (b, p, 0, 0)),
            meta_spec, meta_spec,
            pl.BlockSpec((HEADS_PER_BLOCK, 1, LANES), lambda b, p, i: (p, 0, 0)),
        ],
        out_specs=q_spec,
        out_shape=jax.ShapeDtypeStruct((batch * seq, ATTN_WIDTH), bf16),
        scratch_shapes=[pltpu.VMEM((HEADS_PER_BLOCK, tq, 1), f32),
                        pltpu.VMEM((HEADS_PER_BLOCK, tq, 1), f32),
                        pltpu.VMEM((HEADS_PER_BLOCK, tq, LANES), f32)],
        compiler_params=pltpu.CompilerParams(
            dimension_semantics=("arbitrary", "arbitrary", "arbitrary"),
            vmem_limit_bytes=VMEM_LIMIT_BYTES),
        name="forgetting_attention",
    )(q, k.reshape(batch * seq, ATTN_WIDTH), v.reshape(batch * seq, ATTN_WIDTH),
      fq, fk, k_meta, v_meta, fk_meta)


def _mix_out_body(h_ref, attn_ref, sga_ref, gcv_ref, wa_ref, wo_ref, g_ref, o_ref):
    y_attn = jnp.dot(attn_ref[...], wa_ref[...], preferred_element_type=f32)
    gated = sga_ref[...].astype(f32) * y_attn + gcv_ref[...].astype(f32)
    mixed = jnp.dot(gated.astype(bf16), wo_ref[...], preferred_element_type=f32)
    o_ref[...] = h_ref[...] + _rms_norm(mixed, g_ref[...])


def _mix_out(h, attn, sga, gcv, w_attn_branch, w_out, g, tm):
    rows = h.shape[0]
    row_blk = lambda width: pl.BlockSpec((tm, width), lambda i: (i, 0))
    return pl.pallas_call(
        _mix_out_body,
        grid=(rows // tm,),
        in_specs=[row_blk(D_MODEL), row_blk(ATTN_WIDTH), row_blk(D_MODEL), row_blk(D_MODEL),
                  _resident((ATTN_WIDTH, D_MODEL)), _resident((D_MODEL, D_MODEL)),
                  _resident((1, D_MODEL))],
        out_specs=row_blk(D_MODEL),
        out_shape=jax.ShapeDtypeStruct((rows, D_MODEL), f32),
        compiler_params=pltpu.CompilerParams(
            dimension_semantics=("arbitrary",), vmem_limit_bytes=VMEM_LIMIT_BYTES),
        name="mixer_output",
    )(h, attn, sga, gcv, w_attn_branch, w_out, g)


def kernel(x, meta_tokens, w_in, b_forget, conv_w, w_attn_branch, w_conv_branch, w_out,
           g_ffn1_pre, g_ffn1_post, w_ffn1_in, w_ffn1_out, g_mix_pre, g_mix_post,
           g_ffn2_pre, g_ffn2_post, w_ffn2_in, w_ffn2_out):
    batch, seq, d = x.shape
    assert d == D_MODEL and w_in.shape[0] == 1 and meta_tokens.shape == (N_META, D_MODEL)
    assert seq % ROW_TILE == 0 and seq % ATTN_TQ == 0 and ATTN_TQ % ATTN_TK == 0

    gain = lambda g: g[0].reshape(1, D_MODEL).astype(f32)
    w = w_in[0]
    n_qkv = 3 * ATTN_WIDTH
    w_all = jnp.concatenate(
        [w[:, :n_qkv], w[:, n_qkv + ATTN_HEADS:], w[:, n_qkv:n_qkv + ATTN_HEADS],
         jnp.zeros((D_MODEL, LANES - ATTN_HEADS), w.dtype)], axis=1).astype(bf16)
    b_f = jnp.pad(b_forget[0].astype(f32), (0, LANES - ATTN_HEADS)).reshape(1, LANES)
    cw = conv_w[0].astype(f32)
    wc = w_conv_branch[0].astype(bf16)
    wa = w_attn_branch[0].astype(bf16)
    wo = w_out[0].astype(bf16)
    w1_in, w1_out = w_ffn1_in[0].astype(bf16), w_ffn1_out[0].astype(bf16)
    w2_in, w2_out = w_ffn2_in[0].astype(bf16), w_ffn2_out[0].astype(bf16)

    hm = _ffn(meta_tokens.astype(f32), gain(g_ffn1_pre), gain(g_ffn1_post), w1_in, w1_out, N_META)
    zeros_p = jnp.zeros((SUBLANES, CONV_WIDTH), f32)
    zeros_f = jnp.zeros((1, LANES), f32)
    _, km, vm, fm, _, _, pm_tail = _mix_in(hm, gain(g_mix_pre), w_all, b_f, cw, wc,
                                           zeros_p, zeros_f, 1, N_META)
    pad_rows = lambda a: jnp.pad(a, ((0, LANES - N_META), (0, 0)))
    f_init = jnp.pad(fm[N_META - 1:N_META, :], ((0, 0), (0, LANES - ATTN_HEADS)))
    fk_meta = jnp.pad(fm.T, ((0, 0), (0, LANES - N_META))).reshape(ATTN_HEADS, 1, LANES)

    rows = batch * seq
    h1 = _ffn(x.reshape(rows, D_MODEL), gain(g_ffn1_pre), gain(g_ffn1_post), w1_in, w1_out, ROW_TILE)
    q, k, v, fsum, sga, gcv, _ = _mix_in(h1, gain(g_mix_pre), w_all, b_f, cw, wc,
                                         pm_tail, f_init, batch, ROW_TILE)
    f_heads = fsum.reshape(batch, seq, ATTN_HEADS).transpose(0, 2, 1)
    attn = _attention(q, k, v, f_heads[..., None], f_heads[:, :, None, :],
                      pad_rows(km), pad_rows(vm), fk_meta, batch, seq, ATTN_TQ, ATTN_TK)
    h2 = _mix_out(h1, attn, sga, gcv, wa, wo, gain(g_mix_post), ROW_TILE)
    h3 = _ffn(h2, gain(g_ffn2_pre), gain(g_ffn2_post), w2_in, w2_out, ROW_TILE)
    return h3.reshape(batch, seq, D_MODEL)
```

```python
import functools

import jax
import jax.numpy as jnp
from jax import lax
from jax.experimental import pallas as pl
from jax.experimental.pallas import tpu as pltpu

D_MODEL = 1024
D_FF = 2816
N_META = 16
ATTN_HEADS = 8
HEAD_DIM = 64
ATTN_WIDTH = ATTN_HEADS * HEAD_DIM
CONV_WIDTH = 512
CONV_K = 3
NORM_EPS = 1e-6

LANES = 128
SUBLANES = 8
HEADS_PER_BLOCK = LANES // HEAD_DIM
VMEM_LIMIT_BYTES = 56 * 1024 * 1024
MASK_VALUE = -1e30

ROW_TILE = 512
FF_CHUNK = 256
ATTN_TQ = 512
ATTN_TK = 512
ATTN_ROWS = 128
QK_LEAD = 3

C_Q, C_K, C_V = 0, 512, 1024
C_CB, C_CC, C_CIN = 1536, 2048, 2560
C_GA, C_GC = 3072, 4096
C_F = 5120
IN_COLS_PADDED = C_F + LANES

bf16 = jnp.bfloat16
f32 = jnp.float32


def _rms_norm(x, g):
    ms = jnp.mean(x * x, axis=-1, keepdims=True)
    return x * lax.rsqrt(ms + NORM_EPS) * g


def _resident(shape):
    nd = len(shape)
    return pl.BlockSpec(shape, lambda *_: (0,) * nd, pipeline_mode=pl.Buffered(1))


def _swiglu_residual(h, g_pre, g_post, w_in_ref, w_out_ref, act_ref):
    u = _rms_norm(h, g_pre).astype(bf16)
    for c in range(D_FF // FF_CHUNK):
        lo = c * FF_CHUNK
        a = jnp.dot(u, w_in_ref[:, lo:lo + FF_CHUNK], preferred_element_type=f32)
        b = jnp.dot(u, w_in_ref[:, D_FF + lo:D_FF + lo + FF_CHUNK], preferred_element_type=f32)
        act_ref[:, lo:lo + FF_CHUNK] = (a * jax.nn.sigmoid(a) * b).astype(bf16)
    y = jnp.dot(act_ref[...], w_out_ref[...], preferred_element_type=f32)
    return h + 0.5 * _rms_norm(y, g_post)


def _ffn_body(h_ref, g_pre_ref, g_post_ref, w_in_ref, w_out_ref, o_ref, act_ref):
    o_ref[...] = _swiglu_residual(h_ref[...], g_pre_ref[...], g_post_ref[...],
                                  w_in_ref, w_out_ref, act_ref)


def _ffn(h, g_pre, g_post, w_in, w_out, tm):
    rows = h.shape[0]
    return pl.pallas_call(
        _ffn_body,
        grid=(rows // tm,),
        in_specs=[
            pl.BlockSpec((tm, D_MODEL), lambda i: (i, 0)),
            _resident((1, D_MODEL)),
            _resident((1, D_MODEL)),
            _resident((D_MODEL, 2 * D_FF)),
            _resident((D_FF, D_MODEL)),
        ],
        out_specs=pl.BlockSpec((tm, D_MODEL), lambda i: (i, 0)),
        out_shape=jax.ShapeDtypeStruct((rows, D_MODEL), f32),
        scratch_shapes=[pltpu.VMEM((tm, D_FF), bf16)],
        compiler_params=pltpu.CompilerParams(
            dimension_semantics=("arbitrary",), vmem_limit_bytes=VMEM_LIMIT_BYTES),
        name="ffn_half_step",
    )(h, g_pre, g_post, w_in, w_out)


def _log_sigmoid(x):
    return jnp.minimum(x, 0.0) - jnp.log1p(jnp.exp(-jnp.abs(x)))


def _split3(x):
    hi = x.astype(bf16)
    r = x - hi.astype(f32)
    mid = r.astype(bf16)
    lo = (r - mid.astype(f32)).astype(bf16)
    return hi, mid, lo


def _mix_in_body(h_ref, g_ref, w_ref, bf_ref, cw_ref, wc_ref, p_init_ref, f_init_ref,
                 q_ref, k_ref, v_ref, fsum_ref, sga_ref, gcv_ref, p_tail_ref,
                 p_scr, f_carry, *, tm):
    @pl.when(pl.program_id(1) == 0)
    def _():
        p_scr[0:SUBLANES, :] = p_init_ref[...]
        f_carry[...] = f_init_ref[...]

    u = _rms_norm(h_ref[...], g_ref[...]).astype(bf16)

    def proj(lo, width):
        return jnp.dot(u, w_ref[:, lo:lo + width], preferred_element_type=f32)

    q_ref[...] = (proj(C_Q, ATTN_WIDTH) * (HEAD_DIM ** -0.5)).astype(bf16)
    k_ref[...] = proj(C_K, ATTN_WIDTH).astype(bf16)
    v = proj(C_V, ATTN_WIDTH)
    lane = lax.broadcasted_iota(jnp.int32, (tm, LANES), 1)
    ones_a = jnp.where(lane == HEAD_DIM, 1.0, 0.0)
    ones_b = jnp.where(lane == 0, 1.0, 0.0)
    for pair in range(ATTN_HEADS // HEADS_PER_BLOCK):
        v2 = v[:, pair * LANES:(pair + 1) * LANES]
        c0 = pair * HEADS_PER_BLOCK * LANES
        v_ref[:, c0:c0 + LANES] = jnp.where(lane < HEAD_DIM, v2, ones_a).astype(bf16)
        v_ref[:, c0 + LANES:c0 + 2 * LANES] = jnp.where(lane < HEAD_DIM, ones_b, v2).astype(bf16)

    p = proj(C_CC, CONV_WIDTH) * proj(C_CIN, CONV_WIDTH)
    p_scr[SUBLANES:SUBLANES + tm, :] = p
    conv = (p_scr[SUBLANES - 2:SUBLANES - 2 + tm, :] * cw_ref[0:1, :]
            + p_scr[SUBLANES - 1:SUBLANES - 1 + tm, :] * cw_ref[1:2, :]
            + p * cw_ref[2:3, :])
    tail = p_scr[tm:tm + SUBLANES, :]
    p_scr[0:SUBLANES, :] = tail
    p_tail_ref[...] = tail
    conv_in = (proj(C_CB, CONV_WIDTH) * conv).astype(bf16)
    y_conv = jnp.dot(conv_in, wc_ref[...], preferred_element_type=f32)
    sga_ref[...] = jax.nn.sigmoid(proj(C_GA, D_MODEL)).astype(bf16)
    gcv_ref[...] = (jax.nn.sigmoid(proj(C_GC, D_MODEL)) * y_conv).astype(bf16)

    log_f = _log_sigmoid(proj(C_F, LANES) + bf_ref[...])
    rb = min(tm, LANES)
    tri = (lax.broadcasted_iota(jnp.int32, (rb, rb), 0)
           >= lax.broadcasted_iota(jnp.int32, (rb, rb), 1)).astype(bf16)
    carry = f_carry[...]
    for blk in range(tm // rb):
        hi, mid, lo = _split3(log_f[blk * rb:(blk + 1) * rb, :])
        cs = (jnp.dot(tri, hi, preferred_element_type=f32)
              + jnp.dot(tri, mid, preferred_element_type=f32)
              + jnp.dot(tri, lo, preferred_element_type=f32)) + carry
        fsum_ref[blk * rb:(blk + 1) * rb, :] = cs[:, :ATTN_HEADS]
        carry = cs[rb - 1:rb, :]
    f_carry[...] = carry


def _mix_in(h, g, w_all, b_f, conv_w, w_conv_branch, p_init, f_init, batch, tm):
    rows = h.shape[0]
    nt = rows // (batch * tm)
    row_blk = lambda width: pl.BlockSpec((tm, width), lambda b, t: (b * nt + t, 0))
    out_rows = lambda width, dt: jax.ShapeDtypeStruct((rows, width), dt)
    return pl.pallas_call(
        functools.partial(_mix_in_body, tm=tm),
        grid=(batch, nt),
        in_specs=[
            row_blk(D_MODEL),
            _resident((1, D_MODEL)),
            _resident((D_MODEL, IN_COLS_PADDED)),
            _resident((1, LANES)),
            _resident((CONV_K, CONV_WIDTH)),
            _resident((CONV_WIDTH, D_MODEL)),
            _resident((SUBLANES, CONV_WIDTH)),
            _resident((1, LANES)),
        ],
        out_specs=[
            row_blk(ATTN_WIDTH), row_blk(ATTN_WIDTH), row_blk(ATTN_HEADS * LANES),
            row_blk(ATTN_HEADS), row_blk(D_MODEL), row_blk(D_MODEL),
            pl.BlockSpec((SUBLANES, CONV_WIDTH), lambda b, t: (b * nt + t, 0)),
        ],
        out_shape=[
            out_rows(ATTN_WIDTH, bf16), out_rows(ATTN_WIDTH, bf16),
            out_rows(ATTN_HEADS * LANES, bf16),
            out_rows(ATTN_HEADS, f32), out_rows(D_MODEL, bf16), out_rows(D_MODEL, bf16),
            jax.ShapeDtypeStruct((batch * nt * SUBLANES, CONV_WIDTH), f32),
        ],
        scratch_shapes=[pltpu.VMEM((tm + SUBLANES, CONV_WIDTH), f32),
                        pltpu.VMEM((1, LANES), f32)],
        compiler_params=pltpu.CompilerParams(
            dimension_semantics=("arbitrary", "arbitrary"), vmem_limit_bytes=VMEM_LIMIT_BYTES),
        name="mixer_input",
    )(h, g, w_all, b_f, conv_w, w_conv_branch, p_init, f_init)


def _attn_body(q_ref, k_ref, v_ref, fq_ref, fk_ref, km_ref, vm_ref, fkm_ref, o_ref,
               qh_scr, fqb_scr, m_scr, acc_scr, s_scr, *, tq, tk, rows):
    i = pl.program_id(2)
    low_half = lax.broadcasted_iota(jnp.int32, (tq, LANES), 1) < HEAD_DIM
    q2 = q_ref[...]
    zeros = jnp.zeros_like(q2)
    qh_scr[0] = jnp.where(low_half, q2, zeros)
    qh_scr[1] = jnp.where(low_half, zeros, q2)
    for hh in range(HEADS_PER_BLOCK):
        fqb_scr[hh] = jnp.broadcast_to(fq_ref[hh], (tq, LANES))
    m_scr[...] = jnp.full(m_scr.shape, MASK_VALUE, f32)
    acc_scr[...] = jnp.zeros(acc_scr.shape, f32)
    contract_last = (((1,), (1,)), ((), ()))
    lane = lax.broadcasted_iota(jnp.int32, (rows, LANES), 1)
    sub = lax.broadcasted_iota(jnp.int32, (rows, LANES), 0)
    items = [(hh, r0) for hh in range(HEADS_PER_BLOCK) for r0 in range(0, tq, rows)]

    def scores(item, k2):
        hh, r0 = item
        return lax.dot_general(qh_scr[hh, r0:r0 + rows, :], k2, contract_last,
                               preferred_element_type=f32)

    def softmax_pv(item, parts):
        hh, r0 = item
        rs = slice(r0, r0 + rows)
        fqb = fqb_scr[hh, rs, :]
        blocks = []
        for s, fk_row, _, valid in parts:
            for c in range(s.shape[1] // LANES):
                cs = slice(c * LANES, (c + 1) * LANES)
                blk = s[:, cs] + fqb - fk_row[:, cs]
                if valid[c] is not None:
                    blk = jnp.where(valid[c], blk, MASK_VALUE)
                blocks.append(blk)
        m_old = m_scr[hh, rs, :]
        m_blk = functools.reduce(jnp.maximum, blocks)
        m_new = jnp.maximum(
            m_old, jnp.broadcast_to(jnp.max(m_blk, axis=-1, keepdims=True), (rows, LANES)))
        probs = [jnp.exp(b - m_new) for b in blocks]
        pv = None
        for s, _, v2, _ in parts:
            nb = s.shape[1] // LANES
            p = jnp.concatenate(probs[:nb], axis=-1).astype(bf16)
            probs = probs[nb:]
            d = jnp.dot(p, v2, preferred_element_type=f32)
            pv = d if pv is None else pv + d
        acc_scr[hh, rs, :] = jnp.exp(m_old - m_new) * acc_scr[hh, rs, :] + pv
        m_scr[hh, rs, :] = m_new

    lead = min(QK_LEAD, len(items))
    all_visible = [None] * (tk // LANES)

    k2 = k_ref[pl.ds(0, tk), :]
    for n, item in enumerate(items):
        s_scr[0, n] = scores(item, k2)

    def full_step(j, carry):
        for cur in range(2):
            pl.when(jnp.bitwise_and(j, 1) == cur)(functools.partial(step_body, j, cur, 1 - cur))
        return carry

    def step_body(j, cur, nxt):
        start = pl.multiple_of(j * tk, tk)
        k_next = k_ref[pl.ds(pl.multiple_of(start + tk, tk), tk), :]
        v2 = [v_ref[pl.ds(start, tk), hh * LANES:(hh + 1) * LANES] for hh in range(HEADS_PER_BLOCK)]
        fk_row = [fk_ref[hh, :, pl.ds(start, tk)] for hh in range(HEADS_PER_BLOCK)]
        for n in range(lead):
            s_scr[nxt, n] = scores(items[n], k_next)
        for n, item in enumerate(items):
            hh = item[0]
            softmax_pv(item, [(s_scr[cur, n], fk_row[hh], v2[hh], all_visible)])
            if n + lead < len(items):
                s_scr[nxt, n + lead] = scores(items[n + lead], k_next)

    lax.fori_loop(0, i, full_step, 0)

    cur = jnp.bitwise_and(i, 1)
    start = pl.multiple_of(i * tk, tk)
    meta_valid = [lane < N_META]
    s_meta = [scores(item, km_ref[...]) for item in items]
    for n, item in enumerate(items):
        hh, r0 = item
        n_cols = r0 + rows
        valid = [None if (c + 1) * LANES - 1 <= r0 else (lane + c * LANES <= sub + r0)
                 for c in range(n_cols // LANES)]
        softmax_pv(item, [
            (s_scr[cur, n, :, 0:n_cols], fk_ref[hh, :, pl.ds(start, n_cols)],
             v_ref[pl.ds(start, n_cols), hh * LANES:(hh + 1) * LANES], valid),
            (s_meta[n], fkm_ref[hh], vm_ref[:, hh * LANES:(hh + 1) * LANES], meta_valid)])

    acc_a, acc_b = acc_scr[0], acc_scr[1]
    out = jnp.where(low_half, acc_a / acc_a[:, HEAD_DIM:HEAD_DIM + 1], acc_b / acc_b[:, 0:1])
    o_ref[...] = out.astype(bf16)


def _attention(q, k, v_aug, fq, fk, k_meta, v_meta, fk_meta, batch, seq, tq, tk, rows):
    nq = seq // tq
    n_blk = ATTN_HEADS // HEADS_PER_BLOCK
    q_spec = pl.BlockSpec((tq, LANES), lambda b, p, i: (b * nq + i, p))
    return pl.pallas_call(
        functools.partial(_attn_body, tq=tq, tk=tk, rows=rows),
        grid=(batch, n_blk, nq),
        in_specs=[
            q_spec,
            pl.BlockSpec((seq, LANES), lambda b, p, i: (b, p)),
            pl.BlockSpec((seq, HEADS_PER_BLOCK * LANES), lambda b, p, i: (b, p)),
            pl.BlockSpec((None, HEADS_PER_BLOCK, tq, 1), lambda b, p, i: (b, p, i, 0)),
            pl.BlockSpec((None, HEADS_PER_BLOCK, 1, seq), lambda b, p, i: (b, p, 0, 0)),
            pl.BlockSpec((LANES, LANES), lambda b, p, i: (0, p)),
            pl.BlockSpec((LANES, HEADS_PER_BLOCK * LANES), lambda b, p, i: (0, p)),
            pl.BlockSpec((HEADS_PER_BLOCK, 1, LANES), lambda b, p, i: (p, 0, 0)),
        ],
        out_specs=q_spec,
        out_shape=jax.ShapeDtypeStruct((batch * seq, ATTN_WIDTH), bf16),
        scratch_shapes=[pltpu.VMEM((HEADS_PER_BLOCK, tq, LANES), bf16),
                        pltpu.VMEM((HEADS_PER_BLOCK, tq, LANES), f32),
                        pltpu.VMEM((HEADS_PER_BLOCK, tq, LANES), f32),
                        pltpu.VMEM((HEADS_PER_BLOCK, tq, LANES), f32),
                        pltpu.VMEM((2, HEADS_PER_BLOCK * tq // rows, rows, tk), f32)],
        compiler_params=pltpu.CompilerParams(
            dimension_semantics=("arbitrary", "arbitrary", "arbitrary"),
            vmem_limit_bytes=VMEM_LIMIT_BYTES),
        name="forgetting_attention",
    )(q, k, v_aug, fq, fk, k_meta, v_meta, fk_meta)


def _mix_out_body(h_ref, attn_ref, sga_ref, gcv_ref, wa_ref, wo_ref, g_ref, o_ref):
    y_attn = jnp.dot(attn_ref[...], wa_ref[...], preferred_element_type=f32)
    gated = sga_ref[...].astype(f32) * y_attn + gcv_ref[...].astype(f32)
    mixed = jnp.dot(gated.astype(bf16), wo_ref[...], preferred_element_type=f32)
    o_ref[...] = h_ref[...] + _rms_norm(mixed, g_ref[...])


def _mix_out(h, attn, sga, gcv, w_attn_branch, w_out, g, tm):
    rows = h.shape[0]
    row_blk = lambda width: pl.BlockSpec((tm, width), lambda i: (i, 0))
    return pl.pallas_call(
        _mix_out_body,
        grid=(rows // tm,),
        in_specs=[row_blk(D_MODEL), row_blk(ATTN_WIDTH), row_blk(D_MODEL), row_blk(D_MODEL),
                  _resident((ATTN_WIDTH, D_MODEL)), _resident((D_MODEL, D_MODEL)),
                  _resident((1, D_MODEL))],
        out_specs=row_blk(D_MODEL),
        out_shape=jax.ShapeDtypeStruct((rows, D_MODEL), f32),
        compiler_params=pltpu.CompilerParams(
            dimension_semantics=("arbitrary",), vmem_limit_bytes=VMEM_LIMIT_BYTES),
        name="mixer_output",
    )(h, attn, sga, gcv, w_attn_branch, w_out, g)


def kernel(x, meta_tokens, w_in, b_forget, conv_w, w_attn_branch, w_conv_branch, w_out,
           g_ffn1_pre, g_ffn1_post, w_ffn1_in, w_ffn1_out, g_mix_pre, g_mix_post,
           g_ffn2_pre, g_ffn2_post, w_ffn2_in, w_ffn2_out):
    batch, seq, d = x.shape
    assert d == D_MODEL and w_in.shape[0] == 1 and meta_tokens.shape == (N_META, D_MODEL)
    assert seq % ROW_TILE == 0 and seq % ATTN_TQ == 0 and ATTN_TQ == ATTN_TK
    assert ATTN_TQ % ATTN_ROWS == 0 and ATTN_ROWS % LANES == 0

    gain = lambda g: g[0].reshape(1, D_MODEL).astype(f32)
    w = w_in[0]
    n_qkv = 3 * ATTN_WIDTH
    w_all = jnp.concatenate(
        [w[:, :n_qkv], w[:, n_qkv + ATTN_HEADS:], w[:, n_qkv:n_qkv + ATTN_HEADS],
         jnp.zeros((D_MODEL, LANES - ATTN_HEADS), w.dtype)], axis=1).astype(bf16)
    b_f = jnp.pad(b_forget[0].astype(f32), (0, LANES - ATTN_HEADS)).reshape(1, LANES)
    cw = conv_w[0].astype(f32)
    wc = w_conv_branch[0].astype(bf16)
    wa = w_attn_branch[0].astype(bf16)
    wo = w_out[0].astype(bf16)
    w1_in, w1_out = w_ffn1_in[0].astype(bf16), w_ffn1_out[0].astype(bf16)
    w2_in, w2_out = w_ffn2_in[0].astype(bf16), w_ffn2_out[0].astype(bf16)

    hm = _ffn(meta_tokens.astype(f32), gain(g_ffn1_pre), gain(g_ffn1_post), w1_in, w1_out, N_META)
    zeros_p = jnp.zeros((SUBLANES, CONV_WIDTH), f32)
    zeros_f = jnp.zeros((1, LANES), f32)
    _, km, vm, fm, _, _, pm_tail = _mix_in(hm, gain(g_mix_pre), w_all, b_f, cw, wc,
                                           zeros_p, zeros_f, 1, N_META)
    pad_rows = lambda a: jnp.pad(a, ((0, LANES - N_META), (0, 0)))
    f_init = jnp.pad(fm[N_META - 1:N_META, :], ((0, 0), (0, LANES - ATTN_HEADS)))
    fk_meta = jnp.pad(fm.T, ((0, 0), (0, LANES - N_META))).reshape(ATTN_HEADS, 1, LANES)

    rows = batch * seq
    h1 = _ffn(x.reshape(rows, D_MODEL), gain(g_ffn1_pre), gain(g_ffn1_post), w1_in, w1_out, ROW_TILE)
    q, k, v, fsum, sga, gcv, _ = _mix_in(h1, gain(g_mix_pre), w_all, b_f, cw, wc,
                                         pm_tail, f_init, batch, ROW_TILE)
    f_heads = fsum.reshape(batch, seq, ATTN_HEADS).transpose(0, 2, 1)
    attn = _attention(q, k, v, f_heads[..., None], f_heads[:, :, None, :],
                      pad_rows(km), pad_rows(vm), fk_meta, batch, seq, ATTN_TQ, ATTN_TK, ATTN_ROWS)
    h2 = _mix_out(h1, attn, sga, gcv, wa, wo, gain(g_mix_post), ROW_TILE)
    h3 = _ffn(h2, gain(g_ffn2_pre), gain(g_ffn2_post), w2_in, w2_out, ROW_TILE)
    return h3.reshape(batch, seq, D_MODEL)
```

```python
import functools

import jax
import jax.numpy as jnp
import numpy as np
from jax import lax
from jax.experimental import pallas as pl
from jax.experimental.pallas import tpu as pltpu

D_MODEL = 1024
D_FF = 2816
N_META = 16
ATTN_HEADS = 8
HEAD_DIM = 64
ATTN_WIDTH = ATTN_HEADS * HEAD_DIM
CONV_WIDTH = 512
CONV_K = 3
NORM_EPS = 1e-6

LANES = 128
SUBLANES = 8
HEADS_PER_BLOCK = LANES // HEAD_DIM
VMEM_LIMIT_BYTES = 56 * 1024 * 1024
MASK_VALUE = -1e30
LOG2E = 1.4426950408889634
N_SPLIT = 3

ROW_TILE = 512
FF_CHUNK = 256
ATTN_TQ = 512
ATTN_TK = 512
ATTN_ROWS = 128
QK_LEAD = 3

C_Q, C_K, C_V = 0, 512, 1024
C_CB, C_CC, C_CIN = 1536, 2048, 2560
C_GA, C_GC = 3072, 4096
C_F = 5120
IN_COLS_PADDED = C_F + LANES

bf16 = jnp.bfloat16
f32 = jnp.float32


def _rms_norm(x, g):
    ms = jnp.mean(x * x, axis=-1, keepdims=True)
    return x * lax.rsqrt(ms + NORM_EPS) * g


def _resident(shape):
    nd = len(shape)
    return pl.BlockSpec(shape, lambda *_: (0,) * nd, pipeline_mode=pl.Buffered(1))


def _swiglu_residual(h, g_pre, g_post, w_in_ref, w_out_ref, act_ref):
    u = _rms_norm(h, g_pre).astype(bf16)
    for c in range(D_FF // FF_CHUNK):
        lo = c * FF_CHUNK
        a = jnp.dot(u, w_in_ref[:, lo:lo + FF_CHUNK], preferred_element_type=f32)
        b = jnp.dot(u, w_in_ref[:, D_FF + lo:D_FF + lo + FF_CHUNK], preferred_element_type=f32)
        act_ref[:, lo:lo + FF_CHUNK] = (a * jax.nn.sigmoid(a) * b).astype(bf16)
    y = jnp.dot(act_ref[...], w_out_ref[...], preferred_element_type=f32)
    return h + 0.5 * _rms_norm(y, g_post)


def _ffn_body(h_ref, g_pre_ref, g_post_ref, w_in_ref, w_out_ref, o_ref, act_ref):
    o_ref[...] = _swiglu_residual(h_ref[...], g_pre_ref[...], g_post_ref[...],
                                  w_in_ref, w_out_ref, act_ref)


def _ffn(h, g_pre, g_post, w_in, w_out, tm):
    rows = h.shape[0]
    return pl.pallas_call(
        _ffn_body,
        grid=(rows // tm,),
        in_specs=[
            pl.BlockSpec((tm, D_MODEL), lambda i: (i, 0)),
            _resident((1, D_MODEL)),
            _resident((1, D_MODEL)),
            _resident((D_MODEL, 2 * D_FF)),
            _resident((D_FF, D_MODEL)),
        ],
        out_specs=pl.BlockSpec((tm, D_MODEL), lambda i: (i, 0)),
        out_shape=jax.ShapeDtypeStruct((rows, D_MODEL), f32),
        scratch_shapes=[pltpu.VMEM((tm, D_FF), bf16)],
        compiler_params=pltpu.CompilerParams(
            dimension_semantics=("arbitrary",), vmem_limit_bytes=VMEM_LIMIT_BYTES),
        name="ffn_half_step",
    )(h, g_pre, g_post, w_in, w_out)


def _log_sigmoid(x):
    return jnp.minimum(x, 0.0) - jnp.log1p(jnp.exp(-jnp.abs(x)))


def _split3(x):
    hi = x.astype(bf16)
    r = x - hi.astype(f32)
    mid = r.astype(bf16)
    lo = (r - mid.astype(f32)).astype(bf16)
    return hi, mid, lo


def _mix_in_body(h_ref, g_ref, w_ref, bf_ref, cw_ref, wc_ref, p_init_ref, f_init_ref,
                 ksel_ref, qsel_ref,
                 qx_ref, kx_ref, v_ref, fsum_ref, sga_ref, gcv_ref, p_tail_ref,
                 p_scr, f_carry, *, tm):
    @pl.when(pl.program_id(1) == 0)
    def _():
        p_scr[0:SUBLANES, :] = p_init_ref[...]
        f_carry[...] = f_init_ref[...]

    u = _rms_norm(h_ref[...], g_ref[...]).astype(bf16)

    def proj(lo, width):
        return jnp.dot(u, w_ref[:, lo:lo + width], preferred_element_type=f32)

    q = (proj(C_Q, ATTN_WIDTH) * (LOG2E * HEAD_DIM ** -0.5)).astype(bf16)
    k = proj(C_K, ATTN_WIDTH)
    v = proj(C_V, ATTN_WIDTH)
    lane = lax.broadcasted_iota(jnp.int32, (tm, LANES), 1)
    ones_a = jnp.where(lane == HEAD_DIM, 1.0, 0.0)
    ones_b = jnp.where(lane == 0, 1.0, 0.0)
    for pair in range(ATTN_HEADS // HEADS_PER_BLOCK):
        v2 = v[:, pair * LANES:(pair + 1) * LANES]
        c0 = pair * HEADS_PER_BLOCK * LANES
        kx_ref[:, c0:c0 + LANES] = k[:, pair * LANES:(pair + 1) * LANES].astype(bf16)
        q2 = q[:, pair * LANES:(pair + 1) * LANES]
        zeros = jnp.zeros_like(q2)
        qx_ref[:, 2 * c0:2 * c0 + LANES] = jnp.where(lane < HEAD_DIM, q2, zeros)
        qx_ref[:, 2 * c0 + 2 * LANES:2 * c0 + 3 * LANES] = jnp.where(lane < HEAD_DIM, zeros, q2)
        v_ref[:, c0:c0 + LANES] = jnp.where(lane < HEAD_DIM, v2, ones_a).astype(bf16)
        v_ref[:, c0 + LANES:c0 + 2 * LANES] = jnp.where(lane < HEAD_DIM, ones_b, v2).astype(bf16)

    p = proj(C_CC, CONV_WIDTH) * proj(C_CIN, CONV_WIDTH)
    p_scr[SUBLANES:SUBLANES + tm, :] = p
    conv = (p_scr[SUBLANES - 2:SUBLANES - 2 + tm, :] * cw_ref[0:1, :]
            + p_scr[SUBLANES - 1:SUBLANES - 1 + tm, :] * cw_ref[1:2, :]
            + p * cw_ref[2:3, :])
    tail = p_scr[tm:tm + SUBLANES, :]
    p_scr[0:SUBLANES, :] = tail
    p_tail_ref[...] = tail
    conv_in = (proj(C_CB, CONV_WIDTH) * conv).astype(bf16)
    y_conv = jnp.dot(conv_in, wc_ref[...], preferred_element_type=f32)
    sga_ref[...] = jax.nn.sigmoid(proj(C_GA, D_MODEL)).astype(bf16)
    gcv_ref[...] = (jax.nn.sigmoid(proj(C_GC, D_MODEL)) * y_conv).astype(bf16)

    log_f = _log_sigmoid(proj(C_F, LANES) + bf_ref[...]) * LOG2E
    rb = min(tm, LANES)
    tri = (lax.broadcasted_iota(jnp.int32, (rb, rb), 0)
           >= lax.broadcasted_iota(jnp.int32, (rb, rb), 1)).astype(bf16)
    carry = f_carry[...]
    k_lane = lax.broadcasted_iota(jnp.int32, (rb, ATTN_HEADS // HEADS_PER_BLOCK * LANES), 1)
    k_ones = jnp.bitwise_and(k_lane, LANES - 1) < N_SPLIT
    q_lane = lax.broadcasted_iota(jnp.int32, (rb, ATTN_HEADS * LANES), 1)
    q_head = jnp.right_shift(q_lane, LANES.bit_length() - 1)
    q_ones_lo = N_SPLIT * (1 + jnp.bitwise_and(q_head, HEADS_PER_BLOCK - 1))
    q_in_blk = jnp.bitwise_and(q_lane, LANES - 1)
    q_ones = (q_in_blk >= q_ones_lo) & (q_in_blk < q_ones_lo + N_SPLIT)
    for blk in range(tm // rb):
        rs = slice(blk * rb, (blk + 1) * rb)
        hi, mid, lo = _split3(log_f[rs, :])
        cs = (jnp.dot(tri, hi, preferred_element_type=f32)
              + jnp.dot(tri, mid, preferred_element_type=f32)
              + jnp.dot(tri, lo, preferred_element_type=f32)) + carry
        carry = cs[rb - 1:rb, :]
        fsum_ref[rs, :] = cs[:, :ATTN_HEADS]
        pieces = _split3(cs)
        k_bias = sum(jnp.dot(pc, ksel_ref[t], preferred_element_type=f32)
                     for t, pc in enumerate(pieces))
        k_bias = jnp.where(k_ones, 1.0, k_bias).astype(bf16)
        for pair in range(ATTN_HEADS // HEADS_PER_BLOCK):
            c0 = pair * HEADS_PER_BLOCK * LANES + LANES
            kx_ref[rs, c0:c0 + LANES] = k_bias[:, pair * LANES:(pair + 1) * LANES]
        q_bias = sum(jnp.dot(pc, qsel_ref[t], preferred_element_type=f32)
                     for t, pc in enumerate(pieces))
        q_bias = jnp.where(q_ones, 1.0, q_bias).astype(bf16)
        for head in range(ATTN_HEADS):
            c0 = head * HEADS_PER_BLOCK * LANES + LANES
            qx_ref[rs, c0:c0 + LANES] = q_bias[:, head * LANES:(head + 1) * LANES]
    f_carry[...] = carry


def _bias_selectors():
    ksel = np.zeros((N_SPLIT, LANES, ATTN_HEADS // HEADS_PER_BLOCK * LANES), np.float32)
    qsel = np.zeros((N_SPLIT, LANES, ATTN_HEADS * LANES), np.float32)
    for h in range(ATTN_HEADS):
        for t in range(N_SPLIT):
            col = (h // HEADS_PER_BLOCK) * LANES + N_SPLIT * (1 + h % HEADS_PER_BLOCK) + t
            ksel[t, h, col] = -1.0
            qsel[t, h, h * LANES + t] = 1.0
    return jnp.asarray(ksel, bf16), jnp.asarray(qsel, bf16)


def _mix_in(h, g, w_all, b_f, conv_w, w_conv_branch, p_init, f_init, batch, tm):
    rows = h.shape[0]
    nt = rows // (batch * tm)
    row_blk = lambda width: pl.BlockSpec((tm, width), lambda b, t: (b * nt + t, 0))
    out_rows = lambda width, dt: jax.ShapeDtypeStruct((rows, width), dt)
    return pl.pallas_call(
        functools.partial(_mix_in_body, tm=tm),
        grid=(batch, nt),
        in_specs=[
            row_blk(D_MODEL),
            _resident((1, D_MODEL)),
            _resident((D_MODEL, IN_COLS_PADDED)),
            _resident((1, LANES)),
            _resident((CONV_K, CONV_WIDTH)),
            _resident((CONV_WIDTH, D_MODEL)),
            _resident((SUBLANES, CONV_WIDTH)),
            _resident((1, LANES)),
            _resident((N_SPLIT, LANES, ATTN_HEADS // HEADS_PER_BLOCK * LANES)),
            _resident((N_SPLIT, LANES, ATTN_HEADS * LANES)),
        ],
        out_specs=[
            row_blk(2 * ATTN_HEADS * LANES), row_blk(ATTN_HEADS * LANES),
            row_blk(ATTN_HEADS * LANES),
            row_blk(ATTN_HEADS), row_blk(D_MODEL), row_blk(D_MODEL),
            pl.BlockSpec((SUBLANES, CONV_WIDTH), lambda b, t: (b * nt + t, 0)),
        ],
        out_shape=[
            out_rows(2 * ATTN_HEADS * LANES, bf16), out_rows(ATTN_HEADS * LANES, bf16),
            out_rows(ATTN_HEADS * LANES, bf16),
            out_rows(ATTN_HEADS, f32), out_rows(D_MODEL, bf16), out_rows(D_MODEL, bf16),
            jax.ShapeDtypeStruct((batch * nt * SUBLANES, CONV_WIDTH), f32),
        ],
        scratch_shapes=[pltpu.VMEM((tm + SUBLANES, CONV_WIDTH), f32),
                        pltpu.VMEM((1, LANES), f32)],
        compiler_params=pltpu.CompilerParams(
            dimension_semantics=("arbitrary", "arbitrary"), vmem_limit_bytes=VMEM_LIMIT_BYTES),
        name="mixer_input",
    )(h, g, w_all, b_f, conv_w, w_conv_branch, p_init, f_init, *_bias_selectors())


def _attn_body(q_ref, k_ref, v_ref, km_ref, vm_ref, o_ref,
               m_scr, alpha_scr, acc_scr, s_scr, p_scr, pm_scr, *, tq, tk, rows):
    i = pl.program_id(2)
    low_half = lax.broadcasted_iota(jnp.int32, (tq, LANES), 1) < HEAD_DIM
    head_lanes = HEADS_PER_BLOCK * LANES
    m_scr[...] = jnp.full(m_scr.shape, MASK_VALUE, f32)
    acc_scr[...] = jnp.zeros(acc_scr.shape, f32)
    contract_last = (((1,), (1,)), ((), ()))
    lane = lax.broadcasted_iota(jnp.int32, (rows, LANES), 1)
    sub = lax.broadcasted_iota(jnp.int32, (rows, LANES), 0)
    heads = range(HEADS_PER_BLOCK)

    def scores(hh, k2):
        q_head = q_ref[:, hh * head_lanes:(hh + 1) * head_lanes]
        return lax.dot_general(q_head, k2, contract_last, preferred_element_type=f32)

    def softmax_rows(hh, r0, parts):
        rs = slice(r0, r0 + rows)
        blocks = [blk if valid is None else jnp.where(valid, blk, MASK_VALUE)
                  for blk, valid, _, _ in parts]
        m_old = m_scr[hh, rs, :]
        m_blk = functools.reduce(jnp.maximum, blocks)
        m_new = jnp.maximum(
            m_old, jnp.broadcast_to(jnp.max(m_blk, axis=-1, keepdims=True), (rows, LANES)))
        for blk, (_, _, dst, c0) in zip(blocks, parts):
            dst[hh, rs, c0:c0 + LANES] = jnp.exp2(blk - m_new).astype(bf16)
        alpha_scr[hh, rs, :] = jnp.exp2(m_old - m_new)
        m_scr[hh, rs, :] = m_new

    def accumulate(hh, pv):
        acc_scr[hh] = alpha_scr[hh] * acc_scr[hh] + pv

    k2 = k_ref[pl.ds(0, tk), :]
    for hh in heads:
        s_scr[0, hh] = scores(hh, k2)

    def full_step(j, carry):
        for cur in range(2):
            pl.when(jnp.bitwise_and(j, 1) == cur)(functools.partial(step_body, j, cur, 1 - cur))
        return carry

    def step_body(j, cur, nxt):
        start = pl.multiple_of(j * tk, tk)
        k_next = k_ref[pl.ds(pl.multiple_of(start + tk, tk), tk), :]
        for hh in heads:
            s_scr[nxt, hh] = scores(hh, k_next)
        for hh in heads:
            for r0 in range(0, tq, rows):
                softmax_rows(hh, r0, [
                    (s_scr[cur, hh, r0:r0 + rows, c * LANES:(c + 1) * LANES], None, p_scr, c * LANES)
                    for c in range(tk // LANES)])
            accumulate(hh, jnp.dot(p_scr[hh], v_ref[pl.ds(start, tk), hh * LANES:(hh + 1) * LANES],
                                   preferred_element_type=f32))

    lax.fori_loop(0, i, full_step, 0)

    cur = jnp.bitwise_and(i, 1)
    start = pl.multiple_of(i * tk, tk)
    meta_valid = lane < N_META
    s_meta = [scores(hh, km_ref[...]) for hh in heads]
    for hh in heads:
        for r0 in range(0, tq, rows):
            n_cols = r0 + rows
            parts = [(s_scr[cur, hh, r0:r0 + rows, c * LANES:(c + 1) * LANES],
                      None if (c + 1) * LANES - 1 <= r0 else (lane + c * LANES <= sub + r0),
                      p_scr, c * LANES) for c in range(n_cols // LANES)]
            parts.append((s_meta[hh][r0:r0 + rows, :], meta_valid, pm_scr, 0))
            softmax_rows(hh, r0, parts)
            if n_cols < tk:
                p_scr[hh, r0:r0 + rows, n_cols:tk] = jnp.zeros((rows, tk - n_cols), bf16)
        accumulate(hh, jnp.dot(p_scr[hh], v_ref[pl.ds(start, tk), hh * LANES:(hh + 1) * LANES],
                               preferred_element_type=f32)
                   + jnp.dot(pm_scr[hh], vm_ref[:, hh * LANES:(hh + 1) * LANES],
                             preferred_element_type=f32))

    acc_a, acc_b = acc_scr[0], acc_scr[1]
    out = jnp.where(low_half, acc_a / acc_a[:, HEAD_DIM:HEAD_DIM + 1], acc_b / acc_b[:, 0:1])
    o_ref[...] = out.astype(bf16)


def _attention(qx, kx, v_aug, kx_meta, v_meta, batch, seq, tq, tk, rows):
    nq = seq // tq
    n_blk = ATTN_HEADS // HEADS_PER_BLOCK
    pair_lanes = HEADS_PER_BLOCK * LANES
    q_spec = pl.BlockSpec((tq, HEADS_PER_BLOCK * pair_lanes), lambda b, p, i: (b * nq + i, p))
    o_spec = pl.BlockSpec((tq, LANES), lambda b, p, i: (b * nq + i, p))
    kv_spec = pl.BlockSpec((seq, pair_lanes), lambda b, p, i: (b, p))
    meta_spec = pl.BlockSpec((LANES, pair_lanes), lambda b, p, i: (0, p))
    per_head = lambda width, dt: pltpu.VMEM((HEADS_PER_BLOCK, tq, width), dt)
    return pl.pallas_call(
        functools.partial(_attn_body, tq=tq, tk=tk, rows=rows),
        grid=(batch, n_blk, nq),
        in_specs=[q_spec, kv_spec, kv_spec, meta_spec, meta_spec],
        out_specs=o_spec,
        out_shape=jax.ShapeDtypeStruct((batch * seq, ATTN_WIDTH), bf16),
        scratch_shapes=[per_head(LANES, f32), per_head(LANES, f32), per_head(LANES, f32),
                        pltpu.VMEM((2, HEADS_PER_BLOCK, tq, tk), f32),
                        per_head(tk, bf16), per_head(LANES, bf16)],
        compiler_params=pltpu.CompilerParams(
            dimension_semantics=("arbitrary", "arbitrary", "arbitrary"),
            vmem_limit_bytes=VMEM_LIMIT_BYTES),
        name="forgetting_attention",
    )(qx, kx, v_aug, kx_meta, v_meta)


def _mix_out_body(h_ref, attn_ref, sga_ref, gcv_ref, wa_ref, wo_ref, g_ref, o_ref):
    y_attn = jnp.dot(attn_ref[...], wa_ref[...], preferred_element_type=f32)
    gated = sga_ref[...].astype(f32) * y_attn + gcv_ref[...].astype(f32)
    mixed = jnp.dot(gated.astype(bf16), wo_ref[...], preferred_element_type=f32)
    o_ref[...] = h_ref[...] + _rms_norm(mixed, g_ref[...])


def _mix_out(h, attn, sga, gcv, w_attn_branch, w_out, g, tm):
    rows = h.shape[0]
    row_blk = lambda width: pl.BlockSpec((tm, width), lambda i: (i, 0))
    return pl.pallas_call(
        _mix_out_body,
        grid=(rows // tm,),
        in_specs=[row_blk(D_MODEL), row_blk(ATTN_WIDTH), row_blk(D_MODEL), row_blk(D_MODEL),
                  _resident((ATTN_WIDTH, D_MODEL)), _resident((D_MODEL, D_MODEL)),
                  _resident((1, D_MODEL))],
        out_specs=row_blk(D_MODEL),
        out_shape=jax.ShapeDtypeStruct((rows, D_MODEL), f32),
        compiler_params=pltpu.CompilerParams(
            dimension_semantics=("arbitrary",), vmem_limit_bytes=VMEM_LIMIT_BYTES),
        name="mixer_output",
    )(h, attn, sga, gcv, w_attn_branch, w_out, g)


def kernel(x, meta_tokens, w_in, b_forget, conv_w, w_attn_branch, w_conv_branch, w_out,
           g_ffn1_pre, g_ffn1_post, w_ffn1_in, w_ffn1_out, g_mix_pre, g_mix_post,
           g_ffn2_pre, g_ffn2_post, w_ffn2_in, w_ffn2_out):
    batch, seq, d = x.shape
    assert d == D_MODEL and w_in.shape[0] == 1 and meta_tokens.shape == (N_META, D_MODEL)
    assert seq % ROW_TILE == 0 and seq % ATTN_TQ == 0 and ATTN_TQ == ATTN_TK
    assert ATTN_TQ % ATTN_ROWS == 0 and ATTN_ROWS % LANES == 0

    gain = lambda g: g[0].reshape(1, D_MODEL).astype(f32)
    w = w_in[0]
    n_qkv = 3 * ATTN_WIDTH
    w_all = jnp.concatenate(
        [w[:, :n_qkv], w[:, n_qkv + ATTN_HEADS:], w[:, n_qkv:n_qkv + ATTN_HEADS],
         jnp.zeros((D_MODEL, LANES - ATTN_HEADS), w.dtype)], axis=1).astype(bf16)
    b_f = jnp.pad(b_forget[0].astype(f32), (0, LANES - ATTN_HEADS)).reshape(1, LANES)
    cw = conv_w[0].astype(f32)
    wc = w_conv_branch[0].astype(bf16)
    wa = w_attn_branch[0].astype(bf16)
    wo = w_out[0].astype(bf16)
    w1_in, w1_out = w_ffn1_in[0].astype(bf16), w_ffn1_out[0].astype(bf16)
    w2_in, w2_out = w_ffn2_in[0].astype(bf16), w_ffn2_out[0].astype(bf16)

    hm = _ffn(meta_tokens.astype(f32), gain(g_ffn1_pre), gain(g_ffn1_post), w1_in, w1_out, N_META)
    zeros_p = jnp.zeros((SUBLANES, CONV_WIDTH), f32)
    zeros_f = jnp.zeros((1, LANES), f32)
    _, kxm, vm, fm, _, _, pm_tail = _mix_in(hm, gain(g_mix_pre), w_all, b_f, cw, wc,
                                           zeros_p, zeros_f, 1, N_META)
    pad_rows = lambda a: jnp.pad(a, ((0, LANES - N_META), (0, 0)))
    f_init = jnp.pad(fm[N_META - 1:N_META, :], ((0, 0), (0, LANES - ATTN_HEADS)))

    rows = batch * seq
    h1 = _ffn(x.reshape(rows, D_MODEL), gain(g_ffn1_pre), gain(g_ffn1_post), w1_in, w1_out, ROW_TILE)
    qx, kx, v, _, sga, gcv, _ = _mix_in(h1, gain(g_mix_pre), w_all, b_f, cw, wc,
                                        pm_tail, f_init, batch, ROW_TILE)
    attn = _attention(qx, kx, v, pad_rows(kxm), pad_rows(vm),
                      batch, seq, ATTN_TQ, ATTN_TK, ATTN_ROWS)
    h2 = _mix_out(h1, attn, sga, gcv, wa, wo, gain(g_mix_post), ROW_TILE)
    h3 = _ffn(h2, gain(g_ffn2_pre), gain(g_ffn2_post), w2_in, w2_out, ROW_TILE)
    return h3.reshape(batch, seq, D_MODEL)
```

```python
import functools

import jax
import jax.numpy as jnp
import numpy as np
from jax import lax
from jax.experimental import pallas as pl
from jax.experimental.pallas import tpu as pltpu

D_MODEL = 1024
D_FF = 2816
N_META = 16
ATTN_HEADS = 8
HEAD_DIM = 64
ATTN_WIDTH = ATTN_HEADS * HEAD_DIM
CONV_WIDTH = 512
CONV_K = 3
NORM_EPS = 1e-6

LANES = 128
SUBLANES = 8
HEADS_PER_BLOCK = LANES // HEAD_DIM
VMEM_LIMIT_BYTES = 56 * 1024 * 1024
MASK_VALUE = -1e30
LOG2E = 1.4426950408889634
N_SPLIT = 3

ROW_TILE = 512
FF_CHUNK = 256
ATTN_TQ = 512
ATTN_TK = 512
ATTN_ROWS = 128
ATTN_PAIRS = 2

C_Q, C_K, C_V = 0, 512, 1024
C_CB, C_CC, C_CIN = 1536, 2048, 2560
C_GA, C_GC = 3072, 4096
C_F = 5120
IN_COLS_PADDED = C_F + LANES

bf16 = jnp.bfloat16
f32 = jnp.float32


def _rms_norm(x, g):
    ms = jnp.mean(x * x, axis=-1, keepdims=True)
    return x * lax.rsqrt(ms + NORM_EPS) * g


def _resident(shape):
    nd = len(shape)
    return pl.BlockSpec(shape, lambda *_: (0,) * nd, pipeline_mode=pl.Buffered(1))


def _swiglu_residual(h, g_pre, g_post, w_in_ref, w_out_ref, act_ref):
    u = _rms_norm(h, g_pre).astype(bf16)
    for c in range(D_FF // FF_CHUNK):
        lo = c * FF_CHUNK
        a = jnp.dot(u, w_in_ref[:, lo:lo + FF_CHUNK], preferred_element_type=f32)
        b = jnp.dot(u, w_in_ref[:, D_FF + lo:D_FF + lo + FF_CHUNK], preferred_element_type=f32)
        act_ref[:, lo:lo + FF_CHUNK] = (a * jax.nn.sigmoid(a) * b).astype(bf16)
    y = jnp.dot(act_ref[...], w_out_ref[...], preferred_element_type=f32)
    return h + 0.5 * _rms_norm(y, g_post)


def _ffn_body(h_ref, g_pre_ref, g_post_ref, w_in_ref, w_out_ref, o_ref, act_ref):
    o_ref[...] = _swiglu_residual(h_ref[...], g_pre_ref[...], g_post_ref[...],
                                  w_in_ref, w_out_ref, act_ref)


def _ffn(h, g_pre, g_post, w_in, w_out, tm):
    rows = h.shape[0]
    return pl.pallas_call(
        _ffn_body,
        grid=(rows // tm,),
        in_specs=[
            pl.BlockSpec((tm, D_MODEL), lambda i: (i, 0)),
            _resident((1, D_MODEL)),
            _resident((1, D_MODEL)),
            _resident((D_MODEL, 2 * D_FF)),
            _resident((D_FF, D_MODEL)),
        ],
        out_specs=pl.BlockSpec((tm, D_MODEL), lambda i: (i, 0)),
        out_shape=jax.ShapeDtypeStruct((rows, D_MODEL), f32),
        scratch_shapes=[pltpu.VMEM((tm, D_FF), bf16)],
        compiler_params=pltpu.CompilerParams(
            dimension_semantics=("arbitrary",), vmem_limit_bytes=VMEM_LIMIT_BYTES),
        name="ffn_half_step",
    )(h, g_pre, g_post, w_in, w_out)


def _log_sigmoid(x):
    return jnp.minimum(x, 0.0) - jnp.log1p(jnp.exp(-jnp.abs(x)))


def _split3(x):
    hi = x.astype(bf16)
    r = x - hi.astype(f32)
    mid = r.astype(bf16)
    lo = (r - mid.astype(f32)).astype(bf16)
    return hi, mid, lo


def _pack3(x, head_lanes):
    hi, mid, lo = _split3(jnp.where(head_lanes, x, 0.0))
    packed = (hi.astype(f32) + pltpu.roll(mid.astype(f32), ATTN_HEADS, 1)
              + pltpu.roll(lo.astype(f32), 2 * ATTN_HEADS, 1))
    return packed.astype(bf16)


def _mix_in_body(h_ref, g_ref, w_ref, bf_ref, cw_ref, wc_ref, p_init_ref, f_init_ref,
                 ksel_ref, qsel_ref,
                 qx_ref, kx_ref, v_ref, fsum_ref, sga_ref, gcv_ref, p_tail_ref,
                 p_scr, f_carry, *, tm):
    @pl.when(pl.program_id(1) == 0)
    def _():
        p_scr[0:SUBLANES, :] = p_init_ref[...]
        f_carry[...] = f_init_ref[...]

    u = _rms_norm(h_ref[...], g_ref[...]).astype(bf16)

    def proj(lo, width):
        return jnp.dot(u, w_ref[:, lo:lo + width], preferred_element_type=f32)

    rb = min(tm, LANES)
    n_blk = tm // rb
    row_blk = [slice(blk * rb, (blk + 1) * rb) for blk in range(n_blk)]
    n_pairs = ATTN_HEADS // HEADS_PER_BLOCK
    head_lanes = lax.broadcasted_iota(jnp.int32, (rb, LANES), 1) < ATTN_HEADS
    log_f = _log_sigmoid(proj(C_F, LANES) + bf_ref[...]) * LOG2E

    q = (proj(C_Q, ATTN_WIDTH) * (LOG2E * HEAD_DIM ** -0.5)).astype(bf16)
    lane = lax.broadcasted_iota(jnp.int32, (tm, LANES), 1)
    for pair in range(n_pairs):
        q2 = q[:, pair * LANES:(pair + 1) * LANES]
        zeros = jnp.zeros_like(q2)
        c0 = 2 * pair * HEADS_PER_BLOCK * LANES
        qx_ref[:, c0:c0 + LANES] = jnp.where(lane < HEAD_DIM, q2, zeros)
        qx_ref[:, c0 + 2 * LANES:c0 + 3 * LANES] = jnp.where(lane < HEAD_DIM, zeros, q2)

    tri = (lax.broadcasted_iota(jnp.int32, (rb, rb), 0)
           >= lax.broadcasted_iota(jnp.int32, (rb, rb), 1)).astype(bf16)
    c3 = [jnp.dot(tri, _pack3(log_f[rs, :], head_lanes), preferred_element_type=f32)
          for rs in row_blk]

    k = proj(C_K, ATTN_WIDTH)
    for pair in range(n_pairs):
        c0 = pair * HEADS_PER_BLOCK * LANES
        kx_ref[:, c0:c0 + LANES] = k[:, pair * LANES:(pair + 1) * LANES].astype(bf16)

    carry = f_carry[...]
    pieces = []
    for rs, c in zip(row_blk, c3):
        cs = c + pltpu.roll(c, LANES - ATTN_HEADS, 1) + pltpu.roll(c, LANES - 2 * ATTN_HEADS, 1)
        cs = jnp.where(head_lanes, cs, 0.0) + carry
        carry = cs[rb - 1:rb, :]
        fsum_ref[rs, :] = cs[:, :ATTN_HEADS]
        pieces.append(_pack3(cs, head_lanes))
    f_carry[...] = carry

    v = proj(C_V, ATTN_WIDTH)
    ones_a = jnp.where(lane == HEAD_DIM, 1.0, 0.0)
    ones_b = jnp.where(lane == 0, 1.0, 0.0)
    for pair in range(n_pairs):
        v2 = v[:, pair * LANES:(pair + 1) * LANES]
        c0 = pair * HEADS_PER_BLOCK * LANES
        v_ref[:, c0:c0 + LANES] = jnp.where(lane < HEAD_DIM, v2, ones_a).astype(bf16)
        v_ref[:, c0 + LANES:c0 + 2 * LANES] = jnp.where(lane < HEAD_DIM, ones_b, v2).astype(bf16)

    k_lane = lax.broadcasted_iota(jnp.int32, (rb, n_pairs * LANES), 1)
    k_ones = jnp.bitwise_and(k_lane, LANES - 1) < N_SPLIT
    q_lane = lax.broadcasted_iota(jnp.int32, (rb, ATTN_HEADS * LANES), 1)
    q_head = jnp.right_shift(q_lane, LANES.bit_length() - 1)
    q_ones_lo = N_SPLIT * (1 + jnp.bitwise_and(q_head, HEADS_PER_BLOCK - 1))
    q_in_blk = jnp.bitwise_and(q_lane, LANES - 1)
    q_ones = (q_in_blk >= q_ones_lo) & (q_in_blk < q_ones_lo + N_SPLIT)
    for rs, pc in zip(row_blk, pieces):
        k_bias = jnp.dot(pc, ksel_ref[...], preferred_element_type=f32)
        k_bias = jnp.where(k_ones, 1.0, k_bias).astype(bf16)
        for pair in range(n_pairs):
            c0 = pair * HEADS_PER_BLOCK * LANES + LANES
            kx_ref[rs, c0:c0 + LANES] = k_bias[:, pair * LANES:(pair + 1) * LANES]
        q_bias = jnp.dot(pc, qsel_ref[...], preferred_element_type=f32)
        q_bias = jnp.where(q_ones, 1.0, q_bias).astype(bf16)
        for head in range(ATTN_HEADS):
            c0 = head * HEADS_PER_BLOCK * LANES + LANES
            qx_ref[rs, c0:c0 + LANES] = q_bias[:, head * LANES:(head + 1) * LANES]

    p = proj(C_CC, CONV_WIDTH) * proj(C_CIN, CONV_WIDTH)
    p_scr[SUBLANES:SUBLANES + tm, :] = p
    conv = (p_scr[SUBLANES - 2:SUBLANES - 2 + tm, :] * cw_ref[0:1, :]
            + p_scr[SUBLANES - 1:SUBLANES - 1 + tm, :] * cw_ref[1:2, :]
            + p * cw_ref[2:3, :])
    tail = p_scr[tm:tm + SUBLANES, :]
    p_scr[0:SUBLANES, :] = tail
    p_tail_ref[...] = tail
    conv_in = (proj(C_CB, CONV_WIDTH) * conv).astype(bf16)
    y_conv = jnp.dot(conv_in, wc_ref[...], preferred_element_type=f32)
    sga_ref[...] = jax.nn.sigmoid(proj(C_GA, D_MODEL)).astype(bf16)
    gcv_ref[...] = (jax.nn.sigmoid(proj(C_GC, D_MODEL)) * y_conv).astype(bf16)


def _bias_selectors():
    ksel = np.zeros((LANES, ATTN_HEADS // HEADS_PER_BLOCK * LANES), np.float32)
    qsel = np.zeros((LANES, ATTN_HEADS * LANES), np.float32)
    for h in range(ATTN_HEADS):
        for t in range(N_SPLIT):
            col = (h // HEADS_PER_BLOCK) * LANES + N_SPLIT * (1 + h % HEADS_PER_BLOCK) + t
            ksel[t * ATTN_HEADS + h, col] = -1.0
            qsel[t * ATTN_HEADS + h, h * LANES + t] = 1.0
    return jnp.asarray(ksel, bf16), jnp.asarray(qsel, bf16)


def _mix_in(h, g, w_all, b_f, conv_w, w_conv_branch, p_init, f_init, batch, tm):
    rows = h.shape[0]
    nt = rows // (batch * tm)
    row_blk = lambda width: pl.BlockSpec((tm, width), lambda b, t: (b * nt + t, 0))
    out_rows = lambda width, dt: jax.ShapeDtypeStruct((rows, width), dt)
    return pl.pallas_call(
        functools.partial(_mix_in_body, tm=tm),
        grid=(batch, nt),
        in_specs=[
            row_blk(D_MODEL),
            _resident((1, D_MODEL)),
            _resident((D_MODEL, IN_COLS_PADDED)),
            _resident((1, LANES)),
            _resident((CONV_K, CONV_WIDTH)),
            _resident((CONV_WIDTH, D_MODEL)),
            _resident((SUBLANES, CONV_WIDTH)),
            _resident((1, LANES)),
            _resident((LANES, ATTN_HEADS // HEADS_PER_BLOCK * LANES)),
            _resident((LANES, ATTN_HEADS * LANES)),
        ],
        out_specs=[
            row_blk(2 * ATTN_HEADS * LANES), row_blk(ATTN_HEADS * LANES),
            row_blk(ATTN_HEADS * LANES),
            row_blk(ATTN_HEADS), row_blk(D_MODEL), row_blk(D_MODEL),
            pl.BlockSpec((SUBLANES, CONV_WIDTH), lambda b, t: (b * nt + t, 0)),
        ],
        out_shape=[
            out_rows(2 * ATTN_HEADS * LANES, bf16), out_rows(ATTN_HEADS * LANES, bf16),
            out_rows(ATTN_HEADS * LANES, bf16),
            out_rows(ATTN_HEADS, f32), out_rows(D_MODEL, bf16), out_rows(D_MODEL, bf16),
            jax.ShapeDtypeStruct((batch * nt * SUBLANES, CONV_WIDTH), f32),
        ],
        scratch_shapes=[pltpu.VMEM((tm + SUBLANES, CONV_WIDTH), f32),
                        pltpu.VMEM((1, LANES), f32)],
        compiler_params=pltpu.CompilerParams(
            dimension_semantics=("arbitrary", "arbitrary"), vmem_limit_bytes=VMEM_LIMIT_BYTES),
        name="mixer_input",
    )(h, g, w_all, b_f, conv_w, w_conv_branch, p_init, f_init, *_bias_selectors())


def _attn_body(q_ref, k_ref, v_ref, km_ref, vm_ref, o_ref,
               m_scr, alpha_scr, acc_scr, s_scr, p_scr, pm_scr, *, tq, tk, rows, pairs):
    i = pl.program_id(2)
    low_half = lax.broadcasted_iota(jnp.int32, (tq, LANES), 1) < HEAD_DIM
    head_lanes = HEADS_PER_BLOCK * LANES
    m_scr[...] = jnp.full(m_scr.shape, MASK_VALUE, f32)
    acc_scr[...] = jnp.zeros(acc_scr.shape, f32)
    contract_last = (((1,), (1,)), ((), ()))
    lane = lax.broadcasted_iota(jnp.int32, (rows, LANES), 1)
    sub = lax.broadcasted_iota(jnp.int32, (rows, LANES), 0)
    heads = range(pairs * HEADS_PER_BLOCK)

    def scores(hh, keys_ref, key_rows):
        pair = hh // HEADS_PER_BLOCK
        q_head = q_ref[:, hh * head_lanes:(hh + 1) * head_lanes]
        k_pair = keys_ref[key_rows, pair * head_lanes:(pair + 1) * head_lanes]
        return lax.dot_general(q_head, k_pair, contract_last, preferred_element_type=f32)

    def softmax_rows(hh, r0, parts):
        rs = slice(r0, r0 + rows)
        blocks = [blk if valid is None else jnp.where(valid, blk, MASK_VALUE)
                  for blk, valid, _, _ in parts]
        m_old = m_scr[hh, rs, :]
        m_blk = functools.reduce(jnp.maximum, blocks)
        m_new = jnp.maximum(
            m_old, jnp.broadcast_to(jnp.max(m_blk, axis=-1, keepdims=True), (rows, LANES)))
        for blk, (_, _, dst, c0) in zip(blocks, parts):
            dst[hh, rs, c0:c0 + LANES] = jnp.exp2(blk - m_new).astype(bf16)
        alpha_scr[hh, rs, :] = jnp.exp2(m_old - m_new)
        m_scr[hh, rs, :] = m_new

    def accumulate(hh, pv):
        acc_scr[hh] = alpha_scr[hh] * acc_scr[hh] + pv

    for hh in heads:
        s_scr[0, hh] = scores(hh, k_ref, pl.ds(0, tk))

    def full_step(j, carry):
        for cur in range(2):
            pl.when(jnp.bitwise_and(j, 1) == cur)(functools.partial(step_body, j, cur, 1 - cur))
        return carry

    def step_body(j, cur, nxt):
        start = pl.multiple_of(j * tk, tk)
        next_rows = pl.ds(pl.multiple_of(start + tk, tk), tk)
        for hh in heads:
            s_scr[nxt, hh] = scores(hh, k_ref, next_rows)
            for r0 in range(0, tq, rows):
                softmax_rows(hh, r0, [
                    (s_scr[cur, hh, r0:r0 + rows, c * LANES:(c + 1) * LANES], None, p_scr, c * LANES)
                    for c in range(tk // LANES)])
            accumulate(hh, jnp.dot(p_scr[hh], v_ref[pl.ds(start, tk), hh * LANES:(hh + 1) * LANES],
                                   preferred_element_type=f32))

    lax.fori_loop(0, i, full_step, 0)

    cur = jnp.bitwise_and(i, 1)
    start = pl.multiple_of(i * tk, tk)
    meta_valid = lane < N_META
    s_meta = [scores(hh, km_ref, slice(None)) for hh in heads]
    for hh in heads:
        for r0 in range(0, tq, rows):
            n_cols = r0 + rows
            parts = [(s_scr[cur, hh, r0:r0 + rows, c * LANES:(c + 1) * LANES],
                      None if (c + 1) * LANES - 1 <= r0 else (lane + c * LANES <= sub + r0),
                      p_scr, c * LANES) for c in range(n_cols // LANES)]
            parts.append((s_meta[hh][r0:r0 + rows, :], meta_valid, pm_scr, 0))
            softmax_rows(hh, r0, parts)
            if n_cols < tk:
                p_scr[hh, r0:r0 + rows, n_cols:tk] = jnp.zeros((rows, tk - n_cols), bf16)
        accumulate(hh, jnp.dot(p_scr[hh], v_ref[pl.ds(start, tk), hh * LANES:(hh + 1) * LANES],
                               preferred_element_type=f32)
                   + jnp.dot(pm_scr[hh], vm_ref[:, hh * LANES:(hh + 1) * LANES],
                             preferred_element_type=f32))

    for pair in range(pairs):
        acc_a, acc_b = acc_scr[HEADS_PER_BLOCK * pair], acc_scr[HEADS_PER_BLOCK * pair + 1]
        out = jnp.where(low_half, acc_a / acc_a[:, HEAD_DIM:HEAD_DIM + 1], acc_b / acc_b[:, 0:1])
        o_ref[:, pair * LANES:(pair + 1) * LANES] = out.astype(bf16)


def _attention(qx, kx, v_aug, kx_meta, v_meta, batch, seq, tq, tk, rows, pairs):
    nq = seq // tq
    n_heads = pairs * HEADS_PER_BLOCK
    n_blk = ATTN_HEADS // n_heads
    pair_lanes = n_heads * LANES
    q_spec = pl.BlockSpec((tq, HEADS_PER_BLOCK * pair_lanes), lambda b, p, i: (b * nq + i, p))
    o_spec = pl.BlockSpec((tq, pairs * LANES), lambda b, p, i: (b * nq + i, p))
    kv_spec = pl.BlockSpec((seq, pair_lanes), lambda b, p, i: (b, p))
    meta_spec = pl.BlockSpec((LANES, pair_lanes), lambda b, p, i: (0, p))
    per_head = lambda width, dt: pltpu.VMEM((n_heads, tq, width), dt)
    return pl.pallas_call(
        functools.partial(_attn_body, tq=tq, tk=tk, rows=rows, pairs=pairs),
        grid=(batch, n_blk, nq),
        in_specs=[q_spec, kv_spec, kv_spec, meta_spec, meta_spec],
        out_specs=o_spec,
        out_shape=jax.ShapeDtypeStruct((batch * seq, ATTN_WIDTH), bf16),
        scratch_shapes=[per_head(LANES, f32), per_head(LANES, f32), per_head(LANES, f32),
                        pltpu.VMEM((2, n_heads, tq, tk), f32),
                        per_head(tk, bf16), per_head(LANES, bf16)],
        compiler_params=pltpu.CompilerParams(
            dimension_semantics=("arbitrary", "arbitrary", "arbitrary"),
            vmem_limit_bytes=VMEM_LIMIT_BYTES),
        name="forgetting_attention",
    )(qx, kx, v_aug, kx_meta, v_meta)


def _mix_out_body(h_ref, attn_ref, sga_ref, gcv_ref, wa_ref, wo_ref, g_ref, o_ref):
    y_attn = jnp.dot(attn_ref[...], wa_ref[...], preferred_element_type=f32)
    gated = sga_ref[...].astype(f32) * y_attn + gcv_ref[...].astype(f32)
    mixed = jnp.dot(gated.astype(bf16), wo_ref[...], preferred_element_type=f32)
    o_ref[...] = h_ref[...] + _rms_norm(mixed, g_ref[...])


def _mix_out(h, attn, sga, gcv, w_attn_branch, w_out, g, tm):
    rows = h.shape[0]
    row_blk = lambda width: pl.BlockSpec((tm, width), lambda i: (i, 0))
    return pl.pallas_call(
        _mix_out_body,
        grid=(rows // tm,),
        in_specs=[row_blk(D_MODEL), row_blk(ATTN_WIDTH), row_blk(D_MODEL), row_blk(D_MODEL),
                  _resident((ATTN_WIDTH, D_MODEL)), _resident((D_MODEL, D_MODEL)),
                  _resident((1, D_MODEL))],
        out_specs=row_blk(D_MODEL),
        out_shape=jax.ShapeDtypeStruct((rows, D_MODEL), f32),
        compiler_params=pltpu.CompilerParams(
            dimension_semantics=("arbitrary",), vmem_limit_bytes=VMEM_LIMIT_BYTES),
        name="mixer_output",
    )(h, attn, sga, gcv, w_attn_branch, w_out, g)


def kernel(x, meta_tokens, w_in, b_forget, conv_w, w_attn_branch, w_conv_branch, w_out,
           g_ffn1_pre, g_ffn1_post, w_ffn1_in, w_ffn1_out, g_mix_pre, g_mix_post,
           g_ffn2_pre, g_ffn2_post, w_ffn2_in, w_ffn2_out):
    batch, seq, d = x.shape
    assert d == D_MODEL and w_in.shape[0] == 1 and meta_tokens.shape == (N_META, D_MODEL)
    assert seq % ROW_TILE == 0 and seq % ATTN_TQ == 0 and ATTN_TQ == ATTN_TK
    assert ATTN_TQ % ATTN_ROWS == 0 and ATTN_ROWS % LANES == 0

    gain = lambda g: g[0].reshape(1, D_MODEL).astype(f32)
    w = w_in[0]
    n_qkv = 3 * ATTN_WIDTH
    w_all = jnp.concatenate(
        [w[:, :n_qkv], w[:, n_qkv + ATTN_HEADS:], w[:, n_qkv:n_qkv + ATTN_HEADS],
         jnp.zeros((D_MODEL, LANES - ATTN_HEADS), w.dtype)], axis=1).astype(bf16)
    b_f = jnp.pad(b_forget[0].astype(f32), (0, LANES - ATTN_HEADS)).reshape(1, LANES)
    cw = conv_w[0].astype(f32)
    wc = w_conv_branch[0].astype(bf16)
    wa = w_attn_branch[0].astype(bf16)
    wo = w_out[0].astype(bf16)
    w1_in, w1_out = w_ffn1_in[0].astype(bf16), w_ffn1_out[0].astype(bf16)
    w2_in, w2_out = w_ffn2_in[0].astype(bf16), w_ffn2_out[0].astype(bf16)

    hm = _ffn(meta_tokens.astype(f32), gain(g_ffn1_pre), gain(g_ffn1_post), w1_in, w1_out, N_META)
    zeros_p = jnp.zeros((SUBLANES, CONV_WIDTH), f32)
    zeros_f = jnp.zeros((1, LANES), f32)
    _, kxm, vm, fm, _, _, pm_tail = _mix_in(hm, gain(g_mix_pre), w_all, b_f, cw, wc,
                                           zeros_p, zeros_f, 1, N_META)
    pad_rows = lambda a: jnp.pad(a, ((0, LANES - N_META), (0, 0)))
    f_init = jnp.pad(fm[N_META - 1:N_META, :], ((0, 0), (0, LANES - ATTN_HEADS)))

    rows = batch * seq
    h1 = _ffn(x.reshape(rows, D_MODEL), gain(g_ffn1_pre), gain(g_ffn1_post), w1_in, w1_out, ROW_TILE)
    qx, kx, v, _, sga, gcv, _ = _mix_in(h1, gain(g_mix_pre), w_all, b_f, cw, wc,
                                        pm_tail, f_init, batch, ROW_TILE)
    attn = _attention(qx, kx, v, pad_rows(kxm), pad_rows(vm),
                      batch, seq, ATTN_TQ, ATTN_TK, ATTN_ROWS, ATTN_PAIRS)
    h2 = _mix_out(h1, attn, sga, gcv, wa, wo, gain(g_mix_post), ROW_TILE)
    h3 = _ffn(h2, gain(g_ffn2_pre), gain(g_ffn2_post), w2_in, w2_out, ROW_TILE)
    return h3.reshape(batch, seq, D_MODEL)
```

```python
import functools

import jax
import jax.numpy as jnp
import numpy as np
from jax import lax
from jax.experimental import pallas as pl
from jax.experimental.pallas import tpu as pltpu

D_MODEL = 1024
D_FF = 2816
N_META = 16
ATTN_HEADS = 8
HEAD_DIM = 64
ATTN_WIDTH = ATTN_HEADS * HEAD_DIM
CONV_WIDTH = 512
CONV_K = 3
NORM_EPS = 1e-6

LANES = 128
SUBLANES = 8
HEADS_PER_BLOCK = LANES // HEAD_DIM
VMEM_LIMIT_BYTES = 56 * 1024 * 1024
MASK_VALUE = -1e30
LOG2E = 1.4426950408889634
N_SPLIT = 3

ROW_TILE = 512
FFN_TILE = 1024
FF_CHUNK = 256
ATTN_TQ = 512
ATTN_TK = 512
ATTN_ROWS = 128
ATTN_PAIRS = 2

C_Q, C_K, C_V = 0, 512, 1024
C_CB, C_CC, C_CIN, C_GA, C_GC = 0, 512, 1024, 1536, 2560
N_QKV = 3 * ATTN_WIDTH
N_MIX = 3 * CONV_WIDTH + 2 * D_MODEL

bf16 = jnp.bfloat16
f32 = jnp.float32


def _rms_norm(x, g):
    ms = jnp.mean(x * x, axis=-1, keepdims=True)
    return x * lax.rsqrt(ms + NORM_EPS) * g


def _resident(shape):
    nd = len(shape)
    return pl.BlockSpec(shape, lambda *_: (0,) * nd, pipeline_mode=pl.Buffered(1))


def _swiglu_residual(h, g_pre, g_post, w_in_ref, w_out_ref, act_ref):
    u = _rms_norm(h, g_pre).astype(bf16)
    for c in range(D_FF // FF_CHUNK):
        lo = c * FF_CHUNK
        a = jnp.dot(u, w_in_ref[:, lo:lo + FF_CHUNK], preferred_element_type=f32)
        b = jnp.dot(u, w_in_ref[:, D_FF + lo:D_FF + lo + FF_CHUNK], preferred_element_type=f32)
        act_ref[:, lo:lo + FF_CHUNK] = (a * jax.nn.sigmoid(a) * b).astype(bf16)
    y = jnp.dot(act_ref[...], w_out_ref[...], preferred_element_type=f32)
    return h + 0.5 * _rms_norm(y, g_post)


def _ffn_body(h_ref, g_pre_ref, g_post_ref, w_in_ref, w_out_ref, o_ref, act_ref):
    o_ref[...] = _swiglu_residual(h_ref[...], g_pre_ref[...], g_post_ref[...],
                                  w_in_ref, w_out_ref, act_ref)


def _ffn(h, g_pre, g_post, w_in, w_out, tm):
    rows = h.shape[0]
    return pl.pallas_call(
        _ffn_body,
        grid=(rows // tm,),
        in_specs=[
            pl.BlockSpec((tm, D_MODEL), lambda i: (i, 0)),
            _resident((1, D_MODEL)),
            _resident((1, D_MODEL)),
            _resident((D_MODEL, 2 * D_FF)),
            _resident((D_FF, D_MODEL)),
        ],
        out_specs=pl.BlockSpec((tm, D_MODEL), lambda i: (i, 0)),
        out_shape=jax.ShapeDtypeStruct((rows, D_MODEL), f32),
        scratch_shapes=[pltpu.VMEM((tm, D_FF), bf16)],
        compiler_params=pltpu.CompilerParams(
            dimension_semantics=("arbitrary",), vmem_limit_bytes=VMEM_LIMIT_BYTES),
        name="ffn_half_step",
    )(h, g_pre, g_post, w_in, w_out)


def _log_sigmoid(x):
    return jnp.minimum(x, 0.0) - jnp.log1p(jnp.exp(-jnp.abs(x)))


def _split3(x):
    hi = x.astype(bf16)
    r = x - hi.astype(f32)
    mid = r.astype(bf16)
    lo = (r - mid.astype(f32)).astype(bf16)
    return hi, mid, lo


def _pack3(x, head_lanes):
    hi, mid, lo = _split3(jnp.where(head_lanes, x, 0.0))
    packed = (hi.astype(f32) + pltpu.roll(mid.astype(f32), ATTN_HEADS, 1)
              + pltpu.roll(lo.astype(f32), 2 * ATTN_HEADS, 1))
    return packed.astype(bf16)


def _mix_in_body(h_ref, g_ref, w_qkv_ref, w_mix_ref, w_f_ref, bf_ref, cw_ref, wc_ref, p_init_ref, f_init_ref,
                 ksel_ref, qsel_ref,
                 qx_ref, kx_ref, v_ref, fsum_ref, sga_ref, gcv_ref, p_tail_ref,
                 p_scr, f_carry, *, tm):
    @pl.when(pl.program_id(1) == 0)
    def _():
        p_scr[0:SUBLANES, :] = p_init_ref[...]
        f_carry[...] = f_init_ref[...]

    u = _rms_norm(h_ref[...], g_ref[...]).astype(bf16)

    def proj(w_ref, lo, width):
        return jnp.dot(u, w_ref[:, lo:lo + width], preferred_element_type=f32)

    rb = min(tm, LANES)
    n_blk = tm // rb
    row_blk = [slice(blk * rb, (blk + 1) * rb) for blk in range(n_blk)]
    n_pairs = ATTN_HEADS // HEADS_PER_BLOCK
    head_lanes = lax.broadcasted_iota(jnp.int32, (rb, LANES), 1) < ATTN_HEADS
    log_f = _log_sigmoid(proj(w_f_ref, 0, LANES) + bf_ref[...]) * LOG2E

    q = (proj(w_qkv_ref, C_Q, ATTN_WIDTH) * (LOG2E * HEAD_DIM ** -0.5)).astype(bf16)
    lane = lax.broadcasted_iota(jnp.int32, (tm, LANES), 1)
    for pair in range(n_pairs):
        q2 = q[:, pair * LANES:(pair + 1) * LANES]
        zeros = jnp.zeros_like(q2)
        c0 = 2 * pair * HEADS_PER_BLOCK * LANES
        qx_ref[:, c0:c0 + LANES] = jnp.where(lane < HEAD_DIM, q2, zeros)
        qx_ref[:, c0 + 2 * LANES:c0 + 3 * LANES] = jnp.where(lane < HEAD_DIM, zeros, q2)

    tri = (lax.broadcasted_iota(jnp.int32, (rb, rb), 0)
           >= lax.broadcasted_iota(jnp.int32, (rb, rb), 1)).astype(bf16)
    c3 = [jnp.dot(tri, _pack3(log_f[rs, :], head_lanes), preferred_element_type=f32)
          for rs in row_blk]

    k = proj(w_qkv_ref, C_K, ATTN_WIDTH)
    for pair in range(n_pairs):
        c0 = pair * HEADS_PER_BLOCK * LANES
        kx_ref[:, c0:c0 + LANES] = k[:, pair * LANES:(pair + 1) * LANES].astype(bf16)

    carry = f_carry[...]
    pieces = []
    for rs, c in zip(row_blk, c3):
        cs = c + pltpu.roll(c, LANES - ATTN_HEADS, 1) + pltpu.roll(c, LANES - 2 * ATTN_HEADS, 1)
        cs = jnp.where(head_lanes, cs, 0.0) + carry
        carry = cs[rb - 1:rb, :]
        fsum_ref[rs, :] = cs[:, :ATTN_HEADS]
        pieces.append(_pack3(cs, head_lanes))
    f_carry[...] = carry

    v = proj(w_qkv_ref, C_V, ATTN_WIDTH)
    ones_a = jnp.where(lane == HEAD_DIM, 1.0, 0.0)
    ones_b = jnp.where(lane == 0, 1.0, 0.0)
    for pair in range(n_pairs):
        v2 = v[:, pair * LANES:(pair + 1) * LANES]
        c0 = pair * HEADS_PER_BLOCK * LANES
        v_ref[:, c0:c0 + LANES] = jnp.where(lane < HEAD_DIM, v2, ones_a).astype(bf16)
        v_ref[:, c0 + LANES:c0 + 2 * LANES] = jnp.where(lane < HEAD_DIM, ones_b, v2).astype(bf16)

    k_lane = lax.broadcasted_iota(jnp.int32, (rb, n_pairs * LANES), 1)
    k_ones = jnp.bitwise_and(k_lane, LANES - 1) < N_SPLIT
    q_lane = lax.broadcasted_iota(jnp.int32, (rb, ATTN_HEADS * LANES), 1)
    q_head = jnp.right_shift(q_lane, LANES.bit_length() - 1)
    q_ones_lo = N_SPLIT * (1 + jnp.bitwise_and(q_head, HEADS_PER_BLOCK - 1))
    q_in_blk = jnp.bitwise_and(q_lane, LANES - 1)
    q_ones = (q_in_blk >= q_ones_lo) & (q_in_blk < q_ones_lo + N_SPLIT)
    for rs, pc in zip(row_blk, pieces):
        k_bias = jnp.dot(pc, ksel_ref[...], preferred_element_type=f32)
        k_bias = jnp.where(k_ones, 1.0, k_bias).astype(bf16)
        for pair in range(n_pairs):
            c0 = pair * HEADS_PER_BLOCK * LANES + LANES
            kx_ref[rs, c0:c0 + LANES] = k_bias[:, pair * LANES:(pair + 1) * LANES]
        q_bias = jnp.dot(pc, qsel_ref[...], preferred_element_type=f32)
        q_bias = jnp.where(q_ones, 1.0, q_bias).astype(bf16)
        for head in range(ATTN_HEADS):
            c0 = head * HEADS_PER_BLOCK * LANES + LANES
            qx_ref[rs, c0:c0 + LANES] = q_bias[:, head * LANES:(head + 1) * LANES]

    p = proj(w_mix_ref, C_CC, CONV_WIDTH) * proj(w_mix_ref, C_CIN, CONV_WIDTH)
    p_scr[SUBLANES:SUBLANES + tm, :] = p
    conv = (p_scr[SUBLANES - 2:SUBLANES - 2 + tm, :] * cw_ref[0:1, :]
            + p_scr[SUBLANES - 1:SUBLANES - 1 + tm, :] * cw_ref[1:2, :]
            + p * cw_ref[2:3, :])
    tail = p_scr[tm:tm + SUBLANES, :]
    p_scr[0:SUBLANES, :] = tail
    p_tail_ref[...] = tail
    conv_in = (proj(w_mix_ref, C_CB, CONV_WIDTH) * conv).astype(bf16)
    y_conv = jnp.dot(conv_in, wc_ref[...], preferred_element_type=f32)
    sga_ref[...] = jax.nn.sigmoid(proj(w_mix_ref, C_GA, D_MODEL)).astype(bf16)
    gcv_ref[...] = (jax.nn.sigmoid(proj(w_mix_ref, C_GC, D_MODEL)) * y_conv).astype(bf16)


def _bias_selectors():
    ksel = np.zeros((LANES, ATTN_HEADS // HEADS_PER_BLOCK * LANES), np.float32)
    qsel = np.zeros((LANES, ATTN_HEADS * LANES), np.float32)
    for h in range(ATTN_HEADS):
        for t in range(N_SPLIT):
            col = (h // HEADS_PER_BLOCK) * LANES + N_SPLIT * (1 + h % HEADS_PER_BLOCK) + t
            ksel[t * ATTN_HEADS + h, col] = -1.0
            qsel[t * ATTN_HEADS + h, h * LANES + t] = 1.0
    return jnp.asarray(ksel, bf16), jnp.asarray(qsel, bf16)


def _mix_in(h, g, w_qkv, w_mix, w_f, b_f, conv_w, w_conv_branch, p_init, f_init, batch, tm):
    rows = h.shape[0]
    nt = rows // (batch * tm)
    row_blk = lambda width: pl.BlockSpec((tm, width), lambda b, t: (b * nt + t, 0))
    out_rows = lambda width, dt: jax.ShapeDtypeStruct((rows, width), dt)
    return pl.pallas_call(
        functools.partial(_mix_in_body, tm=tm),
        grid=(batch, nt),
        in_specs=[
            row_blk(D_MODEL),
            _resident((1, D_MODEL)),
            _resident((D_MODEL, N_QKV)),
            _resident((D_MODEL, N_MIX)),
            _resident((D_MODEL, LANES)),
            _resident((1, LANES)),
            _resident((CONV_K, CONV_WIDTH)),
            _resident((CONV_WIDTH, D_MODEL)),
            _resident((SUBLANES, CONV_WIDTH)),
            _resident((1, LANES)),
            _resident((LANES, ATTN_HEADS // HEADS_PER_BLOCK * LANES)),
            _resident((LANES, ATTN_HEADS * LANES)),
        ],
        out_specs=[
            row_blk(2 * ATTN_HEADS * LANES), row_blk(ATTN_HEADS * LANES),
            row_blk(ATTN_HEADS * LANES),
            row_blk(ATTN_HEADS), row_blk(D_MODEL), row_blk(D_MODEL),
            pl.BlockSpec((SUBLANES, CONV_WIDTH), lambda b, t: (b * nt + t, 0)),
        ],
        out_shape=[
            out_rows(2 * ATTN_HEADS * LANES, bf16), out_rows(ATTN_HEADS * LANES, bf16),
            out_rows(ATTN_HEADS * LANES, bf16),
            out_rows(ATTN_HEADS, f32), out_rows(D_MODEL, bf16), out_rows(D_MODEL, bf16),
            jax.ShapeDtypeStruct((batch * nt * SUBLANES, CONV_WIDTH), f32),
        ],
        scratch_shapes=[pltpu.VMEM((tm + SUBLANES, CONV_WIDTH), f32),
                        pltpu.VMEM((1, LANES), f32)],
        compiler_params=pltpu.CompilerParams(
            dimension_semantics=("arbitrary", "arbitrary"), vmem_limit_bytes=VMEM_LIMIT_BYTES),
        name="mixer_input",
    )(h, g, w_qkv, w_mix, w_f, b_f, conv_w, w_conv_branch, p_init, f_init, *_bias_selectors())


def _attn_body(q_ref, k_ref, v_ref, km_ref, vm_ref, o_ref,
               m_scr, alpha_scr, acc_scr, s_scr, p_scr, pm_scr, *, tq, tk, rows, pairs):
    i = pl.program_id(2)
    low_half = lax.broadcasted_iota(jnp.int32, (tq, LANES), 1) < HEAD_DIM
    head_lanes = HEADS_PER_BLOCK * LANES
    m_scr[...] = jnp.full(m_scr.shape, MASK_VALUE, f32)
    acc_scr[...] = jnp.zeros(acc_scr.shape, f32)
    contract_last = (((1,), (1,)), ((), ()))
    lane = lax.broadcasted_iota(jnp.int32, (rows, LANES), 1)
    sub = lax.broadcasted_iota(jnp.int32, (rows, LANES), 0)
    heads = range(pairs * HEADS_PER_BLOCK)

    def scores(hh, keys_ref, key_rows):
        pair = hh // HEADS_PER_BLOCK
        q_head = q_ref[:, hh * head_lanes:(hh + 1) * head_lanes]
        k_pair = keys_ref[key_rows, pair * head_lanes:(pair + 1) * head_lanes]
        return lax.dot_general(q_head, k_pair, contract_last, preferred_element_type=f32)

    def softmax_rows(hh, r0, parts):
        rs = slice(r0, r0 + rows)
        blocks = [blk if valid is None else jnp.where(valid, blk, MASK_VALUE)
                  for blk, valid, _, _ in parts]
        m_old = m_scr[hh, rs, :]
        m_blk = functools.reduce(jnp.maximum, blocks)
        m_new = jnp.maximum(
            m_old, jnp.broadcast_to(jnp.max(m_blk, axis=-1, keepdims=True), (rows, LANES)))
        for blk, (_, _, dst, c0) in zip(blocks, parts):
            dst[hh, rs, c0:c0 + LANES] = jnp.exp2(blk - m_new).astype(bf16)
        alpha_scr[hh, rs, :] = jnp.exp2(m_old - m_new)
        m_scr[hh, rs, :] = m_new

    def accumulate(hh, pv):
        acc_scr[hh] = alpha_scr[hh] * acc_scr[hh] + pv

    for hh in heads:
        s_scr[0, hh] = scores(hh, k_ref, pl.ds(0, tk))

    def full_step(j, carry):
        for cur in range(2):
            pl.when(jnp.bitwise_and(j, 1) == cur)(functools.partial(step_body, j, cur, 1 - cur))
        return carry

    def step_body(j, cur, nxt):
        start = pl.multiple_of(j * tk, tk)
        next_rows = pl.ds(pl.multiple_of(start + tk, tk), tk)
        for hh in heads:
            s_scr[nxt, hh] = scores(hh, k_ref, next_rows)
            for r0 in range(0, tq, rows):
                softmax_rows(hh, r0, [
                    (s_scr[cur, hh, r0:r0 + rows, c * LANES:(c + 1) * LANES], None, p_scr, c * LANES)
                    for c in range(tk // LANES)])
            accumulate(hh, jnp.dot(p_scr[hh], v_ref[pl.ds(start, tk), hh * LANES:(hh + 1) * LANES],
                                   preferred_element_type=f32))

    lax.fori_loop(0, i, full_step, 0)

    cur = jnp.bitwise_and(i, 1)
    start = pl.multiple_of(i * tk, tk)
    meta_valid = lane < N_META
    s_meta = [scores(hh, km_ref, slice(None)) for hh in heads]
    for hh in heads:
        for r0 in range(0, tq, rows):
            n_cols = r0 + rows
            parts = [(s_scr[cur, hh, r0:r0 + rows, c * LANES:(c + 1) * LANES],
                      None if (c + 1) * LANES - 1 <= r0 else (lane + c * LANES <= sub + r0),
                      p_scr, c * LANES) for c in range(n_cols // LANES)]
            parts.append((s_meta[hh][r0:r0 + rows, :], meta_valid, pm_scr, 0))
            softmax_rows(hh, r0, parts)
            if n_cols < tk:
                p_scr[hh, r0:r0 + rows, n_cols:tk] = jnp.zeros((rows, tk - n_cols), bf16)
        accumulate(hh, jnp.dot(p_scr[hh], v_ref[pl.ds(start, tk), hh * LANES:(hh + 1) * LANES],
                               preferred_element_type=f32)
                   + jnp.dot(pm_scr[hh], vm_ref[:, hh * LANES:(hh + 1) * LANES],
                             preferred_element_type=f32))

    for pair in range(pairs):
        acc_a, acc_b = acc_scr[HEADS_PER_BLOCK * pair], acc_scr[HEADS_PER_BLOCK * pair + 1]
        out = jnp.where(low_half, acc_a / acc_a[:, HEAD_DIM:HEAD_DIM + 1], acc_b / acc_b[:, 0:1])
        o_ref[:, pair * LANES:(pair + 1) * LANES] = out.astype(bf16)


def _attention(qx, kx, v_aug, kx_meta, v_meta, batch, seq, tq, tk, rows, pairs):
    nq = seq // tq
    n_heads = pairs * HEADS_PER_BLOCK
    n_blk = ATTN_HEADS // n_heads
    pair_lanes = n_heads * LANES
    q_spec = pl.BlockSpec((tq, HEADS_PER_BLOCK * pair_lanes), lambda b, p, i: (b * nq + i, p))
    o_spec = pl.BlockSpec((tq, pairs * LANES), lambda b, p, i: (b * nq + i, p))
    kv_spec = pl.BlockSpec((seq, pair_lanes), lambda b, p, i: (b, p))
    meta_spec = pl.BlockSpec((LANES, pair_lanes), lambda b, p, i: (0, p))
    per_head = lambda width, dt: pltpu.VMEM((n_heads, tq, width), dt)
    return pl.pallas_call(
        functools.partial(_attn_body, tq=tq, tk=tk, rows=rows, pairs=pairs),
        grid=(batch, n_blk, nq),
        in_specs=[q_spec, kv_spec, kv_spec, meta_spec, meta_spec],
        out_specs=o_spec,
        out_shape=jax.ShapeDtypeStruct((batch * seq, ATTN_WIDTH), bf16),
        scratch_shapes=[per_head(LANES, f32), per_head(LANES, f32), per_head(LANES, f32),
                        pltpu.VMEM((2, n_heads, tq, tk), f32),
                        per_head(tk, bf16), per_head(LANES, bf16)],
        compiler_params=pltpu.CompilerParams(
            dimension_semantics=("arbitrary", "arbitrary", "arbitrary"),
            vmem_limit_bytes=VMEM_LIMIT_BYTES),
        name="forgetting_attention",
    )(qx, kx, v_aug, kx_meta, v_meta)


def _mix_out_ffn_body(h_ref, attn_ref, sga_ref, gcv_ref, wa_ref, wo_ref, g_mix_ref,
                      g_pre_ref, g_post_ref, w_in_ref, w_out_ref, o_ref, act_ref):
    y_attn = jnp.dot(attn_ref[...], wa_ref[...], preferred_element_type=f32)
    gated = sga_ref[...].astype(f32) * y_attn + gcv_ref[...].astype(f32)
    mixed = jnp.dot(gated.astype(bf16), wo_ref[...], preferred_element_type=f32)
    h2 = h_ref[...] + _rms_norm(mixed, g_mix_ref[...])
    o_ref[...] = _swiglu_residual(h2, g_pre_ref[...], g_post_ref[...],
                                  w_in_ref, w_out_ref, act_ref)


def _mix_out_ffn(h, attn, sga, gcv, w_attn_branch, w_out, g_mix, g_pre, g_post, w_in, w_ffn_out, tm):
    rows = h.shape[0]
    row_blk = lambda width: pl.BlockSpec((tm, width), lambda i: (i, 0))
    return pl.pallas_call(
        _mix_out_ffn_body,
        grid=(rows // tm,),
        in_specs=[row_blk(D_MODEL), row_blk(ATTN_WIDTH), row_blk(D_MODEL), row_blk(D_MODEL),
                  _resident((ATTN_WIDTH, D_MODEL)), _resident((D_MODEL, D_MODEL)),
                  _resident((1, D_MODEL)), _resident((1, D_MODEL)), _resident((1, D_MODEL)),
                  _resident((D_MODEL, 2 * D_FF)), _resident((D_FF, D_MODEL))],
        out_specs=row_blk(D_MODEL),
        out_shape=jax.ShapeDtypeStruct((rows, D_MODEL), f32),
        scratch_shapes=[pltpu.VMEM((tm, D_FF), bf16)],
        compiler_params=pltpu.CompilerParams(
            dimension_semantics=("arbitrary",), vmem_limit_bytes=VMEM_LIMIT_BYTES),
        name="mixer_output_ffn",
    )(h, attn, sga, gcv, w_attn_branch, w_out, g_mix, g_pre, g_post, w_in, w_ffn_out)


def kernel(x, meta_tokens, w_in, b_forget, conv_w, w_attn_branch, w_conv_branch, w_out,
           g_ffn1_pre, g_ffn1_post, w_ffn1_in, w_ffn1_out, g_mix_pre, g_mix_post,
           g_ffn2_pre, g_ffn2_post, w_ffn2_in, w_ffn2_out):
    batch, seq, d = x.shape
    assert d == D_MODEL and w_in.shape[0] == 1 and meta_tokens.shape == (N_META, D_MODEL)
    assert seq % ROW_TILE == 0 and seq % FFN_TILE == 0 and seq % ATTN_TQ == 0
    assert ATTN_TQ == ATTN_TK and ATTN_TQ % ATTN_ROWS == 0 and ATTN_ROWS % LANES == 0

    gain = lambda g: g[0].reshape(1, D_MODEL).astype(f32)
    w = w_in[0]
    w_qkv = w[:, :N_QKV].astype(bf16)
    w_mix = w[:, N_QKV + ATTN_HEADS:].astype(bf16)
    w_f = jnp.pad(w[:, N_QKV:N_QKV + ATTN_HEADS].astype(bf16), ((0, 0), (0, LANES - ATTN_HEADS)))
    b_f = jnp.pad(b_forget[0].astype(f32), (0, LANES - ATTN_HEADS)).reshape(1, LANES)
    cw = conv_w[0].astype(f32)
    wc = w_conv_branch[0].astype(bf16)
    wa = w_attn_branch[0].astype(bf16)
    wo = w_out[0].astype(bf16)
    w1_in, w1_out = w_ffn1_in[0].astype(bf16), w_ffn1_out[0].astype(bf16)
    w2_in, w2_out = w_ffn2_in[0].astype(bf16), w_ffn2_out[0].astype(bf16)

    hm = _ffn(meta_tokens.astype(f32), gain(g_ffn1_pre), gain(g_ffn1_post), w1_in, w1_out, N_META)
    zeros_p = jnp.zeros((SUBLANES, CONV_WIDTH), f32)
    zeros_f = jnp.zeros((1, LANES), f32)
    _, kxm, vm, fm, _, _, pm_tail = _mix_in(hm, gain(g_mix_pre), w_qkv, w_mix, w_f, b_f, cw, wc,
                                           zeros_p, zeros_f, 1, N_META)
    pad_rows = lambda a: jnp.pad(a, ((0, LANES - N_META), (0, 0)))
    f_init = jnp.pad(fm[N_META - 1:N_META, :], ((0, 0), (0, LANES - ATTN_HEADS)))

    rows = batch * seq
    h1 = _ffn(x.reshape(rows, D_MODEL), gain(g_ffn1_pre), gain(g_ffn1_post), w1_in, w1_out, FFN_TILE)
    qx, kx, v, _, sga, gcv, _ = _mix_in(h1, gain(g_mix_pre), w_qkv, w_mix, w_f, b_f, cw, wc,
                                        pm_tail, f_init, batch, ROW_TILE)
    attn = _attention(qx, kx, v, pad_rows(kxm), pad_rows(vm),
                      batch, seq, ATTN_TQ, ATTN_TK, ATTN_ROWS, ATTN_PAIRS)
    h3 = _mix_out_ffn(h1, attn, sga, gcv, wa, wo, gain(g_mix_post), gain(g_ffn2_pre),
                      gain(g_ffn2_post), w2_in, w2_out, ROW_TILE)
    return h3.reshape(batch, seq, D_MODEL)
```

```python
import functools

import jax
import jax.numpy as jnp
import numpy as np
from jax import lax
from jax.experimental import pallas as pl
from jax.experimental.pallas import tpu as pltpu

D_MODEL = 1024
D_FF = 2816
N_META = 16
ATTN_HEADS = 8
HEAD_DIM = 64
ATTN_WIDTH = ATTN_HEADS * HEAD_DIM
CONV_WIDTH = 512
CONV_K = 3
NORM_EPS = 1e-6

LANES = 128
SUBLANES = 8
HEADS_PER_BLOCK = LANES // HEAD_DIM
VMEM_LIMIT_BYTES = 56 * 1024 * 1024
MASK_VALUE = -1e30
LOG2E = 1.4426950408889634
N_SPLIT = 3

ROW_TILE = 512
FFN_TILE = 1024
FF_CHUNK = 256
ATTN_TQ = 512
ATTN_TK = 512
ATTN_ROWS = 128
ATTN_PAIRS = 4
KV_BUFFERS = 1

C_Q, C_K, C_V = 0, 512, 1024
C_CB, C_CC, C_CIN, C_GA, C_GC = 0, 512, 1024, 1536, 2560
N_QKV = 3 * ATTN_WIDTH
N_MIX = 3 * CONV_WIDTH + 2 * D_MODEL

bf16 = jnp.bfloat16
f32 = jnp.float32


def _rms_norm(x, g):
    ms = jnp.mean(x * x, axis=-1, keepdims=True)
    return x * lax.rsqrt(ms + NORM_EPS) * g


def _resident(shape):
    nd = len(shape)
    return pl.BlockSpec(shape, lambda *_: (0,) * nd, pipeline_mode=pl.Buffered(1))


def _swiglu_residual(h, g_pre, g_post, w_in_ref, w_out_ref, act_ref):
    u = _rms_norm(h, g_pre).astype(bf16)
    for c in range(D_FF // FF_CHUNK):
        lo = c * FF_CHUNK
        a = jnp.dot(u, w_in_ref[:, lo:lo + FF_CHUNK], preferred_element_type=f32)
        b = jnp.dot(u, w_in_ref[:, D_FF + lo:D_FF + lo + FF_CHUNK], preferred_element_type=f32)
        act_ref[:, lo:lo + FF_CHUNK] = (a * jax.nn.sigmoid(a) * b).astype(bf16)
    y = jnp.dot(act_ref[...], w_out_ref[...], preferred_element_type=f32)
    return h + 0.5 * _rms_norm(y, g_post)


def _ffn_body(h_ref, g_pre_ref, g_post_ref, w_in_ref, w_out_ref, o_ref, act_ref):
    o_ref[...] = _swiglu_residual(h_ref[...], g_pre_ref[...], g_post_ref[...],
                                  w_in_ref, w_out_ref, act_ref)


def _ffn(h, g_pre, g_post, w_in, w_out, tm):
    rows = h.shape[0]
    return pl.pallas_call(
        _ffn_body,
        grid=(rows // tm,),
        in_specs=[
            pl.BlockSpec((tm, D_MODEL), lambda i: (i, 0)),
            _resident((1, D_MODEL)),
            _resident((1, D_MODEL)),
            _resident((D_MODEL, 2 * D_FF)),
            _resident((D_FF, D_MODEL)),
        ],
        out_specs=pl.BlockSpec((tm, D_MODEL), lambda i: (i, 0)),
        out_shape=jax.ShapeDtypeStruct((rows, D_MODEL), f32),
        scratch_shapes=[pltpu.VMEM((tm, D_FF), bf16)],
        compiler_params=pltpu.CompilerParams(
            dimension_semantics=("arbitrary",), vmem_limit_bytes=VMEM_LIMIT_BYTES),
        name="ffn_half_step",
    )(h, g_pre, g_post, w_in, w_out)


def _log_sigmoid(x):
    return jnp.minimum(x, 0.0) - jnp.log1p(jnp.exp(-jnp.abs(x)))


def _split3(x):
    hi = x.astype(bf16)
    r = x - hi.astype(f32)
    mid = r.astype(bf16)
    lo = (r - mid.astype(f32)).astype(bf16)
    return hi, mid, lo


def _pack3(x, head_lanes):
    hi, mid, lo = _split3(jnp.where(head_lanes, x, 0.0))
    packed = (hi.astype(f32) + pltpu.roll(mid.astype(f32), ATTN_HEADS, 1)
              + pltpu.roll(lo.astype(f32), 2 * ATTN_HEADS, 1))
    return packed.astype(bf16)


def _mix_in_body(h_ref, g_ref, w_qkv_ref, w_mix_ref, w_f_ref, bf_ref, cw_ref, wc_ref, p_init_ref, f_init_ref,
                 ksel_ref, qsel_ref,
                 qx_ref, kx_ref, v_ref, fsum_ref, sga_ref, gcv_ref, p_tail_ref,
                 p_scr, f_carry, *, tm):
    @pl.when(pl.program_id(1) == 0)
    def _():
        p_scr[0:SUBLANES, :] = p_init_ref[...]
        f_carry[...] = f_init_ref[...]

    u = _rms_norm(h_ref[...], g_ref[...]).astype(bf16)

    def proj(w_ref, lo, width):
        return jnp.dot(u, w_ref[:, lo:lo + width], preferred_element_type=f32)

    rb = min(tm, LANES)
    n_blk = tm // rb
    row_blk = [slice(blk * rb, (blk + 1) * rb) for blk in range(n_blk)]
    n_pairs = ATTN_HEADS // HEADS_PER_BLOCK
    head_lanes = lax.broadcasted_iota(jnp.int32, (rb, LANES), 1) < ATTN_HEADS
    log_f = _log_sigmoid(proj(w_f_ref, 0, LANES) + bf_ref[...]) * LOG2E

    q = (proj(w_qkv_ref, C_Q, ATTN_WIDTH) * (LOG2E * HEAD_DIM ** -0.5)).astype(bf16)
    lane = lax.broadcasted_iota(jnp.int32, (tm, LANES), 1)
    for pair in range(n_pairs):
        q2 = q[:, pair * LANES:(pair + 1) * LANES]
        zeros = jnp.zeros_like(q2)
        c0 = 2 * pair * HEADS_PER_BLOCK * LANES
        qx_ref[:, c0:c0 + LANES] = jnp.where(lane < HEAD_DIM, q2, zeros)
        qx_ref[:, c0 + 2 * LANES:c0 + 3 * LANES] = jnp.where(lane < HEAD_DIM, zeros, q2)

    tri = (lax.broadcasted_iota(jnp.int32, (rb, rb), 0)
           >= lax.broadcasted_iota(jnp.int32, (rb, rb), 1)).astype(bf16)
    c3 = [jnp.dot(tri, _pack3(log_f[rs, :], head_lanes), preferred_element_type=f32)
          for rs in row_blk]

    k = proj(w_qkv_ref, C_K, ATTN_WIDTH)
    for pair in range(n_pairs):
        c0 = pair * HEADS_PER_BLOCK * LANES
        kx_ref[:, c0:c0 + LANES] = k[:, pair * LANES:(pair + 1) * LANES].astype(bf16)

    carry = f_carry[...]
    pieces = []
    for rs, c in zip(row_blk, c3):
        cs = c + pltpu.roll(c, LANES - ATTN_HEADS, 1) + pltpu.roll(c, LANES - 2 * ATTN_HEADS, 1)
        cs = jnp.where(head_lanes, cs, 0.0) + carry
        carry = cs[rb - 1:rb, :]
        fsum_ref[rs, :] = cs[:, :ATTN_HEADS]
        pieces.append(_pack3(cs, head_lanes))
    f_carry[...] = carry

    v = proj(w_qkv_ref, C_V, ATTN_WIDTH)
    ones_a = jnp.where(lane == HEAD_DIM, 1.0, 0.0)
    ones_b = jnp.where(lane == 0, 1.0, 0.0)
    for pair in range(n_pairs):
        v2 = v[:, pair * LANES:(pair + 1) * LANES]
        c0 = pair * HEADS_PER_BLOCK * LANES
        v_ref[:, c0:c0 + LANES] = jnp.where(lane < HEAD_DIM, v2, ones_a).astype(bf16)
        v_ref[:, c0 + LANES:c0 + 2 * LANES] = jnp.where(lane < HEAD_DIM, ones_b, v2).astype(bf16)

    k_lane = lax.broadcasted_iota(jnp.int32, (rb, n_pairs * LANES), 1)
    k_ones = jnp.bitwise_and(k_lane, LANES - 1) < N_SPLIT
    q_lane = lax.broadcasted_iota(jnp.int32, (rb, ATTN_HEADS * LANES), 1)
    q_head = jnp.right_shift(q_lane, LANES.bit_length() - 1)
    q_ones_lo = N_SPLIT * (1 + jnp.bitwise_and(q_head, HEADS_PER_BLOCK - 1))
    q_in_blk = jnp.bitwise_and(q_lane, LANES - 1)
    q_ones = (q_in_blk >= q_ones_lo) & (q_in_blk < q_ones_lo + N_SPLIT)
    for rs, pc in zip(row_blk, pieces):
        k_bias = jnp.dot(pc, ksel_ref[...], preferred_element_type=f32)
        k_bias = jnp.where(k_ones, 1.0, k_bias).astype(bf16)
        for pair in range(n_pairs):
            c0 = pair * HEADS_PER_BLOCK * LANES + LANES
            kx_ref[rs, c0:c0 + LANES] = k_bias[:, pair * LANES:(pair + 1) * LANES]
        q_bias = jnp.dot(pc, qsel_ref[...], preferred_element_type=f32)
        q_bias = jnp.where(q_ones, 1.0, q_bias).astype(bf16)
        for head in range(ATTN_HEADS):
            c0 = head * HEADS_PER_BLOCK * LANES + LANES
            qx_ref[rs, c0:c0 + LANES] = q_bias[:, head * LANES:(head + 1) * LANES]

    p = proj(w_mix_ref, C_CC, CONV_WIDTH) * proj(w_mix_ref, C_CIN, CONV_WIDTH)
    p_scr[SUBLANES:SUBLANES + tm, :] = p
    conv = (p_scr[SUBLANES - 2:SUBLANES - 2 + tm, :] * cw_ref[0:1, :]
            + p_scr[SUBLANES - 1:SUBLANES - 1 + tm, :] * cw_ref[1:2, :]
            + p * cw_ref[2:3, :])
    tail = p_scr[tm:tm + SUBLANES, :]
    p_scr[0:SUBLANES, :] = tail
    p_tail_ref[...] = tail
    conv_in = (proj(w_mix_ref, C_CB, CONV_WIDTH) * conv).astype(bf16)
    y_conv = jnp.dot(conv_in, wc_ref[...], preferred_element_type=f32)
    sga_ref[...] = jax.nn.sigmoid(proj(w_mix_ref, C_GA, D_MODEL)).astype(bf16)
    gcv_ref[...] = (jax.nn.sigmoid(proj(w_mix_ref, C_GC, D_MODEL)) * y_conv).astype(bf16)


def _bias_selectors():
    ksel = np.zeros((LANES, ATTN_HEADS // HEADS_PER_BLOCK * LANES), np.float32)
    qsel = np.zeros((LANES, ATTN_HEADS * LANES), np.float32)
    for h in range(ATTN_HEADS):
        for t in range(N_SPLIT):
            col = (h // HEADS_PER_BLOCK) * LANES + N_SPLIT * (1 + h % HEADS_PER_BLOCK) + t
            ksel[t * ATTN_HEADS + h, col] = -1.0
            qsel[t * ATTN_HEADS + h, h * LANES + t] = 1.0
    return jnp.asarray(ksel, bf16), jnp.asarray(qsel, bf16)


def _mix_in(h, g, w_qkv, w_mix, w_f, b_f, conv_w, w_conv_branch, p_init, f_init, batch, tm):
    rows = h.shape[0]
    nt = rows // (batch * tm)
    row_blk = lambda width: pl.BlockSpec((tm, width), lambda b, t: (b * nt + t, 0))
    out_rows = lambda width, dt: jax.ShapeDtypeStruct((rows, width), dt)
    return pl.pallas_call(
        functools.partial(_mix_in_body, tm=tm),
        grid=(batch, nt),
        in_specs=[
            row_blk(D_MODEL),
            _resident((1, D_MODEL)),
            _resident((D_MODEL, N_QKV)),
            _resident((D_MODEL, N_MIX)),
            _resident((D_MODEL, LANES)),
            _resident((1, LANES)),
            _resident((CONV_K, CONV_WIDTH)),
            _resident((CONV_WIDTH, D_MODEL)),
            _resident((SUBLANES, CONV_WIDTH)),
            _resident((1, LANES)),
            _resident((LANES, ATTN_HEADS // HEADS_PER_BLOCK * LANES)),
            _resident((LANES, ATTN_HEADS * LANES)),
        ],
        out_specs=[
            row_blk(2 * ATTN_HEADS * LANES), row_blk(ATTN_HEADS * LANES),
            row_blk(ATTN_HEADS * LANES),
            row_blk(ATTN_HEADS), row_blk(D_MODEL), row_blk(D_MODEL),
            pl.BlockSpec((SUBLANES, CONV_WIDTH), lambda b, t: (b * nt + t, 0)),
        ],
        out_shape=[
            out_rows(2 * ATTN_HEADS * LANES, bf16), out_rows(ATTN_HEADS * LANES, bf16),
            out_rows(ATTN_HEADS * LANES, bf16),
            out_rows(ATTN_HEADS, f32), out_rows(D_MODEL, bf16), out_rows(D_MODEL, bf16),
            jax.ShapeDtypeStruct((batch * nt * SUBLANES, CONV_WIDTH), f32),
        ],
        scratch_shapes=[pltpu.VMEM((tm + SUBLANES, CONV_WIDTH), f32),
                        pltpu.VMEM((1, LANES), f32)],
        compiler_params=pltpu.CompilerParams(
            dimension_semantics=("arbitrary", "arbitrary"), vmem_limit_bytes=VMEM_LIMIT_BYTES),
        name="mixer_input",
    )(h, g, w_qkv, w_mix, w_f, b_f, conv_w, w_conv_branch, p_init, f_init, *_bias_selectors())


def _attn_body(q_ref, k_ref, v_ref, km_ref, vm_ref, o_ref,
               m_scr, alpha_scr, acc_scr, s_scr, sm_scr, p_scr, pm_scr, *, tq, tk, rows, pairs):
    i = pl.program_id(2)
    low_half = lax.broadcasted_iota(jnp.int32, (tq, LANES), 1) < HEAD_DIM
    head_lanes = HEADS_PER_BLOCK * LANES
    m_scr[...] = jnp.full(m_scr.shape, MASK_VALUE, f32)
    acc_scr[...] = jnp.zeros(acc_scr.shape, f32)
    contract_last = (((1,), (1,)), ((), ()))
    lane = lax.broadcasted_iota(jnp.int32, (rows, LANES), 1)
    sub = lax.broadcasted_iota(jnp.int32, (rows, LANES), 0)
    n_heads = pairs * HEADS_PER_BLOCK
    assert n_heads % 2 == 0

    def scores(hh, keys_ref, key_rows):
        pair = hh // HEADS_PER_BLOCK
        q_head = q_ref[:, hh * head_lanes:(hh + 1) * head_lanes]
        k_pair = keys_ref[key_rows, pair * head_lanes:(pair + 1) * head_lanes]
        return lax.dot_general(q_head, k_pair, contract_last, preferred_element_type=f32)

    def softmax_rows(hh, r0, parts):
        rs = slice(r0, r0 + rows)
        blocks = [blk if valid is None else jnp.where(valid, blk, MASK_VALUE)
                  for blk, valid, _, _ in parts]
        m_old = m_scr[hh, rs, :]
        m_blk = functools.reduce(jnp.maximum, blocks)
        m_new = jnp.maximum(
            m_old, jnp.broadcast_to(jnp.max(m_blk, axis=-1, keepdims=True), (rows, LANES)))
        for blk, (_, _, dst, c0) in zip(blocks, parts):
            dst[hh % 2, rs, c0:c0 + LANES] = jnp.exp2(blk - m_new).astype(bf16)
        alpha_scr[hh, rs, :] = jnp.exp2(m_old - m_new)
        m_scr[hh, rs, :] = m_new

    def accumulate(hh, pv):
        acc_scr[hh] = alpha_scr[hh] * acc_scr[hh] + pv

    s_scr[0] = scores(0, k_ref, pl.ds(0, tk))

    def full_step(j, carry):
        start = pl.multiple_of(j * tk, tk)
        for hh in range(n_heads):
            slot = hh % 2
            if hh + 1 < n_heads:
                s_scr[1 - slot] = scores(hh + 1, k_ref, pl.ds(start, tk))
            else:
                s_scr[1 - slot] = scores(0, k_ref, pl.ds(pl.multiple_of(start + tk, tk), tk))
            for r0 in range(0, tq, rows):
                softmax_rows(hh, r0, [
                    (s_scr[slot, r0:r0 + rows, c * LANES:(c + 1) * LANES], None, p_scr, c * LANES)
                    for c in range(tk // LANES)])
            accumulate(hh, jnp.dot(p_scr[slot], v_ref[pl.ds(start, tk), hh * LANES:(hh + 1) * LANES],
                                   preferred_element_type=f32))
        return carry

    lax.fori_loop(0, i, full_step, 0)

    start = pl.multiple_of(i * tk, tk)
    meta_valid = lane < N_META
    sm_scr[0] = scores(0, km_ref, slice(None))
    for hh in range(n_heads):
        slot = hh % 2
        if hh + 1 < n_heads:
            s_scr[1 - slot] = scores(hh + 1, k_ref, pl.ds(start, tk))
            sm_scr[1 - slot] = scores(hh + 1, km_ref, slice(None))
        for r0 in range(0, tq, rows):
            n_cols = r0 + rows
            parts = [(s_scr[slot, r0:r0 + rows, c * LANES:(c + 1) * LANES],
                      None if (c + 1) * LANES - 1 <= r0 else (lane + c * LANES <= sub + r0),
                      p_scr, c * LANES) for c in range(n_cols // LANES)]
            parts.append((sm_scr[slot, r0:r0 + rows, :], meta_valid, pm_scr, 0))
            softmax_rows(hh, r0, parts)
            if n_cols < tk:
                p_scr[slot, r0:r0 + rows, n_cols:tk] = jnp.zeros((rows, tk - n_cols), bf16)
        accumulate(hh, jnp.dot(p_scr[slot], v_ref[pl.ds(start, tk), hh * LANES:(hh + 1) * LANES],
                               preferred_element_type=f32)
                   + jnp.dot(pm_scr[slot], vm_ref[:, hh * LANES:(hh + 1) * LANES],
                             preferred_element_type=f32))

    for pair in range(pairs):
        acc_a, acc_b = acc_scr[HEADS_PER_BLOCK * pair], acc_scr[HEADS_PER_BLOCK * pair + 1]
        out = jnp.where(low_half, acc_a / acc_a[:, HEAD_DIM:HEAD_DIM + 1], acc_b / acc_b[:, 0:1])
        o_ref[:, pair * LANES:(pair + 1) * LANES] = out.astype(bf16)


def _attention(qx, kx, v_aug, kx_meta, v_meta, batch, seq, tq, tk, rows, pairs):
    nq = seq // tq
    n_heads = pairs * HEADS_PER_BLOCK
    n_blk = ATTN_HEADS // n_heads
    pair_lanes = n_heads * LANES
    q_spec = pl.BlockSpec((tq, HEADS_PER_BLOCK * pair_lanes), lambda b, p, i: (b * nq + i, p))
    o_spec = pl.BlockSpec((tq, pairs * LANES), lambda b, p, i: (b * nq + i, p))
    kv_spec = pl.BlockSpec((seq, pair_lanes), lambda b, p, i: (b, p),
                           pipeline_mode=pl.Buffered(KV_BUFFERS))
    meta_spec = pl.BlockSpec((LANES, pair_lanes), lambda b, p, i: (0, p))
    per_head = lambda width, dt: pltpu.VMEM((n_heads, tq, width), dt)
    ring = lambda width, dt: pltpu.VMEM((2, tq, width), dt)
    return pl.pallas_call(
        functools.partial(_attn_body, tq=tq, tk=tk, rows=rows, pairs=pairs),
        grid=(batch, n_blk, nq),
        in_specs=[q_spec, kv_spec, kv_spec, meta_spec, meta_spec],
        out_specs=o_spec,
        out_shape=jax.ShapeDtypeStruct((batch * seq, ATTN_WIDTH), bf16),
        scratch_shapes=[per_head(LANES, f32), per_head(LANES, f32), per_head(LANES, f32),
                        ring(tk, f32), ring(LANES, f32), ring(tk, bf16), ring(LANES, bf16)],
        compiler_params=pltpu.CompilerParams(
            dimension_semantics=("arbitrary", "arbitrary", "arbitrary"),
            vmem_limit_bytes=VMEM_LIMIT_BYTES),
        name="forgetting_attention",
    )(qx, kx, v_aug, kx_meta, v_meta)


def _mix_out_ffn_body(h_ref, attn_ref, sga_ref, gcv_ref, wa_ref, wo_ref, g_mix_ref,
                      g_pre_ref, g_post_ref, w_in_ref, w_out_ref, o_ref, act_ref):
    y_attn = jnp.dot(attn_ref[...], wa_ref[...], preferred_element_type=f32)
    gated = sga_ref[...].astype(f32) * y_attn + gcv_ref[...].astype(f32)
    mixed = jnp.dot(gated.astype(bf16), wo_ref[...], preferred_element_type=f32)
    h2 = h_ref[...] + _rms_norm(mixed, g_mix_ref[...])
    o_ref[...] = _swiglu_residual(h2, g_pre_ref[...], g_post_ref[...],
                                  w_in_ref, w_out_ref, act_ref)


def _mix_out_ffn(h, attn, sga, gcv, w_attn_branch, w_out, g_mix, g_pre, g_post, w_in, w_ffn_out, tm):
    rows = h.shape[0]
    row_blk = lambda width: pl.BlockSpec((tm, width), lambda i: (i, 0))
    return pl.pallas_call(
        _mix_out_ffn_body,
        grid=(rows // tm,),
        in_specs=[row_blk(D_MODEL), row_blk(ATTN_WIDTH), row_blk(D_MODEL), row_blk(D_MODEL),
                  _resident((ATTN_WIDTH, D_MODEL)), _resident((D_MODEL, D_MODEL)),
                  _resident((1, D_MODEL)), _resident((1, D_MODEL)), _resident((1, D_MODEL)),
                  _resident((D_MODEL, 2 * D_FF)), _resident((D_FF, D_MODEL))],
        out_specs=row_blk(D_MODEL),
        out_shape=jax.ShapeDtypeStruct((rows, D_MODEL), f32),
        scratch_shapes=[pltpu.VMEM((tm, D_FF), bf16)],
        compiler_params=pltpu.CompilerParams(
            dimension_semantics=("arbitrary",), vmem_limit_bytes=VMEM_LIMIT_BYTES),
        name="mixer_output_ffn",
    )(h, attn, sga, gcv, w_attn_branch, w_out, g_mix, g_pre, g_post, w_in, w_ffn_out)


def kernel(x, meta_tokens, w_in, b_forget, conv_w, w_attn_branch, w_conv_branch, w_out,
           g_ffn1_pre, g_ffn1_post, w_ffn1_in, w_ffn1_out, g_mix_pre, g_mix_post,
           g_ffn2_pre, g_ffn2_post, w_ffn2_in, w_ffn2_out):
    batch, seq, d = x.shape
    assert d == D_MODEL and w_in.shape[0] == 1 and meta_tokens.shape == (N_META, D_MODEL)
    assert seq % ROW_TILE == 0 and seq % FFN_TILE == 0 and seq % ATTN_TQ == 0
    assert ATTN_TQ == ATTN_TK and ATTN_TQ % ATTN_ROWS == 0 and ATTN_ROWS % LANES == 0

    gain = lambda g: g[0].reshape(1, D_MODEL).astype(f32)
    w = w_in[0].astype(bf16)
    w_qkv = w[:, :N_QKV]
    w_mix = w[:, N_QKV + ATTN_HEADS:]
    w_f = jnp.pad(w[:, N_QKV:N_QKV + ATTN_HEADS], ((0, 0), (0, LANES - ATTN_HEADS)))
    b_f = jnp.pad(b_forget[0].astype(f32), (0, LANES - ATTN_HEADS)).reshape(1, LANES)
    cw = conv_w[0].astype(f32)
    wc = w_conv_branch[0].astype(bf16)
    wa = w_attn_branch[0].astype(bf16)
    wo = w_out[0].astype(bf16)
    w1_in, w1_out = w_ffn1_in[0].astype(bf16), w_ffn1_out[0].astype(bf16)
    w2_in, w2_out = w_ffn2_in[0].astype(bf16), w_ffn2_out[0].astype(bf16)

    hm = _ffn(meta_tokens.astype(f32), gain(g_ffn1_pre), gain(g_ffn1_post), w1_in, w1_out, N_META)
    zeros_p = jnp.zeros((SUBLANES, CONV_WIDTH), f32)
    zeros_f = jnp.zeros((1, LANES), f32)
    _, kxm, vm, fm, _, _, pm_tail = _mix_in(hm, gain(g_mix_pre), w_qkv, w_mix, w_f, b_f, cw, wc,
                                           zeros_p, zeros_f, 1, N_META)
    pad_rows = lambda a: jnp.pad(a, ((0, LANES - N_META), (0, 0)))
    f_init = jnp.pad(fm[N_META - 1:N_META, :], ((0, 0), (0, LANES - ATTN_HEADS)))

    rows = batch * seq
    h1 = _ffn(x.reshape(rows, D_MODEL), gain(g_ffn1_pre), gain(g_ffn1_post), w1_in, w1_out, FFN_TILE)
    qx, kx, v, _, sga, gcv, _ = _mix_in(h1, gain(g_mix_pre), w_qkv, w_mix, w_f, b_f, cw, wc,
                                        pm_tail, f_init, batch, ROW_TILE)
    attn = _attention(qx, kx, v, pad_rows(kxm), pad_rows(vm),
                      batch, seq, ATTN_TQ, ATTN_TK, ATTN_ROWS, ATTN_PAIRS)
    h3 = _mix_out_ffn(h1, attn, sga, gcv, wa, wo, gain(g_mix_post), gain(g_ffn2_pre),
                      gain(g_ffn2_post), w2_in, w2_out, ROW_TILE)
    return h3.reshape(batch, seq, D_MODEL)
```

```python
import functools

import jax
import jax.numpy as jnp
import numpy as np
from jax import lax
from jax.experimental import pallas as pl
from jax.experimental.pallas import tpu as pltpu

D_MODEL = 1024
D_FF = 2816
N_META = 16
ATTN_HEADS = 8
HEAD_DIM = 64
ATTN_WIDTH = ATTN_HEADS * HEAD_DIM
CONV_WIDTH = 512
CONV_K = 3
NORM_EPS = 1e-6

LANES = 128
SUBLANES = 8
HEADS_PER_BLOCK = LANES // HEAD_DIM
VMEM_LIMIT_BYTES = 56 * 1024 * 1024
MASK_VALUE = -1e30
LOG2E = 1.4426950408889634
N_SPLIT = 3

ROW_TILE = 512
FFN_TILE = 1024
FF_CHUNK = 256
ROW_GROUPS = 2
ATTN_TQ = 512
ATTN_TK = 512
ATTN_ROWS = 128
ATTN_PAIRS = 4
KV_BUFFERS = 1

C_Q, C_K, C_V = 0, 512, 1024
C_CB, C_CC, C_CIN, C_GA, C_GC = 0, 512, 1024, 1536, 2560
N_QKV = 3 * ATTN_WIDTH
N_MIX = 3 * CONV_WIDTH + 2 * D_MODEL

bf16 = jnp.bfloat16
f32 = jnp.float32


def _rms_norm(x, g):
    ms = jnp.mean(x * x, axis=-1, keepdims=True)
    return x * lax.rsqrt(ms + NORM_EPS) * g


def _resident(shape):
    nd = len(shape)
    return pl.BlockSpec(shape, lambda *_: (0,) * nd, pipeline_mode=pl.Buffered(1))


def _row_groups(tm):
    n = ROW_GROUPS if tm % (ROW_GROUPS * 2 * SUBLANES) == 0 else 1
    return [slice(g * tm // n, (g + 1) * tm // n) for g in range(n)]


def _swiglu_up(u, rs, w_in_ref, act_ref):
    for c in range(D_FF // FF_CHUNK):
        lo = c * FF_CHUNK
        a = jnp.dot(u, w_in_ref[:, lo:lo + FF_CHUNK], preferred_element_type=f32)
        b = jnp.dot(u, w_in_ref[:, D_FF + lo:D_FF + lo + FF_CHUNK], preferred_element_type=f32)
        act_ref[rs, lo:lo + FF_CHUNK] = (a * jax.nn.sigmoid(a) * b).astype(bf16)


def _swiglu_groups(residuals, g_pre, g_post, w_in_ref, w_out_ref, act_ref, o_ref, groups):
    h = [None] * len(groups)
    u = [None] * len(groups)
    for g, rs in enumerate(groups + [None]):
        if rs is not None:
            h[g] = residuals[g]()
            u[g] = _rms_norm(h[g], g_pre).astype(bf16)
        if g > 0:
            y = jnp.dot(act_ref[groups[g - 1], :], w_out_ref[...], preferred_element_type=f32)
        if rs is not None:
            _swiglu_up(u[g], rs, w_in_ref, act_ref)
        if g > 0:
            o_ref[groups[g - 1], :] = h[g - 1] + 0.5 * _rms_norm(y, g_post)


def _ffn_body(h_ref, g_pre_ref, g_post_ref, w_in_ref, w_out_ref, o_ref, act_ref):
    groups = _row_groups(h_ref.shape[0])
    residuals = [functools.partial(lambda rs: h_ref[rs, :], rs) for rs in groups]
    _swiglu_groups(residuals, g_pre_ref[...], g_post_ref[...], w_in_ref, w_out_ref, act_ref,
                   o_ref, groups)


def _ffn(h, g_pre, g_post, w_in, w_out, tm):
    rows = h.shape[0]
    return pl.pallas_call(
        _ffn_body,
        grid=(rows // tm,),
        in_specs=[
            pl.BlockSpec((tm, D_MODEL), lambda i: (i, 0)),
            _resident((1, D_MODEL)),
            _resident((1, D_MODEL)),
            _resident((D_MODEL, 2 * D_FF)),
            _resident((D_FF, D_MODEL)),
        ],
        out_specs=pl.BlockSpec((tm, D_MODEL), lambda i: (i, 0)),
        out_shape=jax.ShapeDtypeStruct((rows, D_MODEL), f32),
        scratch_shapes=[pltpu.VMEM((tm, D_FF), bf16)],
        compiler_params=pltpu.CompilerParams(
            dimension_semantics=("arbitrary",), vmem_limit_bytes=VMEM_LIMIT_BYTES),
        name="ffn_half_step",
    )(h, g_pre, g_post, w_in, w_out)


def _log_sigmoid(x):
    return jnp.minimum(x, 0.0) - jnp.log1p(jnp.exp(-jnp.abs(x)))


def _split3(x):
    hi = x.astype(bf16)
    r = x - hi.astype(f32)
    mid = r.astype(bf16)
    lo = (r - mid.astype(f32)).astype(bf16)
    return hi, mid, lo


def _pack3(x, head_lanes):
    hi, mid, lo = _split3(jnp.where(head_lanes, x, 0.0))
    packed = (hi.astype(f32) + pltpu.roll(mid.astype(f32), ATTN_HEADS, 1)
              + pltpu.roll(lo.astype(f32), 2 * ATTN_HEADS, 1))
    return packed.astype(bf16)


def _mix_in_body(h_ref, g_ref, w_qkv_ref, w_mix_ref, w_f_ref, bf_ref, cw_ref, wc_ref, p_init_ref, f_init_ref,
                 ksel_ref, qsel_ref,
                 qx_ref, kx_ref, v_ref, fsum_ref, sga_ref, gcv_ref, p_tail_ref,
                 p_scr, f_carry, *, tm):
    @pl.when(pl.program_id(1) == 0)
    def _():
        p_scr[0:SUBLANES, :] = p_init_ref[...]
        f_carry[...] = f_init_ref[...]

    u = _rms_norm(h_ref[...], g_ref[...]).astype(bf16)

    def proj(w_ref, lo, width):
        return jnp.dot(u, w_ref[:, lo:lo + width], preferred_element_type=f32)

    rb = min(tm, LANES)
    n_blk = tm // rb
    row_blk = [slice(blk * rb, (blk + 1) * rb) for blk in range(n_blk)]
    n_pairs = ATTN_HEADS // HEADS_PER_BLOCK
    head_lanes = lax.broadcasted_iota(jnp.int32, (rb, LANES), 1) < ATTN_HEADS
    log_f = _log_sigmoid(proj(w_f_ref, 0, LANES) + bf_ref[...]) * LOG2E

    q = (proj(w_qkv_ref, C_Q, ATTN_WIDTH) * (LOG2E * HEAD_DIM ** -0.5)).astype(bf16)
    lane = lax.broadcasted_iota(jnp.int32, (tm, LANES), 1)
    for pair in range(n_pairs):
        q2 = q[:, pair * LANES:(pair + 1) * LANES]
        zeros = jnp.zeros_like(q2)
        c0 = 2 * pair * HEADS_PER_BLOCK * LANES
        qx_ref[:, c0:c0 + LANES] = jnp.where(lane < HEAD_DIM, q2, zeros)
        qx_ref[:, c0 + 2 * LANES:c0 + 3 * LANES] = jnp.where(lane < HEAD_DIM, zeros, q2)

    tri = (lax.broadcasted_iota(jnp.int32, (rb, rb), 0)
           >= lax.broadcasted_iota(jnp.int32, (rb, rb), 1)).astype(bf16)
    c3 = [jnp.dot(tri, _pack3(log_f[rs, :], head_lanes), preferred_element_type=f32)
          for rs in row_blk]

    k = proj(w_qkv_ref, C_K, ATTN_WIDTH)
    for pair in range(n_pairs):
        c0 = pair * HEADS_PER_BLOCK * LANES
        kx_ref[:, c0:c0 + LANES] = k[:, pair * LANES:(pair + 1) * LANES].astype(bf16)

    carry = f_carry[...]
    pieces = []
    for rs, c in zip(row_blk, c3):
        cs = c + pltpu.roll(c, LANES - ATTN_HEADS, 1) + pltpu.roll(c, LANES - 2 * ATTN_HEADS, 1)
        cs = jnp.where(head_lanes, cs, 0.0) + carry
        carry = cs[rb - 1:rb, :]
        fsum_ref[rs, :] = cs[:, :ATTN_HEADS]
        pieces.append(_pack3(cs, head_lanes))
    f_carry[...] = carry

    v = proj(w_qkv_ref, C_V, ATTN_WIDTH)
    ones_a = jnp.where(lane == HEAD_DIM, 1.0, 0.0)
    ones_b = jnp.where(lane == 0, 1.0, 0.0)
    for pair in range(n_pairs):
        v2 = v[:, pair * LANES:(pair + 1) * LANES]
        c0 = pair * HEADS_PER_BLOCK * LANES
        v_ref[:, c0:c0 + LANES] = jnp.where(lane < HEAD_DIM, v2, ones_a).astype(bf16)
        v_ref[:, c0 + LANES:c0 + 2 * LANES] = jnp.where(lane < HEAD_DIM, ones_b, v2).astype(bf16)

    k_lane = lax.broadcasted_iota(jnp.int32, (rb, n_pairs * LANES), 1)
    k_ones = jnp.bitwise_and(k_lane, LANES - 1) < N_SPLIT
    q_lane = lax.broadcasted_iota(jnp.int32, (rb, ATTN_HEADS * LANES), 1)
    q_head = jnp.right_shift(q_lane, LANES.bit_length() - 1)
    q_ones_lo = N_SPLIT * (1 + jnp.bitwise_and(q_head, HEADS_PER_BLOCK - 1))
    q_in_blk = jnp.bitwise_and(q_lane, LANES - 1)
    q_ones = (q_in_blk >= q_ones_lo) & (q_in_blk < q_ones_lo + N_SPLIT)
    for rs, pc in zip(row_blk, pieces):
        k_bias = jnp.dot(pc, ksel_ref[...], preferred_element_type=f32)
        k_bias = jnp.where(k_ones, 1.0, k_bias).astype(bf16)
        for pair in range(n_pairs):
            c0 = pair * HEADS_PER_BLOCK * LANES + LANES
            kx_ref[rs, c0:c0 + LANES] = k_bias[:, pair * LANES:(pair + 1) * LANES]
        q_bias = jnp.dot(pc, qsel_ref[...], preferred_element_type=f32)
        q_bias = jnp.where(q_ones, 1.0, q_bias).astype(bf16)
        for head in range(ATTN_HEADS):
            c0 = head * HEADS_PER_BLOCK * LANES + LANES
            qx_ref[rs, c0:c0 + LANES] = q_bias[:, head * LANES:(head + 1) * LANES]

    p = proj(w_mix_ref, C_CC, CONV_WIDTH) * proj(w_mix_ref, C_CIN, CONV_WIDTH)
    p_scr[SUBLANES:SUBLANES + tm, :] = p
    conv = (p_scr[SUBLANES - 2:SUBLANES - 2 + tm, :] * cw_ref[0:1, :]
            + p_scr[SUBLANES - 1:SUBLANES - 1 + tm, :] * cw_ref[1:2, :]
            + p * cw_ref[2:3, :])
    tail = p_scr[tm:tm + SUBLANES, :]
    p_scr[0:SUBLANES, :] = tail
    p_tail_ref[...] = tail
    conv_in = (proj(w_mix_ref, C_CB, CONV_WIDTH) * conv).astype(bf16)
    y_conv = jnp.dot(conv_in, wc_ref[...], preferred_element_type=f32)
    sga_ref[...] = jax.nn.sigmoid(proj(w_mix_ref, C_GA, D_MODEL)).astype(bf16)
    gcv_ref[...] = (jax.nn.sigmoid(proj(w_mix_ref, C_GC, D_MODEL)) * y_conv).astype(bf16)


def _bias_selectors():
    ksel = np.zeros((LANES, ATTN_HEADS // HEADS_PER_BLOCK * LANES), np.float32)
    qsel = np.zeros((LANES, ATTN_HEADS * LANES), np.float32)
    for h in range(ATTN_HEADS):
        for t in range(N_SPLIT):
            col = (h // HEADS_PER_BLOCK) * LANES + N_SPLIT * (1 + h % HEADS_PER_BLOCK) + t
            ksel[t * ATTN_HEADS + h, col] = -1.0
            qsel[t * ATTN_HEADS + h, h * LANES + t] = 1.0
    return jnp.asarray(ksel, bf16), jnp.asarray(qsel, bf16)


def _mix_in(h, g, w_qkv, w_mix, w_f, b_f, conv_w, w_conv_branch, p_init, f_init, batch, tm):
    rows = h.shape[0]
    nt = rows // (batch * tm)
    row_blk = lambda width: pl.BlockSpec((tm, width), lambda b, t: (b * nt + t, 0))
    out_rows = lambda width, dt: jax.ShapeDtypeStruct((rows, width), dt)
    return pl.pallas_call(
        functools.partial(_mix_in_body, tm=tm),
        grid=(batch, nt),
        in_specs=[
            row_blk(D_MODEL),
            _resident((1, D_MODEL)),
            _resident((D_MODEL, N_QKV)),
            _resident((D_MODEL, N_MIX)),
            _resident((D_MODEL, LANES)),
            _resident((1, LANES)),
            _resident((CONV_K, CONV_WIDTH)),
            _resident((CONV_WIDTH, D_MODEL)),
            _resident((SUBLANES, CONV_WIDTH)),
            _resident((1, LANES)),
            _resident((LANES, ATTN_HEADS // HEADS_PER_BLOCK * LANES)),
            _resident((LANES, ATTN_HEADS * LANES)),
        ],
        out_specs=[
            row_blk(2 * ATTN_HEADS * LANES), row_blk(ATTN_HEADS * LANES),
            row_blk(ATTN_HEADS * LANES),
            row_blk(ATTN_HEADS), row_blk(D_MODEL), row_blk(D_MODEL),
            pl.BlockSpec((SUBLANES, CONV_WIDTH), lambda b, t: (b * nt + t, 0)),
        ],
        out_shape=[
            out_rows(2 * ATTN_HEADS * LANES, bf16), out_rows(ATTN_HEADS * LANES, bf16),
            out_rows(ATTN_HEADS * LANES, bf16),
            out_rows(ATTN_HEADS, f32), out_rows(D_MODEL, bf16), out_rows(D_MODEL, bf16),
            jax.ShapeDtypeStruct((batch * nt * SUBLANES, CONV_WIDTH), f32),
        ],
        scratch_shapes=[pltpu.VMEM((tm + SUBLANES, CONV_WIDTH), f32),
                        pltpu.VMEM((1, LANES), f32)],
        compiler_params=pltpu.CompilerParams(
            dimension_semantics=("arbitrary", "arbitrary"), vmem_limit_bytes=VMEM_LIMIT_BYTES),
        name="mixer_input",
    )(h, g, w_qkv, w_mix, w_f, b_f, conv_w, w_conv_branch, p_init, f_init, *_bias_selectors())


def _attn_body(q_ref, k_ref, v_ref, km_ref, vm_ref, o_ref,
               m_scr, alpha_scr, acc_scr, s_scr, sm_scr, p_scr, pm_scr, *, tq, tk, rows, pairs):
    i = pl.program_id(2)
    low_half = lax.broadcasted_iota(jnp.int32, (tq, LANES), 1) < HEAD_DIM
    head_lanes = HEADS_PER_BLOCK * LANES
    m_scr[...] = jnp.full(m_scr.shape, MASK_VALUE, f32)
    acc_scr[...] = jnp.zeros(acc_scr.shape, f32)
    contract_last = (((1,), (1,)), ((), ()))
    lane = lax.broadcasted_iota(jnp.int32, (rows, LANES), 1)
    sub = lax.broadcasted_iota(jnp.int32, (rows, LANES), 0)
    n_heads = pairs * HEADS_PER_BLOCK
    assert n_heads % 2 == 0

    def scores(hh, keys_ref, key_rows):
        pair = hh // HEADS_PER_BLOCK
        q_head = q_ref[:, hh * head_lanes:(hh + 1) * head_lanes]
        k_pair = keys_ref[key_rows, pair * head_lanes:(pair + 1) * head_lanes]
        return lax.dot_general(q_head, k_pair, contract_last, preferred_element_type=f32)

    def softmax_rows(hh, r0, parts):
        rs = slice(r0, r0 + rows)
        blocks = [blk if valid is None else jnp.where(valid, blk, MASK_VALUE)
                  for blk, valid, _, _ in parts]
        m_old = m_scr[hh, rs, :]
        m_blk = functools.reduce(jnp.maximum, blocks)
        m_new = jnp.maximum(
            m_old, jnp.broadcast_to(jnp.max(m_blk, axis=-1, keepdims=True), (rows, LANES)))
        for blk, (_, _, dst, c0) in zip(blocks, parts):
            dst[hh % 2, rs, c0:c0 + LANES] = jnp.exp2(blk - m_new).astype(bf16)
        alpha_scr[hh, rs, :] = jnp.exp2(m_old - m_new)
        m_scr[hh, rs, :] = m_new

    def accumulate(hh, pv):
        acc_scr[hh] = alpha_scr[hh] * acc_scr[hh] + pv

    s_scr[0] = scores(0, k_ref, pl.ds(0, tk))

    def full_step(j, carry):
        start = pl.multiple_of(j * tk, tk)
        for hh in range(n_heads):
            slot = hh % 2
            if hh + 1 < n_heads:
                s_scr[1 - slot] = scores(hh + 1, k_ref, pl.ds(start, tk))
            else:
                s_scr[1 - slot] = scores(0, k_ref, pl.ds(pl.multiple_of(start + tk, tk), tk))
            for r0 in range(0, tq, rows):
                softmax_rows(hh, r0, [
                    (s_scr[slot, r0:r0 + rows, c * LANES:(c + 1) * LANES], None, p_scr, c * LANES)
                    for c in range(tk // LANES)])
            accumulate(hh, jnp.dot(p_scr[slot], v_ref[pl.ds(start, tk), hh * LANES:(hh + 1) * LANES],
                                   preferred_element_type=f32))
        return carry

    lax.fori_loop(0, lax.shift_right_logical(i, 1),
                  lambda jj, c: full_step(2 * jj + 1, full_step(2 * jj, c)), 0)

    @pl.when(jnp.bitwise_and(i, 1) == 1)
    def _():
        full_step(i - 1, 0)

    start = pl.multiple_of(i * tk, tk)
    meta_valid = lane < N_META
    sm_scr[0] = scores(0, km_ref, slice(None))
    for hh in range(n_heads):
        slot = hh % 2
        if hh + 1 < n_heads:
            s_scr[1 - slot] = scores(hh + 1, k_ref, pl.ds(start, tk))
            sm_scr[1 - slot] = scores(hh + 1, km_ref, slice(None))
        for r0 in range(0, tq, rows):
            n_cols = r0 + rows
            parts = [(s_scr[slot, r0:r0 + rows, c * LANES:(c + 1) * LANES],
                      None if (c + 1) * LANES - 1 <= r0 else (lane + c * LANES <= sub + r0),
                      p_scr, c * LANES) for c in range(n_cols // LANES)]
            parts.append((sm_scr[slot, r0:r0 + rows, :], meta_valid, pm_scr, 0))
            softmax_rows(hh, r0, parts)
            if n_cols < tk:
                p_scr[slot, r0:r0 + rows, n_cols:tk] = jnp.zeros((rows, tk - n_cols), bf16)
        accumulate(hh, jnp.dot(p_scr[slot], v_ref[pl.ds(start, tk), hh * LANES:(hh + 1) * LANES],
                               preferred_element_type=f32)
                   + jnp.dot(pm_scr[slot], vm_ref[:, hh * LANES:(hh + 1) * LANES],
                             preferred_element_type=f32))

    for pair in range(pairs):
        acc_a, acc_b = acc_scr[HEADS_PER_BLOCK * pair], acc_scr[HEADS_PER_BLOCK * pair + 1]
        out = jnp.where(low_half, acc_a / acc_a[:, HEAD_DIM:HEAD_DIM + 1], acc_b / acc_b[:, 0:1])
        o_ref[:, pair * LANES:(pair + 1) * LANES] = out.astype(bf16)


def _attention(qx, kx, v_aug, kx_meta, v_meta, batch, seq, tq, tk, rows, pairs):
    nq = seq // tq
    n_heads = pairs * HEADS_PER_BLOCK
    n_blk = ATTN_HEADS // n_heads
    pair_lanes = n_heads * LANES
    q_spec = pl.BlockSpec((tq, HEADS_PER_BLOCK * pair_lanes), lambda b, p, i: (b * nq + i, p))
    o_spec = pl.BlockSpec((tq, pairs * LANES), lambda b, p, i: (b * nq + i, p))
    kv_spec = pl.BlockSpec((seq, pair_lanes), lambda b, p, i: (b, p),
                           pipeline_mode=pl.Buffered(KV_BUFFERS))
    meta_spec = pl.BlockSpec((LANES, pair_lanes), lambda b, p, i: (0, p))
    per_head = lambda width, dt: pltpu.VMEM((n_heads, tq, width), dt)
    ring = lambda width, dt: pltpu.VMEM((2, tq, width), dt)
    return pl.pallas_call(
        functools.partial(_attn_body, tq=tq, tk=tk, rows=rows, pairs=pairs),
        grid=(batch, n_blk, nq),
        in_specs=[q_spec, kv_spec, kv_spec, meta_spec, meta_spec],
        out_specs=o_spec,
        out_shape=jax.ShapeDtypeStruct((batch * seq, ATTN_WIDTH), bf16),
        scratch_shapes=[per_head(LANES, f32), per_head(LANES, f32), per_head(LANES, f32),
                        ring(tk, f32), ring(LANES, f32), ring(tk, bf16), ring(LANES, bf16)],
        compiler_params=pltpu.CompilerParams(
            dimension_semantics=("arbitrary", "arbitrary", "arbitrary"),
            vmem_limit_bytes=VMEM_LIMIT_BYTES),
        name="forgetting_attention",
    )(qx, kx, v_aug, kx_meta, v_meta)


def _mix_out_ffn_body(h_ref, attn_ref, sga_ref, gcv_ref, wa_ref, wo_ref, g_mix_ref,
                      g_pre_ref, g_post_ref, w_in_ref, w_out_ref, o_ref, act_ref):
    def mixer_residual(rs):
        y_attn = jnp.dot(attn_ref[rs, :], wa_ref[...], preferred_element_type=f32)
        gated = sga_ref[rs, :].astype(f32) * y_attn + gcv_ref[rs, :].astype(f32)
        mixed = jnp.dot(gated.astype(bf16), wo_ref[...], preferred_element_type=f32)
        return h_ref[rs, :] + _rms_norm(mixed, g_mix_ref[...])

    groups = _row_groups(h_ref.shape[0])
    residuals = [functools.partial(mixer_residual, rs) for rs in groups]
    _swiglu_groups(residuals, g_pre_ref[...], g_post_ref[...], w_in_ref, w_out_ref, act_ref,
                   o_ref, groups)


def _mix_out_ffn(h, attn, sga, gcv, w_attn_branch, w_out, g_mix, g_pre, g_post, w_in, w_ffn_out, tm):
    rows = h.shape[0]
    row_blk = lambda width: pl.BlockSpec((tm, width), lambda i: (i, 0))
    return pl.pallas_call(
        _mix_out_ffn_body,
        grid=(rows // tm,),
        in_specs=[row_blk(D_MODEL), row_blk(ATTN_WIDTH), row_blk(D_MODEL), row_blk(D_MODEL),
                  _resident((ATTN_WIDTH, D_MODEL)), _resident((D_MODEL, D_MODEL)),
                  _resident((1, D_MODEL)), _resident((1, D_MODEL)), _resident((1, D_MODEL)),
                  _resident((D_MODEL, 2 * D_FF)), _resident((D_FF, D_MODEL))],
        out_specs=row_blk(D_MODEL),
        out_shape=jax.ShapeDtypeStruct((rows, D_MODEL), f32),
        scratch_shapes=[pltpu.VMEM((tm, D_FF), bf16)],
        compiler_params=pltpu.CompilerParams(
            dimension_semantics=("arbitrary",), vmem_limit_bytes=VMEM_LIMIT_BYTES),
        name="mixer_output_ffn",
    )(h, attn, sga, gcv, w_attn_branch, w_out, g_mix, g_pre, g_post, w_in, w_ffn_out)


def kernel(x, meta_tokens, w_in, b_forget, conv_w, w_attn_branch, w_conv_branch, w_out,
           g_ffn1_pre, g_ffn1_post, w_ffn1_in, w_ffn1_out, g_mix_pre, g_mix_post,
           g_ffn2_pre, g_ffn2_post, w_ffn2_in, w_ffn2_out):
    batch, seq, d = x.shape
    assert d == D_MODEL and w_in.shape[0] == 1 and meta_tokens.shape == (N_META, D_MODEL)
    assert seq % ROW_TILE == 0 and seq % FFN_TILE == 0 and seq % ATTN_TQ == 0
    assert ATTN_TQ == ATTN_TK and ATTN_TQ % ATTN_ROWS == 0 and ATTN_ROWS % LANES == 0

    gain = lambda g: g[0].reshape(1, D_MODEL).astype(f32)
    w = w_in[0].astype(bf16)
    w_qkv = w[:, :N_QKV]
    w_mix = w[:, N_QKV + ATTN_HEADS:]
    w_f = jnp.pad(w[:, N_QKV:N_QKV + ATTN_HEADS], ((0, 0), (0, LANES - ATTN_HEADS)))
    b_f = jnp.pad(b_forget[0].astype(f32), (0, LANES - ATTN_HEADS)).reshape(1, LANES)
    cw = conv_w[0].astype(f32)
    wc = w_conv_branch[0].astype(bf16)
    wa = w_attn_branch[0].astype(bf16)
    wo = w_out[0].astype(bf16)
    w1_in, w1_out = w_ffn1_in[0].astype(bf16), w_ffn1_out[0].astype(bf16)
    w2_in, w2_out = w_ffn2_in[0].astype(bf16), w_ffn2_out[0].astype(bf16)

    hm = _ffn(meta_tokens.astype(f32), gain(g_ffn1_pre), gain(g_ffn1_post), w1_in, w1_out, N_META)
    zeros_p = jnp.zeros((SUBLANES, CONV_WIDTH), f32)
    zeros_f = jnp.zeros((1, LANES), f32)
    _, kxm, vm, fm, _, _, pm_tail = _mix_in(hm, gain(g_mix_pre), w_qkv, w_mix, w_f, b_f, cw, wc,
                                           zeros_p, zeros_f, 1, N_META)
    pad_rows = lambda a: jnp.pad(a, ((0, LANES - N_META), (0, 0)))
    f_init = jnp.pad(fm[N_META - 1:N_META, :], ((0, 0), (0, LANES - ATTN_HEADS)))

    rows = batch * seq
    h1 = _ffn(x.reshape(rows, D_MODEL), gain(g_ffn1_pre), gain(g_ffn1_post), w1_in, w1_out, FFN_TILE)
    qx, kx, v, _, sga, gcv, _ = _mix_in(h1, gain(g_mix_pre), w_qkv, w_mix, w_f, b_f, cw, wc,
                                        pm_tail, f_init, batch, ROW_TILE)
    attn = _attention(qx, kx, v, pad_rows(kxm), pad_rows(vm),
                      batch, seq, ATTN_TQ, ATTN_TK, ATTN_ROWS, ATTN_PAIRS)
    h3 = _mix_out_ffn(h1, attn, sga, gcv, wa, wo, gain(g_mix_post), gain(g_ffn2_pre),
                      gain(g_ffn2_post), w2_in, w2_out, ROW_TILE)
    return h3.reshape(batch, seq, D_MODEL)
```

```python
import functools

import jax
import jax.numpy as jnp
import numpy as np
from jax import lax
from jax.experimental import pallas as pl
from jax.experimental.pallas import tpu as pltpu

D_MODEL = 1024
D_FF = 2816
N_META = 16
ATTN_HEADS = 8
HEAD_DIM = 64
ATTN_WIDTH = ATTN_HEADS * HEAD_DIM
CONV_WIDTH = 512
CONV_K = 3
NORM_EPS = 1e-6

LANES = 128
SUBLANES = 8
HEADS_PER_BLOCK = LANES // HEAD_DIM
VMEM_LIMIT_BYTES = 56 * 1024 * 1024
MASK_VALUE = -1e30
LOG2E = 1.4426950408889634
N_SPLIT = 3

ROW_TILE = 512
FFN_TILE = 1024
FF_CHUNK = 256
ROW_GROUPS = 1
STAGE_ROWS_IN = 128
STAGE_ROWS_OUT = 256
ATTN_TQ = 512
ATTN_TK = 512
ATTN_ROWS = 128
ATTN_PAIRS = 4
KV_BUFFERS = 1

C_Q, C_K, C_V = 0, 512, 1024
C_CB, C_CC, C_CIN, C_GA, C_GC = 0, 512, 1024, 1536, 2560
N_QKV = 3 * ATTN_WIDTH
N_MIX = 3 * CONV_WIDTH + 2 * D_MODEL

bf16 = jnp.bfloat16
f32 = jnp.float32


def _rms_norm(x, g):
    ms = jnp.mean(x * x, axis=-1, keepdims=True)
    return x * lax.rsqrt(ms + NORM_EPS) * g


def _resident(shape):
    nd = len(shape)
    return pl.BlockSpec(shape, lambda *_: (0,) * nd, pipeline_mode=pl.Buffered(1))


def _row_groups(tm):
    n = ROW_GROUPS if tm % (ROW_GROUPS * 2 * SUBLANES) == 0 else 1
    return [slice(g * tm // n, (g + 1) * tm // n) for g in range(n)]


def _swiglu_up(u, rs, w_in_ref, act_ref):
    for c in range(D_FF // FF_CHUNK):
        lo = c * FF_CHUNK
        a = jnp.dot(u, w_in_ref[:, lo:lo + FF_CHUNK], preferred_element_type=f32)
        b = jnp.dot(u, w_in_ref[:, D_FF + lo:D_FF + lo + FF_CHUNK], preferred_element_type=f32)
        act_ref[rs, lo:lo + FF_CHUNK] = (a * jax.nn.sigmoid(a) * b).astype(bf16)


def _swiglu_groups(residuals, g_pre, g_post, w_in_ref, w_out_ref, act_ref, o_ref, groups):
    h = [None] * len(groups)
    u = [None] * len(groups)
    for g, rs in enumerate(groups + [None]):
        if rs is not None:
            h[g] = residuals[g]()
            u[g] = _rms_norm(h[g], g_pre).astype(bf16)
        if g > 0:
            y = jnp.dot(act_ref[groups[g - 1], :], w_out_ref[...], preferred_element_type=f32)
        if rs is not None:
            _swiglu_up(u[g], rs, w_in_ref, act_ref)
        if g > 0:
            o_ref[groups[g - 1], :] = h[g - 1] + 0.5 * _rms_norm(y, g_post)


def _stage_ffn_weights(w_in_hbm, w_out_hbm, w_in_ref, w_out_ref, stage_in, stage_out, sems):
    def stream(src, dst, stage, first_sem):
        chunk = stage.shape[1]
        n_chunks = dst.shape[0] // chunk

        def copy(c):
            return pltpu.make_async_copy(src.at[pl.ds(c * chunk, chunk), :], stage.at[c % 2],
                                         sems.at[first_sem + c % 2])

        copy(0).start()
        for c in range(n_chunks):
            if c + 1 < n_chunks:
                copy(c + 1).start()
            copy(c).wait()
            dst[c * chunk:(c + 1) * chunk, :] = stage[c % 2].astype(bf16)

    @pl.when(pl.program_id(0) == 0)
    def _():
        stream(w_in_hbm, w_in_ref, stage_in, 0)
        stream(w_out_hbm, w_out_ref, stage_out, 2)


def _ffn_weight_specs():
    in_specs = [pl.BlockSpec(memory_space=pl.ANY), pl.BlockSpec(memory_space=pl.ANY)]
    scratch = [pltpu.VMEM((D_MODEL, 2 * D_FF), bf16), pltpu.VMEM((D_FF, D_MODEL), bf16),
               pltpu.VMEM((2, STAGE_ROWS_IN, 2 * D_FF), f32),
               pltpu.VMEM((2, STAGE_ROWS_OUT, D_MODEL), f32),
               pltpu.SemaphoreType.DMA((4,))]
    return in_specs, scratch


def _ffn_body(h_ref, g_pre_ref, g_post_ref, w_in_hbm, w_out_hbm, o_ref, act_ref,
              w_in_ref, w_out_ref, stage_in, stage_out, sems):
    _stage_ffn_weights(w_in_hbm, w_out_hbm, w_in_ref, w_out_ref, stage_in, stage_out, sems)
    groups = _row_groups(h_ref.shape[0])
    residuals = [functools.partial(lambda rs: h_ref[rs, :], rs) for rs in groups]
    _swiglu_groups(residuals, g_pre_ref[...], g_post_ref[...], w_in_ref, w_out_ref, act_ref,
                   o_ref, groups)


def _ffn(h, g_pre, g_post, w_in, w_out, tm):
    rows = h.shape[0]
    weight_specs, weight_scratch = _ffn_weight_specs()
    return pl.pallas_call(
        _ffn_body,
        grid=(rows // tm,),
        in_specs=[
            pl.BlockSpec((tm, D_MODEL), lambda i: (i, 0)),
            _resident((1, D_MODEL)),
            _resident((1, D_MODEL)),
        ] + weight_specs,
        out_specs=pl.BlockSpec((tm, D_MODEL), lambda i: (i, 0)),
        out_shape=jax.ShapeDtypeStruct((rows, D_MODEL), f32),
        scratch_shapes=[pltpu.VMEM((tm, D_FF), bf16)] + weight_scratch,
        compiler_params=pltpu.CompilerParams(
            dimension_semantics=("arbitrary",), vmem_limit_bytes=VMEM_LIMIT_BYTES),
        name="ffn_half_step",
    )(h, g_pre, g_post, w_in, w_out)


def _log_sigmoid(x):
    return jnp.minimum(x, 0.0) - jnp.log1p(jnp.exp(-jnp.abs(x)))


def _split3(x):
    hi = x.astype(bf16)
    r = x - hi.astype(f32)
    mid = r.astype(bf16)
    lo = (r - mid.astype(f32)).astype(bf16)
    return hi, mid, lo


def _pack3(x, head_lanes):
    hi, mid, lo = _split3(jnp.where(head_lanes, x, 0.0))
    packed = (hi.astype(f32) + pltpu.roll(mid.astype(f32), ATTN_HEADS, 1)
              + pltpu.roll(lo.astype(f32), 2 * ATTN_HEADS, 1))
    return packed.astype(bf16)


def _mix_in_body(h_ref, g_ref, w_qkv_ref, w_mix_ref, w_f_ref, bf_ref, cw_ref, wc_ref, p_init_ref, f_init_ref,
                 ksel_ref, qsel_ref,
                 qx_ref, kx_ref, v_ref, fsum_ref, sga_ref, gcv_ref, p_tail_ref,
                 p_scr, f_carry, *, tm):
    @pl.when(pl.program_id(1) == 0)
    def _():
        p_scr[0:SUBLANES, :] = p_init_ref[...]
        f_carry[...] = f_init_ref[...]

    u = _rms_norm(h_ref[...], g_ref[...]).astype(bf16)

    def proj(w_ref, lo, width):
        return jnp.dot(u, w_ref[:, lo:lo + width], preferred_element_type=f32)

    rb = min(tm, LANES)
    n_blk = tm // rb
    row_blk = [slice(blk * rb, (blk + 1) * rb) for blk in range(n_blk)]
    n_pairs = ATTN_HEADS // HEADS_PER_BLOCK
    head_lanes = lax.broadcasted_iota(jnp.int32, (rb, LANES), 1) < ATTN_HEADS
    log_f = _log_sigmoid(proj(w_f_ref, 0, LANES) + bf_ref[...]) * LOG2E

    q = (proj(w_qkv_ref, C_Q, ATTN_WIDTH) * (LOG2E * HEAD_DIM ** -0.5)).astype(bf16)
    lane = lax.broadcasted_iota(jnp.int32, (tm, LANES), 1)
    for pair in range(n_pairs):
        q2 = q[:, pair * LANES:(pair + 1) * LANES]
        zeros = jnp.zeros_like(q2)
        c0 = 2 * pair * HEADS_PER_BLOCK * LANES
        qx_ref[:, c0:c0 + LANES] = jnp.where(lane < HEAD_DIM, q2, zeros)
        qx_ref[:, c0 + 2 * LANES:c0 + 3 * LANES] = jnp.where(lane < HEAD_DIM, zeros, q2)

    tri = (lax.broadcasted_iota(jnp.int32, (rb, rb), 0)
           >= lax.broadcasted_iota(jnp.int32, (rb, rb), 1)).astype(bf16)
    c3 = [jnp.dot(tri, _pack3(log_f[rs, :], head_lanes), preferred_element_type=f32)
          for rs in row_blk]

    k = proj(w_qkv_ref, C_K, ATTN_WIDTH)
    for pair in range(n_pairs):
        c0 = pair * HEADS_PER_BLOCK * LANES
        kx_ref[:, c0:c0 + LANES] = k[:, pair * LANES:(pair + 1) * LANES].astype(bf16)

    carry = f_carry[...]
    pieces = []
    for rs, c in zip(row_blk, c3):
        cs = c + pltpu.roll(c, LANES - ATTN_HEADS, 1) + pltpu.roll(c, LANES - 2 * ATTN_HEADS, 1)
        cs = jnp.where(head_lanes, cs, 0.0) + carry
        carry = cs[rb - 1:rb, :]
        fsum_ref[rs, :] = cs[:, :ATTN_HEADS]
        pieces.append(_pack3(cs, head_lanes))
    f_carry[...] = carry

    v = proj(w_qkv_ref, C_V, ATTN_WIDTH)
    ones_a = jnp.where(lane == HEAD_DIM, 1.0, 0.0)
    ones_b = jnp.where(lane == 0, 1.0, 0.0)
    for pair in range(n_pairs):
        v2 = v[:, pair * LANES:(pair + 1) * LANES]
        c0 = pair * HEADS_PER_BLOCK * LANES
        v_ref[:, c0:c0 + LANES] = jnp.where(lane < HEAD_DIM, v2, ones_a).astype(bf16)
        v_ref[:, c0 + LANES:c0 + 2 * LANES] = jnp.where(lane < HEAD_DIM, ones_b, v2).astype(bf16)

    k_lane = lax.broadcasted_iota(jnp.int32, (rb, n_pairs * LANES), 1)
    k_ones = jnp.bitwise_and(k_lane, LANES - 1) < N_SPLIT
    q_lane = lax.broadcasted_iota(jnp.int32, (rb, ATTN_HEADS * LANES), 1)
    q_head = jnp.right_shift(q_lane, LANES.bit_length() - 1)
    q_ones_lo = N_SPLIT * (1 + jnp.bitwise_and(q_head, HEADS_PER_BLOCK - 1))
    q_in_blk = jnp.bitwise_and(q_lane, LANES - 1)
    q_ones = (q_in_blk >= q_ones_lo) & (q_in_blk < q_ones_lo + N_SPLIT)
    for rs, pc in zip(row_blk, pieces):
        k_bias = jnp.dot(pc, ksel_ref[...], preferred_element_type=f32)
        k_bias = jnp.where(k_ones, 1.0, k_bias).astype(bf16)
        for pair in range(n_pairs):
            c0 = pair * HEADS_PER_BLOCK * LANES + LANES
            kx_ref[rs, c0:c0 + LANES] = k_bias[:, pair * LANES:(pair + 1) * LANES]
        q_bias = jnp.dot(pc, qsel_ref[...], preferred_element_type=f32)
        q_bias = jnp.where(q_ones, 1.0, q_bias).astype(bf16)
        for head in range(ATTN_HEADS):
            c0 = head * HEADS_PER_BLOCK * LANES + LANES
            qx_ref[rs, c0:c0 + LANES] = q_bias[:, head * LANES:(head + 1) * LANES]

    p = proj(w_mix_ref, C_CC, CONV_WIDTH) * proj(w_mix_ref, C_CIN, CONV_WIDTH)
    p_scr[SUBLANES:SUBLANES + tm, :] = p
    conv = (p_scr[SUBLANES - 2:SUBLANES - 2 + tm, :] * cw_ref[0:1, :]
            + p_scr[SUBLANES - 1:SUBLANES - 1 + tm, :] * cw_ref[1:2, :]
            + p * cw_ref[2:3, :])
    tail = p_scr[tm:tm + SUBLANES, :]
    p_scr[0:SUBLANES, :] = tail
    p_tail_ref[...] = tail
    conv_in = (proj(w_mix_ref, C_CB, CONV_WIDTH) * conv).astype(bf16)
    y_conv = jnp.dot(conv_in, wc_ref[...], preferred_element_type=f32)
    sga_ref[...] = jax.nn.sigmoid(proj(w_mix_ref, C_GA, D_MODEL)).astype(bf16)
    gcv_ref[...] = (jax.nn.sigmoid(proj(w_mix_ref, C_GC, D_MODEL)) * y_conv).astype(bf16)


def _bias_selectors():
    ksel = np.zeros((LANES, ATTN_HEADS // HEADS_PER_BLOCK * LANES), np.float32)
    qsel = np.zeros((LANES, ATTN_HEADS * LANES), np.float32)
    for h in range(ATTN_HEADS):
        for t in range(N_SPLIT):
            col = (h // HEADS_PER_BLOCK) * LANES + N_SPLIT * (1 + h % HEADS_PER_BLOCK) + t
            ksel[t * ATTN_HEADS + h, col] = -1.0
            qsel[t * ATTN_HEADS + h, h * LANES + t] = 1.0
    return jnp.asarray(ksel, bf16), jnp.asarray(qsel, bf16)


def _mix_in(h, g, w_qkv, w_mix, w_f, b_f, conv_w, w_conv_branch, p_init, f_init, batch, tm):
    rows = h.shape[0]
    nt = rows // (batch * tm)
    row_blk = lambda width: pl.BlockSpec((tm, width), lambda b, t: (b * nt + t, 0))
    out_rows = lambda width, dt: jax.ShapeDtypeStruct((rows, width), dt)
    return pl.pallas_call(
        functools.partial(_mix_in_body, tm=tm),
        grid=(batch, nt),
        in_specs=[
            row_blk(D_MODEL),
            _resident((1, D_MODEL)),
            _resident((D_MODEL, N_QKV)),
            _resident((D_MODEL, N_MIX)),
            _resident((D_MODEL, LANES)),
            _resident((1, LANES)),
            _resident((CONV_K, CONV_WIDTH)),
            _resident((CONV_WIDTH, D_MODEL)),
            _resident((SUBLANES, CONV_WIDTH)),
            _resident((1, LANES)),
            _resident((LANES, ATTN_HEADS // HEADS_PER_BLOCK * LANES)),
            _resident((LANES, ATTN_HEADS * LANES)),
        ],
        out_specs=[
            row_blk(2 * ATTN_HEADS * LANES), row_blk(ATTN_HEADS * LANES),
            row_blk(ATTN_HEADS * LANES),
            row_blk(ATTN_HEADS), row_blk(D_MODEL), row_blk(D_MODEL),
            pl.BlockSpec((SUBLANES, CONV_WIDTH), lambda b, t: (b * nt + t, 0)),
        ],
        out_shape=[
            out_rows(2 * ATTN_HEADS * LANES, bf16), out_rows(ATTN_HEADS * LANES, bf16),
            out_rows(ATTN_HEADS * LANES, bf16),
            out_rows(ATTN_HEADS, f32), out_rows(D_MODEL, bf16), out_rows(D_MODEL, bf16),
            jax.ShapeDtypeStruct((batch * nt * SUBLANES, CONV_WIDTH), f32),
        ],
        scratch_shapes=[pltpu.VMEM((tm + SUBLANES, CONV_WIDTH), f32),
                        pltpu.VMEM((1, LANES), f32)],
        compiler_params=pltpu.CompilerParams(
            dimension_semantics=("arbitrary", "arbitrary"), vmem_limit_bytes=VMEM_LIMIT_BYTES),
        name="mixer_input",
    )(h, g, w_qkv, w_mix, w_f, b_f, conv_w, w_conv_branch, p_init, f_init, *_bias_selectors())


def _attn_body(q_ref, k_ref, v_ref, km_ref, vm_ref, o_ref,
               m_scr, alpha_scr, acc_scr, s_scr, sm_scr, p_scr, pm_scr, *, tq, tk, rows, pairs):
    i = pl.program_id(2)
    low_half = lax.broadcasted_iota(jnp.int32, (tq, LANES), 1) < HEAD_DIM
    head_lanes = HEADS_PER_BLOCK * LANES
    m_scr[...] = jnp.full(m_scr.shape, MASK_VALUE, f32)
    acc_scr[...] = jnp.zeros(acc_scr.shape, f32)
    contract_last = (((1,), (1,)), ((), ()))
    lane = lax.broadcasted_iota(jnp.int32, (rows, LANES), 1)
    sub = lax.broadcasted_iota(jnp.int32, (rows, LANES), 0)
    n_heads = pairs * HEADS_PER_BLOCK
    assert n_heads % 2 == 0

    def scores(hh, keys_ref, key_rows):
        pair = hh // HEADS_PER_BLOCK
        q_head = q_ref[:, hh * head_lanes:(hh + 1) * head_lanes]
        k_pair = keys_ref[key_rows, pair * head_lanes:(pair + 1) * head_lanes]
        return lax.dot_general(q_head, k_pair, contract_last, preferred_element_type=f32)

    def softmax_rows(hh, r0, parts):
        rs = slice(r0, r0 + rows)
        blocks = [blk if valid is None else jnp.where(valid, blk, MASK_VALUE)
                  for blk, valid, _, _ in parts]
        m_old = m_scr[hh, rs, :]
        m_blk = functools.reduce(jnp.maximum, blocks)
        m_new = jnp.maximum(
            m_old, jnp.broadcast_to(jnp.max(m_blk, axis=-1, keepdims=True), (rows, LANES)))
        for blk, (_, _, dst, c0) in zip(blocks, parts):
            dst[hh % 2, rs, c0:c0 + LANES] = jnp.exp2(blk - m_new).astype(bf16)
        alpha_scr[hh, rs, :] = jnp.exp2(m_old - m_new)
        m_scr[hh, rs, :] = m_new

    def accumulate(hh, pv):
        acc_scr[hh] = alpha_scr[hh] * acc_scr[hh] + pv

    s_scr[0] = scores(0, k_ref, pl.ds(0, tk))

    def full_step(j, carry):
        start = pl.multiple_of(j * tk, tk)
        for hh in range(n_heads):
            slot = hh % 2
            if hh + 1 < n_heads:
                s_scr[1 - slot] = scores(hh + 1, k_ref, pl.ds(start, tk))
            else:
                s_scr[1 - slot] = scores(0, k_ref, pl.ds(pl.multiple_of(start + tk, tk), tk))
            for r0 in range(0, tq, rows):
                softmax_rows(hh, r0, [
                    (s_scr[slot, r0:r0 + rows, c * LANES:(c + 1) * LANES], None, p_scr, c * LANES)
                    for c in range(tk // LANES)])
            accumulate(hh, jnp.dot(p_scr[slot], v_ref[pl.ds(start, tk), hh * LANES:(hh + 1) * LANES],
                                   preferred_element_type=f32))
        return carry

    lax.fori_loop(0, lax.shift_right_logical(i, 1),
                  lambda jj, c: full_step(2 * jj + 1, full_step(2 * jj, c)), 0)

    @pl.when(jnp.bitwise_and(i, 1) == 1)
    def _():
        full_step(i - 1, 0)

    start = pl.multiple_of(i * tk, tk)
    meta_valid = lane < N_META
    sm_scr[0] = scores(0, km_ref, slice(None))
    for hh in range(n_heads):
        slot = hh % 2
        if hh + 1 < n_heads:
            s_scr[1 - slot] = scores(hh + 1, k_ref, pl.ds(start, tk))
            sm_scr[1 - slot] = scores(hh + 1, km_ref, slice(None))
        for r0 in range(0, tq, rows):
            n_cols = r0 + rows
            parts = [(s_scr[slot, r0:r0 + rows, c * LANES:(c + 1) * LANES],
                      None if (c + 1) * LANES - 1 <= r0 else (lane + c * LANES <= sub + r0),
                      p_scr, c * LANES) for c in range(n_cols // LANES)]
            parts.append((sm_scr[slot, r0:r0 + rows, :], meta_valid, pm_scr, 0))
            softmax_rows(hh, r0, parts)
            if n_cols < tk:
                p_scr[slot, r0:r0 + rows, n_cols:tk] = jnp.zeros((rows, tk - n_cols), bf16)
        accumulate(hh, jnp.dot(p_scr[slot], v_ref[pl.ds(start, tk), hh * LANES:(hh + 1) * LANES],
                               preferred_element_type=f32)
                   + jnp.dot(pm_scr[slot], vm_ref[:, hh * LANES:(hh + 1) * LANES],
                             preferred_element_type=f32))

    for pair in range(pairs):
        acc_a, acc_b = acc_scr[HEADS_PER_BLOCK * pair], acc_scr[HEADS_PER_BLOCK * pair + 1]
        out = jnp.where(low_half, acc_a / acc_a[:, HEAD_DIM:HEAD_DIM + 1], acc_b / acc_b[:, 0:1])
        o_ref[:, pair * LANES:(pair + 1) * LANES] = out.astype(bf16)


def _attention(qx, kx, v_aug, kx_meta, v_meta, batch, seq, tq, tk, rows, pairs):
    nq = seq // tq
    n_heads = pairs * HEADS_PER_BLOCK
    n_blk = ATTN_HEADS // n_heads
    pair_lanes = n_heads * LANES
    q_spec = pl.BlockSpec((tq, HEADS_PER_BLOCK * pair_lanes), lambda b, p, i: (b * nq + i, p))
    o_spec = pl.BlockSpec((tq, pairs * LANES), lambda b, p, i: (b * nq + i, p))
    kv_spec = pl.BlockSpec((seq, pair_lanes), lambda b, p, i: (b, p),
                           pipeline_mode=pl.Buffered(KV_BUFFERS))
    meta_spec = pl.BlockSpec((LANES, pair_lanes), lambda b, p, i: (0, p))
    per_head = lambda width, dt: pltpu.VMEM((n_heads, tq, width), dt)
    ring = lambda width, dt: pltpu.VMEM((2, tq, width), dt)
    return pl.pallas_call(
        functools.partial(_attn_body, tq=tq, tk=tk, rows=rows, pairs=pairs),
        grid=(batch, n_blk, nq),
        in_specs=[q_spec, kv_spec, kv_spec, meta_spec, meta_spec],
        out_specs=o_spec,
        out_shape=jax.ShapeDtypeStruct((batch * seq, ATTN_WIDTH), bf16),
        scratch_shapes=[per_head(LANES, f32), per_head(LANES, f32), per_head(LANES, f32),
                        ring(tk, f32), ring(LANES, f32), ring(tk, bf16), ring(LANES, bf16)],
        compiler_params=pltpu.CompilerParams(
            dimension_semantics=("arbitrary", "arbitrary", "arbitrary"),
            vmem_limit_bytes=VMEM_LIMIT_BYTES),
        name="forgetting_attention",
    )(qx, kx, v_aug, kx_meta, v_meta)


def _mix_out_ffn_body(h_ref, attn_ref, sga_ref, gcv_ref, wa_ref, wo_ref, g_mix_ref,
                      g_pre_ref, g_post_ref, w_in_hbm, w_out_hbm, o_ref, act_ref,
                      w_in_ref, w_out_ref, stage_in, stage_out, sems):
    _stage_ffn_weights(w_in_hbm, w_out_hbm, w_in_ref, w_out_ref, stage_in, stage_out, sems)

    def mixer_residual(rs):
        y_attn = jnp.dot(attn_ref[rs, :], wa_ref[...], preferred_element_type=f32)
        gated = sga_ref[rs, :].astype(f32) * y_attn + gcv_ref[rs, :].astype(f32)
        mixed = jnp.dot(gated.astype(bf16), wo_ref[...], preferred_element_type=f32)
        return h_ref[rs, :] + _rms_norm(mixed, g_mix_ref[...])

    groups = _row_groups(h_ref.shape[0])
    residuals = [functools.partial(mixer_residual, rs) for rs in groups]
    _swiglu_groups(residuals, g_pre_ref[...], g_post_ref[...], w_in_ref, w_out_ref, act_ref,
                   o_ref, groups)


def _mix_out_ffn(h, attn, sga, gcv, w_attn_branch, w_out, g_mix, g_pre, g_post, w_in, w_ffn_out, tm):
    rows = h.shape[0]
    row_blk = lambda width: pl.BlockSpec((tm, width), lambda i: (i, 0))
    weight_specs, weight_scratch = _ffn_weight_specs()
    return pl.pallas_call(
        _mix_out_ffn_body,
        grid=(rows // tm,),
        in_specs=[row_blk(D_MODEL), row_blk(ATTN_WIDTH), row_blk(D_MODEL), row_blk(D_MODEL),
                  _resident((ATTN_WIDTH, D_MODEL)), _resident((D_MODEL, D_MODEL)),
                  _resident((1, D_MODEL)), _resident((1, D_MODEL)), _resident((1, D_MODEL))]
        + weight_specs,
        out_specs=row_blk(D_MODEL),
        out_shape=jax.ShapeDtypeStruct((rows, D_MODEL), f32),
        scratch_shapes=[pltpu.VMEM((tm, D_FF), bf16)] + weight_scratch,
        compiler_params=pltpu.CompilerParams(
            dimension_semantics=("arbitrary",), vmem_limit_bytes=VMEM_LIMIT_BYTES),
        name="mixer_output_ffn",
    )(h, attn, sga, gcv, w_attn_branch, w_out, g_mix, g_pre, g_post, w_in, w_ffn_out)


def kernel(x, meta_tokens, w_in, b_forget, conv_w, w_attn_branch, w_conv_branch, w_out,
           g_ffn1_pre, g_ffn1_post, w_ffn1_in, w_ffn1_out, g_mix_pre, g_mix_post,
           g_ffn2_pre, g_ffn2_post, w_ffn2_in, w_ffn2_out):
    batch, seq, d = x.shape
    assert d == D_MODEL and w_in.shape[0] == 1 and meta_tokens.shape == (N_META, D_MODEL)
    assert seq % ROW_TILE == 0 and seq % FFN_TILE == 0 and seq % ATTN_TQ == 0
    assert ATTN_TQ == ATTN_TK and ATTN_TQ % ATTN_ROWS == 0 and ATTN_ROWS % LANES == 0

    gain = lambda g: g[0].reshape(1, D_MODEL).astype(f32)
    w = w_in[0].astype(bf16)
    w_qkv = w[:, :N_QKV]
    w_mix = w[:, N_QKV + ATTN_HEADS:]
    w_f = jnp.pad(w[:, N_QKV:N_QKV + ATTN_HEADS], ((0, 0), (0, LANES - ATTN_HEADS)))
    b_f = jnp.pad(b_forget[0].astype(f32), (0, LANES - ATTN_HEADS)).reshape(1, LANES)
    cw = conv_w[0].astype(f32)
    wc = w_conv_branch[0].astype(bf16)
    wa = w_attn_branch[0].astype(bf16)
    wo = w_out[0].astype(bf16)
    w1_in, w1_out = w_ffn1_in[0], w_ffn1_out[0]
    w2_in, w2_out = w_ffn2_in[0], w_ffn2_out[0]

    hm = _ffn(meta_tokens.astype(f32), gain(g_ffn1_pre), gain(g_ffn1_post), w1_in, w1_out, N_META)
    zeros_p = jnp.zeros((SUBLANES, CONV_WIDTH), f32)
    zeros_f = jnp.zeros((1, LANES), f32)
    _, kxm, vm, fm, _, _, pm_tail = _mix_in(hm, gain(g_mix_pre), w_qkv, w_mix, w_f, b_f, cw, wc,
                                           zeros_p, zeros_f, 1, N_META)
    pad_rows = lambda a: jnp.pad(a, ((0, LANES - N_META), (0, 0)))
    f_init = jnp.pad(fm[N_META - 1:N_META, :], ((0, 0), (0, LANES - ATTN_HEADS)))

    rows = batch * seq
    h1 = _ffn(x.reshape(rows, D_MODEL), gain(g_ffn1_pre), gain(g_ffn1_post), w1_in, w1_out, FFN_TILE)
    qx, kx, v, _, sga, gcv, _ = _mix_in(h1, gain(g_mix_pre), w_qkv, w_mix, w_f, b_f, cw, wc,
                                        pm_tail, f_init, batch, ROW_TILE)
    attn = _attention(qx, kx, v, pad_rows(kxm), pad_rows(vm),
                      batch, seq, ATTN_TQ, ATTN_TK, ATTN_ROWS, ATTN_PAIRS)
    h3 = _mix_out_ffn(h1, attn, sga, gcv, wa, wo, gain(g_mix_post), gain(g_ffn2_pre),
                      gain(g_ffn2_post), w2_in, w2_out, ROW_TILE)
    return h3.reshape(batch, seq, D_MODEL)
```

```python
import functools

import jax
import jax.numpy as jnp
import numpy as np
from jax import lax
from jax.experimental import pallas as pl
from jax.experimental.pallas import tpu as pltpu

D_MODEL = 1024
D_FF = 2816
N_META = 16
ATTN_HEADS = 8
HEAD_DIM = 64
ATTN_WIDTH = ATTN_HEADS * HEAD_DIM
CONV_WIDTH = 512
CONV_K = 3
NORM_EPS = 1e-6

LANES = 128
SUBLANES = 8
HEADS_PER_BLOCK = LANES // HEAD_DIM
VMEM_LIMIT_BYTES = 56 * 1024 * 1024
MASK_VALUE = -1e30
LOG2E = 1.4426950408889634
N_SPLIT = 3

ROW_TILE = 512
FFN_TILE = 1024
FF_CHUNK = 256
ROW_GROUPS = 1
STAGE_ROWS_IN = 64
STAGE_ROWS_OUT = 128
STAGE_SLOTS = 4
ATTN_TQ = 512
ATTN_TK = 512
ATTN_ROWS = 128
ATTN_PAIRS = 4
KV_BUFFERS = 1

C_Q, C_K, C_V = 0, 512, 1024
C_CB, C_CC, C_CIN, C_GA, C_GC = 0, 512, 1024, 1536, 2560
N_QKV = 3 * ATTN_WIDTH
N_MIX = 3 * CONV_WIDTH + 2 * D_MODEL

bf16 = jnp.bfloat16
f32 = jnp.float32


def _rms_norm(x, g):
    ms = jnp.mean(x * x, axis=-1, keepdims=True)
    return x * lax.rsqrt(ms + NORM_EPS) * g


def _resident(shape):
    nd = len(shape)
    return pl.BlockSpec(shape, lambda *_: (0,) * nd, pipeline_mode=pl.Buffered(1))


def _row_groups(tm):
    n = ROW_GROUPS if tm % (ROW_GROUPS * 2 * SUBLANES) == 0 else 1
    return [slice(g * tm // n, (g + 1) * tm // n) for g in range(n)]


def _swiglu_up(u, rs, w_in_ref, act_ref):
    for c in range(D_FF // FF_CHUNK):
        lo = c * FF_CHUNK
        a = jnp.dot(u, w_in_ref[:, lo:lo + FF_CHUNK], preferred_element_type=f32)
        b = jnp.dot(u, w_in_ref[:, D_FF + lo:D_FF + lo + FF_CHUNK], preferred_element_type=f32)
        act_ref[rs, lo:lo + FF_CHUNK] = (a * jax.nn.sigmoid(a) * b).astype(bf16)


def _swiglu_groups(residuals, g_pre, g_post, w_in_ref, w_out_ref, act_ref, o_ref, groups):
    h = [None] * len(groups)
    u = [None] * len(groups)
    for g, rs in enumerate(groups + [None]):
        if rs is not None:
            h[g] = residuals[g]()
            u[g] = _rms_norm(h[g], g_pre).astype(bf16)
        if g > 0:
            y = jnp.dot(act_ref[groups[g - 1], :], w_out_ref[...], preferred_element_type=f32)
        if rs is not None:
            _swiglu_up(u[g], rs, w_in_ref, act_ref)
        if g > 0:
            o_ref[groups[g - 1], :] = h[g - 1] + 0.5 * _rms_norm(y, g_post)


def _stream_to_bf16(src_hbm, col0, dst, stage, sems, first_sem):
    slots, chunk, n = stage.shape
    n_chunks = dst.shape[0] // chunk

    def copy(c):
        return pltpu.make_async_copy(src_hbm.at[pl.ds(c * chunk, chunk), pl.ds(col0, n)],
                                     stage.at[c % slots], sems.at[first_sem + c % slots])

    for c in range(min(slots - 1, n_chunks)):
        copy(c).start()
    for c in range(n_chunks):
        if c + slots - 1 < n_chunks:
            copy(c + slots - 1).start()
        copy(c).wait()
        dst[c * chunk:(c + 1) * chunk, 0:n] = stage[c % slots].astype(bf16)


def _stage_ffn_weights(w_in_hbm, w_out_hbm, w_in_ref, w_out_ref, stage_in, stage_out, sems):
    @pl.when(pl.program_id(0) == 0)
    def _():
        _stream_to_bf16(w_in_hbm, 0, w_in_ref, stage_in, sems, 0)
        _stream_to_bf16(w_out_hbm, 0, w_out_ref, stage_out, sems, STAGE_SLOTS)


def _ffn_weight_specs():
    in_specs = [pl.BlockSpec(memory_space=pl.ANY), pl.BlockSpec(memory_space=pl.ANY)]
    scratch = [pltpu.VMEM((D_MODEL, 2 * D_FF), bf16), pltpu.VMEM((D_FF, D_MODEL), bf16),
               pltpu.VMEM((STAGE_SLOTS, STAGE_ROWS_IN, 2 * D_FF), f32),
               pltpu.VMEM((STAGE_SLOTS, STAGE_ROWS_OUT, D_MODEL), f32),
               pltpu.SemaphoreType.DMA((2 * STAGE_SLOTS,))]
    return in_specs, scratch


def _ffn_body(h_ref, g_pre_ref, g_post_ref, w_in_hbm, w_out_hbm, o_ref, act_ref,
              w_in_ref, w_out_ref, stage_in, stage_out, sems):
    _stage_ffn_weights(w_in_hbm, w_out_hbm, w_in_ref, w_out_ref, stage_in, stage_out, sems)
    groups = _row_groups(h_ref.shape[0])
    residuals = [functools.partial(lambda rs: h_ref[rs, :], rs) for rs in groups]
    _swiglu_groups(residuals, g_pre_ref[...], g_post_ref[...], w_in_ref, w_out_ref, act_ref,
                   o_ref, groups)


def _ffn(h, g_pre, g_post, w_in, w_out, tm):
    rows = h.shape[0]
    weight_specs, weight_scratch = _ffn_weight_specs()
    return pl.pallas_call(
        _ffn_body,
        grid=(rows // tm,),
        in_specs=[
            pl.BlockSpec((tm, D_MODEL), lambda i: (i, 0)),
            _resident((1, D_MODEL)),
            _resident((1, D_MODEL)),
        ] + weight_specs,
        out_specs=pl.BlockSpec((tm, D_MODEL), lambda i: (i, 0)),
        out_shape=jax.ShapeDtypeStruct((rows, D_MODEL), f32),
        scratch_shapes=[pltpu.VMEM((tm, D_FF), bf16)] + weight_scratch,
        compiler_params=pltpu.CompilerParams(
            dimension_semantics=("arbitrary",), vmem_limit_bytes=VMEM_LIMIT_BYTES),
        name="ffn_half_step",
    )(h, g_pre, g_post, w_in, w_out)


def _log_sigmoid(x):
    return jnp.minimum(x, 0.0) - jnp.log1p(jnp.exp(-jnp.abs(x)))


def _split3(x):
    hi = x.astype(bf16)
    r = x - hi.astype(f32)
    mid = r.astype(bf16)
    lo = (r - mid.astype(f32)).astype(bf16)
    return hi, mid, lo


def _pack3(x, head_lanes):
    hi, mid, lo = _split3(jnp.where(head_lanes, x, 0.0))
    packed = (hi.astype(f32) + pltpu.roll(mid.astype(f32), ATTN_HEADS, 1)
              + pltpu.roll(lo.astype(f32), 2 * ATTN_HEADS, 1))
    return packed.astype(bf16)


def _mix_in_body(h_ref, g_ref, w_qkv_ref, w_mix_ref, w_f_ref, bf_ref, cw_ref, wc_ref, p_init_ref, f_init_ref,
                 ksel_ref, qsel_ref,
                 qx_ref, kx_ref, v_ref, fsum_ref, sga_ref, gcv_ref, p_tail_ref,
                 p_scr, f_carry, *, tm):
    @pl.when(pl.program_id(1) == 0)
    def _():
        p_scr[0:SUBLANES, :] = p_init_ref[...]
        f_carry[...] = f_init_ref[...]

    u = _rms_norm(h_ref[...], g_ref[...]).astype(bf16)

    def proj(w_ref, lo, width):
        return jnp.dot(u, w_ref[:, lo:lo + width], preferred_element_type=f32)

    rb = min(tm, LANES)
    n_blk = tm // rb
    row_blk = [slice(blk * rb, (blk + 1) * rb) for blk in range(n_blk)]
    n_pairs = ATTN_HEADS // HEADS_PER_BLOCK
    head_lanes = lax.broadcasted_iota(jnp.int32, (rb, LANES), 1) < ATTN_HEADS
    log_f = _log_sigmoid(proj(w_f_ref, 0, LANES) + bf_ref[...]) * LOG2E

    q = (proj(w_qkv_ref, C_Q, ATTN_WIDTH) * (LOG2E * HEAD_DIM ** -0.5)).astype(bf16)
    lane = lax.broadcasted_iota(jnp.int32, (tm, LANES), 1)
    for pair in range(n_pairs):
        q2 = q[:, pair * LANES:(pair + 1) * LANES]
        zeros = jnp.zeros_like(q2)
        c0 = 2 * pair * HEADS_PER_BLOCK * LANES
        qx_ref[:, c0:c0 + LANES] = jnp.where(lane < HEAD_DIM, q2, zeros)
        qx_ref[:, c0 + 2 * LANES:c0 + 3 * LANES] = jnp.where(lane < HEAD_DIM, zeros, q2)

    tri = (lax.broadcasted_iota(jnp.int32, (rb, rb), 0)
           >= lax.broadcasted_iota(jnp.int32, (rb, rb), 1)).astype(bf16)
    c3 = [jnp.dot(tri, _pack3(log_f[rs, :], head_lanes), preferred_element_type=f32)
          for rs in row_blk]

    k = proj(w_qkv_ref, C_K, ATTN_WIDTH)
    for pair in range(n_pairs):
        c0 = pair * HEADS_PER_BLOCK * LANES
        kx_ref[:, c0:c0 + LANES] = k[:, pair * LANES:(pair + 1) * LANES].astype(bf16)

    carry = f_carry[...]
    pieces = []
    for rs, c in zip(row_blk, c3):
        cs = c + pltpu.roll(c, LANES - ATTN_HEADS, 1) + pltpu.roll(c, LANES - 2 * ATTN_HEADS, 1)
        cs = jnp.where(head_lanes, cs, 0.0) + carry
        carry = cs[rb - 1:rb, :]
        fsum_ref[rs, :] = cs[:, :ATTN_HEADS]
        pieces.append(_pack3(cs, head_lanes))
    f_carry[...] = carry

    v = proj(w_qkv_ref, C_V, ATTN_WIDTH)
    ones_a = jnp.where(lane == HEAD_DIM, 1.0, 0.0)
    ones_b = jnp.where(lane == 0, 1.0, 0.0)
    for pair in range(n_pairs):
        v2 = v[:, pair * LANES:(pair + 1) * LANES]
        c0 = pair * HEADS_PER_BLOCK * LANES
        v_ref[:, c0:c0 + LANES] = jnp.where(lane < HEAD_DIM, v2, ones_a).astype(bf16)
        v_ref[:, c0 + LANES:c0 + 2 * LANES] = jnp.where(lane < HEAD_DIM, ones_b, v2).astype(bf16)

    k_lane = lax.broadcasted_iota(jnp.int32, (rb, n_pairs * LANES), 1)
    k_ones = jnp.bitwise_and(k_lane, LANES - 1) < N_SPLIT
    q_lane = lax.broadcasted_iota(jnp.int32, (rb, ATTN_HEADS * LANES), 1)
    q_head = jnp.right_shift(q_lane, LANES.bit_length() - 1)
    q_ones_lo = N_SPLIT * (1 + jnp.bitwise_and(q_head, HEADS_PER_BLOCK - 1))
    q_in_blk = jnp.bitwise_and(q_lane, LANES - 1)
    q_ones = (q_in_blk >= q_ones_lo) & (q_in_blk < q_ones_lo + N_SPLIT)
    for rs, pc in zip(row_blk, pieces):
        k_bias = jnp.dot(pc, ksel_ref[...], preferred_element_type=f32)
        k_bias = jnp.where(k_ones, 1.0, k_bias).astype(bf16)
        for pair in range(n_pairs):
            c0 = pair * HEADS_PER_BLOCK * LANES + LANES
            kx_ref[rs, c0:c0 + LANES] = k_bias[:, pair * LANES:(pair + 1) * LANES]
        q_bias = jnp.dot(pc, qsel_ref[...], preferred_element_type=f32)
        q_bias = jnp.where(q_ones, 1.0, q_bias).astype(bf16)
        for head in range(ATTN_HEADS):
            c0 = head * HEADS_PER_BLOCK * LANES + LANES
            qx_ref[rs, c0:c0 + LANES] = q_bias[:, head * LANES:(head + 1) * LANES]

    p = proj(w_mix_ref, C_CC, CONV_WIDTH) * proj(w_mix_ref, C_CIN, CONV_WIDTH)
    p_scr[SUBLANES:SUBLANES + tm, :] = p
    conv = (p_scr[SUBLANES - 2:SUBLANES - 2 + tm, :] * cw_ref[0:1, :]
            + p_scr[SUBLANES - 1:SUBLANES - 1 + tm, :] * cw_ref[1:2, :]
            + p * cw_ref[2:3, :])
    tail = p_scr[tm:tm + SUBLANES, :]
    p_scr[0:SUBLANES, :] = tail
    p_tail_ref[...] = tail
    conv_in = (proj(w_mix_ref, C_CB, CONV_WIDTH) * conv).astype(bf16)
    y_conv = jnp.dot(conv_in, wc_ref[...], preferred_element_type=f32)
    sga_ref[...] = jax.nn.sigmoid(proj(w_mix_ref, C_GA, D_MODEL)).astype(bf16)
    gcv_ref[...] = (jax.nn.sigmoid(proj(w_mix_ref, C_GC, D_MODEL)) * y_conv).astype(bf16)


def _bias_selectors():
    ksel = np.zeros((LANES, ATTN_HEADS // HEADS_PER_BLOCK * LANES), np.float32)
    qsel = np.zeros((LANES, ATTN_HEADS * LANES), np.float32)
    for h in range(ATTN_HEADS):
        for t in range(N_SPLIT):
            col = (h // HEADS_PER_BLOCK) * LANES + N_SPLIT * (1 + h % HEADS_PER_BLOCK) + t
            ksel[t * ATTN_HEADS + h, col] = -1.0
            qsel[t * ATTN_HEADS + h, h * LANES + t] = 1.0
    return jnp.asarray(ksel, bf16), jnp.asarray(qsel, bf16)


def _mix_in(h, g, w_qkv, w_mix, w_f, b_f, conv_w, w_conv_branch, p_init, f_init, batch, tm):
    rows = h.shape[0]
    nt = rows // (batch * tm)
    row_blk = lambda width: pl.BlockSpec((tm, width), lambda b, t: (b * nt + t, 0))
    out_rows = lambda width, dt: jax.ShapeDtypeStruct((rows, width), dt)
    return pl.pallas_call(
        functools.partial(_mix_in_body, tm=tm),
        grid=(batch, nt),
        in_specs=[
            row_blk(D_MODEL),
            _resident((1, D_MODEL)),
            _resident((D_MODEL, N_QKV)),
            _resident((D_MODEL, N_MIX)),
            _resident((D_MODEL, LANES)),
            _resident((1, LANES)),
            _resident((CONV_K, CONV_WIDTH)),
            _resident((CONV_WIDTH, D_MODEL)),
            _resident((SUBLANES, CONV_WIDTH)),
            _resident((1, LANES)),
            _resident((LANES, ATTN_HEADS // HEADS_PER_BLOCK * LANES)),
            _resident((LANES, ATTN_HEADS * LANES)),
        ],
        out_specs=[
            row_blk(2 * ATTN_HEADS * LANES), row_blk(ATTN_HEADS * LANES),
            row_blk(ATTN_HEADS * LANES),
            row_blk(ATTN_HEADS), row_blk(D_MODEL), row_blk(D_MODEL),
            pl.BlockSpec((SUBLANES, CONV_WIDTH), lambda b, t: (b * nt + t, 0)),
        ],
        out_shape=[
            out_rows(2 * ATTN_HEADS * LANES, bf16), out_rows(ATTN_HEADS * LANES, bf16),
            out_rows(ATTN_HEADS * LANES, bf16),
            out_rows(ATTN_HEADS, f32), out_rows(D_MODEL, bf16), out_rows(D_MODEL, bf16),
            jax.ShapeDtypeStruct((batch * nt * SUBLANES, CONV_WIDTH), f32),
        ],
        scratch_shapes=[pltpu.VMEM((tm + SUBLANES, CONV_WIDTH), f32),
                        pltpu.VMEM((1, LANES), f32)],
        compiler_params=pltpu.CompilerParams(
            dimension_semantics=("arbitrary", "arbitrary"), vmem_limit_bytes=VMEM_LIMIT_BYTES),
        name="mixer_input",
    )(h, g, w_qkv, w_mix, w_f, b_f, conv_w, w_conv_branch, p_init, f_init, *_bias_selectors())


def _attn_body(q_ref, k_ref, v_ref, km_ref, vm_ref, o_ref,
               m_scr, alpha_scr, acc_scr, s_scr, sm_scr, p_scr, pm_scr, *, tq, tk, rows, pairs):
    i = pl.program_id(2)
    low_half = lax.broadcasted_iota(jnp.int32, (tq, LANES), 1) < HEAD_DIM
    head_lanes = HEADS_PER_BLOCK * LANES
    m_scr[...] = jnp.full(m_scr.shape, MASK_VALUE, f32)
    acc_scr[...] = jnp.zeros(acc_scr.shape, f32)
    contract_last = (((1,), (1,)), ((), ()))
    lane = lax.broadcasted_iota(jnp.int32, (rows, LANES), 1)
    sub = lax.broadcasted_iota(jnp.int32, (rows, LANES), 0)
    n_heads = pairs * HEADS_PER_BLOCK
    assert n_heads % 2 == 0

    def scores(hh, keys_ref, key_rows):
        pair = hh // HEADS_PER_BLOCK
        q_head = q_ref[:, hh * head_lanes:(hh + 1) * head_lanes]
        k_pair = keys_ref[key_rows, pair * head_lanes:(pair + 1) * head_lanes]
        return lax.dot_general(q_head, k_pair, contract_last, preferred_element_type=f32)

    def softmax_rows(hh, r0, parts):
        rs = slice(r0, r0 + rows)
        blocks = [blk if valid is None else jnp.where(valid, blk, MASK_VALUE)
                  for blk, valid, _, _ in parts]
        m_old = m_scr[hh, rs, :]
        m_blk = functools.reduce(jnp.maximum, blocks)
        m_new = jnp.maximum(
            m_old, jnp.broadcast_to(jnp.max(m_blk, axis=-1, keepdims=True), (rows, LANES)))
        for blk, (_, _, dst, c0) in zip(blocks, parts):
            dst[hh % 2, rs, c0:c0 + LANES] = jnp.exp2(blk - m_new).astype(bf16)
        alpha_scr[hh, rs, :] = jnp.exp2(m_old - m_new)
        m_scr[hh, rs, :] = m_new

    def accumulate(hh, pv):
        acc_scr[hh] = alpha_scr[hh] * acc_scr[hh] + pv

    s_scr[0] = scores(0, k_ref, pl.ds(0, tk))

    def full_step(j, carry):
        start = pl.multiple_of(j * tk, tk)
        for hh in range(n_heads):
            slot = hh % 2
            if hh + 1 < n_heads:
                s_scr[1 - slot] = scores(hh + 1, k_ref, pl.ds(start, tk))
            else:
                s_scr[1 - slot] = scores(0, k_ref, pl.ds(pl.multiple_of(start + tk, tk), tk))
            for r0 in range(0, tq, rows):
                softmax_rows(hh, r0, [
                    (s_scr[slot, r0:r0 + rows, c * LANES:(c + 1) * LANES], None, p_scr, c * LANES)
                    for c in range(tk // LANES)])
            accumulate(hh, jnp.dot(p_scr[slot], v_ref[pl.ds(start, tk), hh * LANES:(hh + 1) * LANES],
                                   preferred_element_type=f32))
        return carry

    lax.fori_loop(0, lax.shift_right_logical(i, 1),
                  lambda jj, c: full_step(2 * jj + 1, full_step(2 * jj, c)), 0)

    @pl.when(jnp.bitwise_and(i, 1) == 1)
    def _():
        full_step(i - 1, 0)

    start = pl.multiple_of(i * tk, tk)
    meta_valid = lane < N_META
    sm_scr[0] = scores(0, km_ref, slice(None))
    for hh in range(n_heads):
        slot = hh % 2
        if hh + 1 < n_heads:
            s_scr[1 - slot] = scores(hh + 1, k_ref, pl.ds(start, tk))
            sm_scr[1 - slot] = scores(hh + 1, km_ref, slice(None))
        for r0 in range(0, tq, rows):
            n_cols = r0 + rows
            parts = [(s_scr[slot, r0:r0 + rows, c * LANES:(c + 1) * LANES],
                      None if (c + 1) * LANES - 1 <= r0 else (lane + c * LANES <= sub + r0),
                      p_scr, c * LANES) for c in range(n_cols // LANES)]
            parts.append((sm_scr[slot, r0:r0 + rows, :], meta_valid, pm_scr, 0))
            softmax_rows(hh, r0, parts)
            if n_cols < tk:
                p_scr[slot, r0:r0 + rows, n_cols:tk] = jnp.zeros((rows, tk - n_cols), bf16)
        accumulate(hh, jnp.dot(p_scr[slot], v_ref[pl.ds(start, tk), hh * LANES:(hh + 1) * LANES],
                               preferred_element_type=f32)
                   + jnp.dot(pm_scr[slot], vm_ref[:, hh * LANES:(hh + 1) * LANES],
                             preferred_element_type=f32))

    for pair in range(pairs):
        acc_a, acc_b = acc_scr[HEADS_PER_BLOCK * pair], acc_scr[HEADS_PER_BLOCK * pair + 1]
        out = jnp.where(low_half, acc_a / acc_a[:, HEAD_DIM:HEAD_DIM + 1], acc_b / acc_b[:, 0:1])
        o_ref[:, pair * LANES:(pair + 1) * LANES] = out.astype(bf16)


def _attention(qx, kx, v_aug, kx_meta, v_meta, batch, seq, tq, tk, rows, pairs):
    nq = seq // tq
    n_heads = pairs * HEADS_PER_BLOCK
    n_blk = ATTN_HEADS // n_heads
    pair_lanes = n_heads * LANES
    q_spec = pl.BlockSpec((tq, HEADS_PER_BLOCK * pair_lanes), lambda b, p, i: (b * nq + i, p))
    o_spec = pl.BlockSpec((tq, pairs * LANES), lambda b, p, i: (b * nq + i, p))
    kv_spec = pl.BlockSpec((seq, pair_lanes), lambda b, p, i: (b, p),
                           pipeline_mode=pl.Buffered(KV_BUFFERS))
    meta_spec = pl.BlockSpec((LANES, pair_lanes), lambda b, p, i: (0, p))
    per_head = lambda width, dt: pltpu.VMEM((n_heads, tq, width), dt)
    ring = lambda width, dt: pltpu.VMEM((2, tq, width), dt)
    return pl.pallas_call(
        functools.partial(_attn_body, tq=tq, tk=tk, rows=rows, pairs=pairs),
        grid=(batch, n_blk, nq),
        in_specs=[q_spec, kv_spec, kv_spec, meta_spec, meta_spec],
        out_specs=o_spec,
        out_shape=jax.ShapeDtypeStruct((batch * seq, ATTN_WIDTH), bf16),
        scratch_shapes=[per_head(LANES, f32), per_head(LANES, f32), per_head(LANES, f32),
                        ring(tk, f32), ring(LANES, f32), ring(tk, bf16), ring(LANES, bf16)],
        compiler_params=pltpu.CompilerParams(
            dimension_semantics=("arbitrary", "arbitrary", "arbitrary"),
            vmem_limit_bytes=VMEM_LIMIT_BYTES),
        name="forgetting_attention",
    )(qx, kx, v_aug, kx_meta, v_meta)


def _mix_out_ffn_body(h_ref, attn_ref, sga_ref, gcv_ref, wa_ref, wo_ref, g_mix_ref,
                      g_pre_ref, g_post_ref, w_in_hbm, w_out_hbm, o_ref, act_ref,
                      w_in_ref, w_out_ref, stage_in, stage_out, sems):
    _stage_ffn_weights(w_in_hbm, w_out_hbm, w_in_ref, w_out_ref, stage_in, stage_out, sems)

    def mixer_residual(rs):
        y_attn = jnp.dot(attn_ref[rs, :], wa_ref[...], preferred_element_type=f32)
        gated = sga_ref[rs, :].astype(f32) * y_attn + gcv_ref[rs, :].astype(f32)
        mixed = jnp.dot(gated.astype(bf16), wo_ref[...], preferred_element_type=f32)
        return h_ref[rs, :] + _rms_norm(mixed, g_mix_ref[...])

    groups = _row_groups(h_ref.shape[0])
    residuals = [functools.partial(mixer_residual, rs) for rs in groups]
    _swiglu_groups(residuals, g_pre_ref[...], g_post_ref[...], w_in_ref, w_out_ref, act_ref,
                   o_ref, groups)


def _mix_out_ffn(h, attn, sga, gcv, w_attn_branch, w_out, g_mix, g_pre, g_post, w_in, w_ffn_out, tm):
    rows = h.shape[0]
    row_blk = lambda width: pl.BlockSpec((tm, width), lambda i: (i, 0))
    weight_specs, weight_scratch = _ffn_weight_specs()
    return pl.pallas_call(
        _mix_out_ffn_body,
        grid=(rows // tm,),
        in_specs=[row_blk(D_MODEL), row_blk(ATTN_WIDTH), row_blk(D_MODEL), row_blk(D_MODEL),
                  _resident((ATTN_WIDTH, D_MODEL)), _resident((D_MODEL, D_MODEL)),
                  _resident((1, D_MODEL)), _resident((1, D_MODEL)), _resident((1, D_MODEL))]
        + weight_specs,
        out_specs=row_blk(D_MODEL),
        out_shape=jax.ShapeDtypeStruct((rows, D_MODEL), f32),
        scratch_shapes=[pltpu.VMEM((tm, D_FF), bf16)] + weight_scratch,
        compiler_params=pltpu.CompilerParams(
            dimension_semantics=("arbitrary",), vmem_limit_bytes=VMEM_LIMIT_BYTES),
        name="mixer_output_ffn",
    )(h, attn, sga, gcv, w_attn_branch, w_out, g_mix, g_pre, g_post, w_in, w_ffn_out)


def kernel(x, meta_tokens, w_in, b_forget, conv_w, w_attn_branch, w_conv_branch, w_out,
           g_ffn1_pre, g_ffn1_post, w_ffn1_in, w_ffn1_out, g_mix_pre, g_mix_post,
           g_ffn2_pre, g_ffn2_post, w_ffn2_in, w_ffn2_out):
    batch, seq, d = x.shape
    assert d == D_MODEL and w_in.shape[0] == 1 and meta_tokens.shape == (N_META, D_MODEL)
    assert seq % ROW_TILE == 0 and seq % FFN_TILE == 0 and seq % ATTN_TQ == 0
    assert ATTN_TQ == ATTN_TK and ATTN_TQ % ATTN_ROWS == 0 and ATTN_ROWS % LANES == 0

    gain = lambda g: g[0].reshape(1, D_MODEL).astype(f32)
    w = w_in[0].astype(bf16)
    w_qkv = w[:, :N_QKV]
    w_mix = w[:, N_QKV + ATTN_HEADS:]
    w_f = jnp.pad(w[:, N_QKV:N_QKV + ATTN_HEADS], ((0, 0), (0, LANES - ATTN_HEADS)))
    b_f = jnp.pad(b_forget[0].astype(f32), (0, LANES - ATTN_HEADS)).reshape(1, LANES)
    cw = conv_w[0].astype(f32)
    wc = w_conv_branch[0].astype(bf16)
    wa = w_attn_branch[0].astype(bf16)
    wo = w_out[0].astype(bf16)
    w1_in, w1_out = w_ffn1_in[0], w_ffn1_out[0]
    w2_in, w2_out = w_ffn2_in[0], w_ffn2_out[0]

    hm = _ffn(meta_tokens.astype(f32), gain(g_ffn1_pre), gain(g_ffn1_post), w1_in, w1_out, N_META)
    zeros_p = jnp.zeros((SUBLANES, CONV_WIDTH), f32)
    zeros_f = jnp.zeros((1, LANES), f32)
    _, kxm, vm, fm, _, _, pm_tail = _mix_in(hm, gain(g_mix_pre), w_qkv, w_mix, w_f, b_f, cw, wc,
                                           zeros_p, zeros_f, 1, N_META)
    pad_rows = lambda a: jnp.pad(a, ((0, LANES - N_META), (0, 0)))
    f_init = jnp.pad(fm[N_META - 1:N_META, :], ((0, 0), (0, LANES - ATTN_HEADS)))

    rows = batch * seq
    h1 = _ffn(x.reshape(rows, D_MODEL), gain(g_ffn1_pre), gain(g_ffn1_post), w1_in, w1_out, FFN_TILE)
    qx, kx, v, _, sga, gcv, _ = _mix_in(h1, gain(g_mix_pre), w_qkv, w_mix, w_f, b_f, cw, wc,
                                        pm_tail, f_init, batch, ROW_TILE)
    attn = _attention(qx, kx, v, pad_rows(kxm), pad_rows(vm),
                      batch, seq, ATTN_TQ, ATTN_TK, ATTN_ROWS, ATTN_PAIRS)
    h3 = _mix_out_ffn(h1, attn, sga, gcv, wa, wo, gain(g_mix_post), gain(g_ffn2_pre),
                      gain(g_ffn2_post), w2_in, w2_out, ROW_TILE)
    return h3.reshape(batch, seq, D_MODEL)
```

```python
import functools

import jax
import jax.numpy as jnp
import numpy as np
from jax import lax
from jax.experimental import pallas as pl
from jax.experimental.pallas import tpu as pltpu

D_MODEL = 1024
D_FF = 2816
N_META = 16
ATTN_HEADS = 8
HEAD_DIM = 64
ATTN_WIDTH = ATTN_HEADS * HEAD_DIM
CONV_WIDTH = 512
CONV_K = 3
NORM_EPS = 1e-6

LANES = 128
SUBLANES = 8
HEADS_PER_BLOCK = LANES // HEAD_DIM
VMEM_LIMIT_BYTES = 56 * 1024 * 1024
MASK_VALUE = -1e30
LOG2E = 1.4426950408889634
N_SPLIT = 3

ROW_TILE = 512
FFN_TILE = 1024
REGROUP_ROWS = 128
FF_CHUNK = 256
ATTN_TQ = 512
ATTN_TK = 512
ATTN_ROWS = 128
ATTN_PAIRS = 4
KV_BUFFERS = 1

C_Q, C_K, C_V = 0, 512, 1024
C_CB, C_CC, C_CIN, C_GA, C_GC = 0, 512, 1024, 1536, 2560
N_QKV = 3 * ATTN_WIDTH
N_MIX = 3 * CONV_WIDTH + 2 * D_MODEL

bf16 = jnp.bfloat16
f32 = jnp.float32


def _rms_norm(x, g):
    ms = jnp.mean(x * x, axis=-1, keepdims=True)
    return x * lax.rsqrt(ms + NORM_EPS) * g


def _resident(shape):
    nd = len(shape)
    return pl.BlockSpec(shape, lambda *_: (0,) * nd, pipeline_mode=pl.Buffered(1))


def _swiglu_residual(h, g_pre, g_post, w_in_ref, w_out_ref, act_ref):
    u = _rms_norm(h, g_pre).astype(bf16)
    for c in range(D_FF // FF_CHUNK):
        lo = c * FF_CHUNK
        a = jnp.dot(u, w_in_ref[:, lo:lo + FF_CHUNK], preferred_element_type=f32)
        b = jnp.dot(u, w_in_ref[:, D_FF + lo:D_FF + lo + FF_CHUNK], preferred_element_type=f32)
        act_ref[:, lo:lo + FF_CHUNK] = (a * jax.nn.sigmoid(a) * b).astype(bf16)
    y = jnp.dot(act_ref[...], w_out_ref[...], preferred_element_type=f32)
    return h + 0.5 * _rms_norm(y, g_post)


def _ffn_body(h_ref, g_pre_ref, g_post_ref, w_in_ref, w_out_ref, o_ref, act_ref):
    o_ref[...] = _swiglu_residual(h_ref[...], g_pre_ref[...], g_post_ref[...],
                                  w_in_ref, w_out_ref, act_ref)


def _ffn(h, g_pre, g_post, w_in, w_out, tm):
    rows = h.shape[0]
    return pl.pallas_call(
        _ffn_body,
        grid=(rows // tm,),
        in_specs=[
            pl.BlockSpec((tm, D_MODEL), lambda i: (i, 0)),
            _resident((1, D_MODEL)),
            _resident((1, D_MODEL)),
            _resident((D_MODEL, 2 * D_FF)),
            _resident((D_FF, D_MODEL)),
        ],
        out_specs=pl.BlockSpec((tm, D_MODEL), lambda i: (i, 0)),
        out_shape=jax.ShapeDtypeStruct((rows, D_MODEL), f32),
        scratch_shapes=[pltpu.VMEM((tm, D_FF), bf16)],
        compiler_params=pltpu.CompilerParams(
            dimension_semantics=("arbitrary",), vmem_limit_bytes=VMEM_LIMIT_BYTES),
        name="ffn_half_step",
    )(h, g_pre, g_post, w_in, w_out)


def _regroup_body(w_ref, w_qkv_ref, w_mix_ref, w_f_ref):
    w = w_ref[...]
    w_qkv_ref[...] = w[:, :N_QKV].astype(bf16)
    w_mix_ref[...] = w[:, N_QKV + ATTN_HEADS:].astype(bf16)
    w_f_ref[...] = jnp.zeros(w_f_ref.shape, bf16)
    w_f_ref[:, 0:ATTN_HEADS] = w[:, N_QKV:N_QKV + ATTN_HEADS].astype(bf16)


def _regroup_input_projection(w):
    rows, cols = w.shape
    row_blk = lambda width: pl.BlockSpec((REGROUP_ROWS, width), lambda i: (i, 0))
    return pl.pallas_call(
        _regroup_body,
        grid=(rows // REGROUP_ROWS,),
        in_specs=[row_blk(cols)],
        out_specs=[row_blk(N_QKV), row_blk(N_MIX), row_blk(LANES)],
        out_shape=[jax.ShapeDtypeStruct((rows, N_QKV), bf16),
                   jax.ShapeDtypeStruct((rows, N_MIX), bf16),
                   jax.ShapeDtypeStruct((rows, LANES), bf16)],
        compiler_params=pltpu.CompilerParams(
            dimension_semantics=("arbitrary",), vmem_limit_bytes=VMEM_LIMIT_BYTES),
        name="regroup_input_projection",
    )(w)


def _log_sigmoid(x):
    return jnp.minimum(x, 0.0) - jnp.log1p(jnp.exp(-jnp.abs(x)))


def _split3(x):
    hi = x.astype(bf16)
    r = x - hi.astype(f32)
    mid = r.astype(bf16)
    lo = (r - mid.astype(f32)).astype(bf16)
    return hi, mid, lo


def _pack3(x, head_lanes):
    hi, mid, lo = _split3(jnp.where(head_lanes, x, 0.0))
    packed = (hi.astype(f32) + pltpu.roll(mid.astype(f32), ATTN_HEADS, 1)
              + pltpu.roll(lo.astype(f32), 2 * ATTN_HEADS, 1))
    return packed.astype(bf16)


def _mix_in_body(h_ref, g_ref, w_qkv_ref, w_mix_ref, w_f_ref, bf_ref, cw_ref, wc_ref,
                 p_init_ref, f_init_ref, ksel_ref, qsel_ref,
                 qx_ref, kx_ref, v_ref, fsum_ref, sga_ref, gcv_ref, p_tail_ref,
                 p_scr, f_carry, *, tm):
    @pl.when(pl.program_id(1) == 0)
    def _():
        p_scr[0:SUBLANES, :] = p_init_ref[...]
        f_carry[...] = f_init_ref[...]

    u = _rms_norm(h_ref[...], g_ref[...]).astype(bf16)

    def proj(w_ref, lo, width):
        return jnp.dot(u, w_ref[:, lo:lo + width], preferred_element_type=f32)

    rb = min(tm, LANES)
    n_blk = tm // rb
    row_blk = [slice(blk * rb, (blk + 1) * rb) for blk in range(n_blk)]
    n_pairs = ATTN_HEADS // HEADS_PER_BLOCK
    head_lanes = lax.broadcasted_iota(jnp.int32, (rb, LANES), 1) < ATTN_HEADS
    log_f = _log_sigmoid(proj(w_f_ref, 0, LANES) + bf_ref[...]) * LOG2E

    q = (proj(w_qkv_ref, C_Q, ATTN_WIDTH) * (LOG2E * HEAD_DIM ** -0.5)).astype(bf16)
    lane = lax.broadcasted_iota(jnp.int32, (tm, LANES), 1)
    for pair in range(n_pairs):
        q2 = q[:, pair * LANES:(pair + 1) * LANES]
        zeros = jnp.zeros_like(q2)
        c0 = 2 * pair * HEADS_PER_BLOCK * LANES
        qx_ref[:, c0:c0 + LANES] = jnp.where(lane < HEAD_DIM, q2, zeros)
        qx_ref[:, c0 + 2 * LANES:c0 + 3 * LANES] = jnp.where(lane < HEAD_DIM, zeros, q2)

    tri = (lax.broadcasted_iota(jnp.int32, (rb, rb), 0)
           >= lax.broadcasted_iota(jnp.int32, (rb, rb), 1)).astype(bf16)
    c3 = [jnp.dot(tri, _pack3(log_f[rs, :], head_lanes), preferred_element_type=f32)
          for rs in row_blk]

    k = proj(w_qkv_ref, C_K, ATTN_WIDTH)
    for pair in range(n_pairs):
        c0 = pair * HEADS_PER_BLOCK * LANES
        kx_ref[:, c0:c0 + LANES] = k[:, pair * LANES:(pair + 1) * LANES].astype(bf16)

    carry = f_carry[...]
    pieces = []
    for rs, c in zip(row_blk, c3):
        cs = c + pltpu.roll(c, LANES - ATTN_HEADS, 1) + pltpu.roll(c, LANES - 2 * ATTN_HEADS, 1)
        cs = jnp.where(head_lanes, cs, 0.0) + carry
        carry = cs[rb - 1:rb, :]
        fsum_ref[rs, :] = cs[:, :ATTN_HEADS]
        pieces.append(_pack3(cs, head_lanes))
    f_carry[...] = carry

    v = proj(w_qkv_ref, C_V, ATTN_WIDTH)
    ones_a = jnp.where(lane == HEAD_DIM, 1.0, 0.0)
    ones_b = jnp.where(lane == 0, 1.0, 0.0)
    for pair in range(n_pairs):
        v2 = v[:, pair * LANES:(pair + 1) * LANES]
        c0 = pair * HEADS_PER_BLOCK * LANES
        v_ref[:, c0:c0 + LANES] = jnp.where(lane < HEAD_DIM, v2, ones_a).astype(bf16)
        v_ref[:, c0 + LANES:c0 + 2 * LANES] = jnp.where(lane < HEAD_DIM, ones_b, v2).astype(bf16)

    k_lane = lax.broadcasted_iota(jnp.int32, (rb, n_pairs * LANES), 1)
    k_ones = jnp.bitwise_and(k_lane, LANES - 1) < N_SPLIT
    q_lane = lax.broadcasted_iota(jnp.int32, (rb, ATTN_HEADS * LANES), 1)
    q_head = jnp.right_shift(q_lane, LANES.bit_length() - 1)
    q_ones_lo = N_SPLIT * (1 + jnp.bitwise_and(q_head, HEADS_PER_BLOCK - 1))
    q_in_blk = jnp.bitwise_and(q_lane, LANES - 1)
    q_ones = (q_in_blk >= q_ones_lo) & (q_in_blk < q_ones_lo + N_SPLIT)
    for rs, pc in zip(row_blk, pieces):
        k_bias = jnp.dot(pc, ksel_ref[...], preferred_element_type=f32)
        k_bias = jnp.where(k_ones, 1.0, k_bias).astype(bf16)
        for pair in range(n_pairs):
            c0 = pair * HEADS_PER_BLOCK * LANES + LANES
            kx_ref[rs, c0:c0 + LANES] = k_bias[:, pair * LANES:(pair + 1) * LANES]
        q_bias = jnp.dot(pc, qsel_ref[...], preferred_element_type=f32)
        q_bias = jnp.where(q_ones, 1.0, q_bias).astype(bf16)
        for head in range(ATTN_HEADS):
            c0 = head * HEADS_PER_BLOCK * LANES + LANES
            qx_ref[rs, c0:c0 + LANES] = q_bias[:, head * LANES:(head + 1) * LANES]

    p = proj(w_mix_ref, C_CC, CONV_WIDTH) * proj(w_mix_ref, C_CIN, CONV_WIDTH)
    p_scr[SUBLANES:SUBLANES + tm, :] = p
    conv = (p_scr[SUBLANES - 2:SUBLANES - 2 + tm, :] * cw_ref[0:1, :]
            + p_scr[SUBLANES - 1:SUBLANES - 1 + tm, :] * cw_ref[1:2, :]
            + p * cw_ref[2:3, :])
    tail = p_scr[tm:tm + SUBLANES, :]
    p_scr[0:SUBLANES, :] = tail
    p_tail_ref[...] = tail
    conv_in = (proj(w_mix_ref, C_CB, CONV_WIDTH) * conv).astype(bf16)
    y_conv = jnp.dot(conv_in, wc_ref[...], preferred_element_type=f32)
    sga_ref[...] = jax.nn.sigmoid(proj(w_mix_ref, C_GA, D_MODEL)).astype(bf16)
    gcv_ref[...] = (jax.nn.sigmoid(proj(w_mix_ref, C_GC, D_MODEL)) * y_conv).astype(bf16)


def _bias_selectors():
    ksel = np.zeros((LANES, ATTN_HEADS // HEADS_PER_BLOCK * LANES), np.float32)
    qsel = np.zeros((LANES, ATTN_HEADS * LANES), np.float32)
    for h in range(ATTN_HEADS):
        for t in range(N_SPLIT):
            col = (h // HEADS_PER_BLOCK) * LANES + N_SPLIT * (1 + h % HEADS_PER_BLOCK) + t
            ksel[t * ATTN_HEADS + h, col] = -1.0
            qsel[t * ATTN_HEADS + h, h * LANES + t] = 1.0
    return jnp.asarray(ksel, bf16), jnp.asarray(qsel, bf16)


def _mix_in(h, g, w_qkv, w_mix, w_f, b_f, conv_w, w_conv_branch, p_init, f_init, batch, tm):
    rows = h.shape[0]
    nt = rows // (batch * tm)
    row_blk = lambda width: pl.BlockSpec((tm, width), lambda b, t: (b * nt + t, 0))
    out_rows = lambda width, dt: jax.ShapeDtypeStruct((rows, width), dt)
    return pl.pallas_call(
        functools.partial(_mix_in_body, tm=tm),
        grid=(batch, nt),
        in_specs=[
            row_blk(D_MODEL),
            _resident((1, D_MODEL)),
            _resident((D_MODEL, N_QKV)),
            _resident((D_MODEL, N_MIX)),
            _resident((D_MODEL, LANES)),
            _resident((1, LANES)),
            _resident((CONV_K, CONV_WIDTH)),
            _resident((CONV_WIDTH, D_MODEL)),
            _resident((SUBLANES, CONV_WIDTH)),
            _resident((1, LANES)),
            _resident((LANES, ATTN_HEADS // HEADS_PER_BLOCK * LANES)),
            _resident((LANES, ATTN_HEADS * LANES)),
        ],
        out_specs=[
            row_blk(2 * ATTN_HEADS * LANES), row_blk(ATTN_HEADS * LANES),
            row_blk(ATTN_HEADS * LANES),
            row_blk(ATTN_HEADS), row_blk(D_MODEL), row_blk(D_MODEL),
            pl.BlockSpec((SUBLANES, CONV_WIDTH), lambda b, t: (b * nt + t, 0)),
        ],
        out_shape=[
            out_rows(2 * ATTN_HEADS * LANES, bf16), out_rows(ATTN_HEADS * LANES, bf16),
            out_rows(ATTN_HEADS * LANES, bf16),
            out_rows(ATTN_HEADS, f32), out_rows(D_MODEL, bf16), out_rows(D_MODEL, bf16),
            jax.ShapeDtypeStruct((batch * nt * SUBLANES, CONV_WIDTH), f32),
        ],
        scratch_shapes=[pltpu.VMEM((tm + SUBLANES, CONV_WIDTH), f32),
                        pltpu.VMEM((1, LANES), f32)],
        compiler_params=pltpu.CompilerParams(
            dimension_semantics=("arbitrary", "arbitrary"), vmem_limit_bytes=VMEM_LIMIT_BYTES),
        name="mixer_input",
    )(h, g, w_qkv, w_mix, w_f, b_f, conv_w, w_conv_branch, p_init, f_init, *_bias_selectors())


def _attn_body(q_ref, k_ref, v_ref, km_ref, vm_ref, o_ref,
               m_scr, alpha_scr, acc_scr, s_scr, sm_scr, p_scr, pm_scr, *, tq, tk, rows, pairs):
    i = pl.program_id(2)
    low_half = lax.broadcasted_iota(jnp.int32, (tq, LANES), 1) < HEAD_DIM
    head_lanes = HEADS_PER_BLOCK * LANES
    m_scr[...] = jnp.full(m_scr.shape, MASK_VALUE, f32)
    acc_scr[...] = jnp.zeros(acc_scr.shape, f32)
    contract_last = (((1,), (1,)), ((), ()))
    lane = lax.broadcasted_iota(jnp.int32, (rows, LANES), 1)
    sub = lax.broadcasted_iota(jnp.int32, (rows, LANES), 0)
    n_heads = pairs * HEADS_PER_BLOCK
    assert n_heads % 2 == 0

    def scores(hh, keys_ref, key_rows):
        pair = hh // HEADS_PER_BLOCK
        q_head = q_ref[:, hh * head_lanes:(hh + 1) * head_lanes]
        k_pair = keys_ref[key_rows, pair * head_lanes:(pair + 1) * head_lanes]
        return lax.dot_general(q_head, k_pair, contract_last, preferred_element_type=f32)

    def softmax_rows(hh, r0, parts):
        rs = slice(r0, r0 + rows)
        blocks = [blk if valid is None else jnp.where(valid, blk, MASK_VALUE)
                  for blk, valid, _, _ in parts]
        m_old = m_scr[hh, rs, :]
        m_blk = functools.reduce(jnp.maximum, blocks)
        m_new = jnp.maximum(
            m_old, jnp.broadcast_to(jnp.max(m_blk, axis=-1, keepdims=True), (rows, LANES)))
        for blk, (_, _, dst, c0) in zip(blocks, parts):
            dst[hh % 2, rs, c0:c0 + LANES] = jnp.exp2(blk - m_new).astype(bf16)
        alpha_scr[hh, rs, :] = jnp.exp2(m_old - m_new)
        m_scr[hh, rs, :] = m_new

    def accumulate(hh, pv):
        acc_scr[hh] = alpha_scr[hh] * acc_scr[hh] + pv

    s_scr[0] = scores(0, k_ref, pl.ds(0, tk))

    def full_step(j, carry):
        start = pl.multiple_of(j * tk, tk)
        for hh in range(n_heads):
            slot = hh % 2
            if hh + 1 < n_heads:
                s_scr[1 - slot] = scores(hh + 1, k_ref, pl.ds(start, tk))
            else:
                s_scr[1 - slot] = scores(0, k_ref, pl.ds(pl.multiple_of(start + tk, tk), tk))
            for r0 in range(0, tq, rows):
                softmax_rows(hh, r0, [
                    (s_scr[slot, r0:r0 + rows, c * LANES:(c + 1) * LANES], None, p_scr, c * LANES)
                    for c in range(tk // LANES)])
            accumulate(hh, jnp.dot(p_scr[slot], v_ref[pl.ds(start, tk), hh * LANES:(hh + 1) * LANES],
                                   preferred_element_type=f32))
        return carry

    lax.fori_loop(0, lax.shift_right_logical(i, 1),
                  lambda jj, c: full_step(2 * jj + 1, full_step(2 * jj, c)), 0)

    @pl.when(jnp.bitwise_and(i, 1) == 1)
    def _():
        full_step(i - 1, 0)

    start = pl.multiple_of(i * tk, tk)
    meta_valid = lane < N_META
    sm_scr[0] = scores(0, km_ref, slice(None))
    for hh in range(n_heads):
        slot = hh % 2
        if hh + 1 < n_heads:
            s_scr[1 - slot] = scores(hh + 1, k_ref, pl.ds(start, tk))
            sm_scr[1 - slot] = scores(hh + 1, km_ref, slice(None))
        for r0 in range(0, tq, rows):
            n_cols = r0 + rows
            parts = [(s_scr[slot, r0:r0 + rows, c * LANES:(c + 1) * LANES],
                      None if (c + 1) * LANES - 1 <= r0 else (lane + c * LANES <= sub + r0),
                      p_scr, c * LANES) for c in range(n_cols // LANES)]
            parts.append((sm_scr[slot, r0:r0 + rows, :], meta_valid, pm_scr, 0))
            softmax_rows(hh, r0, parts)
            if n_cols < tk:
                p_scr[slot, r0:r0 + rows, n_cols:tk] = jnp.zeros((rows, tk - n_cols), bf16)
        accumulate(hh, jnp.dot(p_scr[slot], v_ref[pl.ds(start, tk), hh * LANES:(hh + 1) * LANES],
                               preferred_element_type=f32)
                   + jnp.dot(pm_scr[slot], vm_ref[:, hh * LANES:(hh + 1) * LANES],
                             preferred_element_type=f32))

    for pair in range(pairs):
        acc_a, acc_b = acc_scr[HEADS_PER_BLOCK * pair], acc_scr[HEADS_PER_BLOCK * pair + 1]
        out = jnp.where(low_half, acc_a / acc_a[:, HEAD_DIM:HEAD_DIM + 1], acc_b / acc_b[:, 0:1])
        o_ref[:, pair * LANES:(pair + 1) * LANES] = out.astype(bf16)


def _attention(qx, kx, v_aug, kx_meta, v_meta, batch, seq, tq, tk, rows, pairs):
    nq = seq // tq
    n_heads = pairs * HEADS_PER_BLOCK
    n_blk = ATTN_HEADS // n_heads
    pair_lanes = n_heads * LANES
    q_spec = pl.BlockSpec((tq, HEADS_PER_BLOCK * pair_lanes), lambda b, p, i: (b * nq + i, p))
    o_spec = pl.BlockSpec((tq, pairs * LANES), lambda b, p, i: (b * nq + i, p))
    kv_spec = pl.BlockSpec((seq, pair_lanes), lambda b, p, i: (b, p),
                           pipeline_mode=pl.Buffered(KV_BUFFERS))
    meta_spec = pl.BlockSpec((LANES, pair_lanes), lambda b, p, i: (0, p))
    per_head = lambda width, dt: pltpu.VMEM((n_heads, tq, width), dt)
    ring = lambda width, dt: pltpu.VMEM((2, tq, width), dt)
    return pl.pallas_call(
        functools.partial(_attn_body, tq=tq, tk=tk, rows=rows, pairs=pairs),
        grid=(batch, n_blk, nq),
        in_specs=[q_spec, kv_spec, kv_spec, meta_spec, meta_spec],
        out_specs=o_spec,
        out_shape=jax.ShapeDtypeStruct((batch * seq, ATTN_WIDTH), bf16),
        scratch_shapes=[per_head(LANES, f32), per_head(LANES, f32), per_head(LANES, f32),
                        ring(tk, f32), ring(LANES, f32), ring(tk, bf16), ring(LANES, bf16)],
        compiler_params=pltpu.CompilerParams(
            dimension_semantics=("arbitrary", "arbitrary", "arbitrary"),
            vmem_limit_bytes=VMEM_LIMIT_BYTES),
        name="forgetting_attention",
    )(qx, kx, v_aug, kx_meta, v_meta)


def _mix_out_ffn_body(h_ref, attn_ref, sga_ref, gcv_ref, wa_ref, wo_ref, g_mix_ref,
                      g_pre_ref, g_post_ref, w_in_ref, w_out_ref, o_ref, act_ref):
    y_attn = jnp.dot(attn_ref[...], wa_ref[...], preferred_element_type=f32)
    gated = sga_ref[...].astype(f32) * y_attn + gcv_ref[...].astype(f32)
    mixed = jnp.dot(gated.astype(bf16), wo_ref[...], preferred_element_type=f32)
    h2 = h_ref[...] + _rms_norm(mixed, g_mix_ref[...])
    o_ref[...] = _swiglu_residual(h2, g_pre_ref[...], g_post_ref[...],
                                  w_in_ref, w_out_ref, act_ref)


def _mix_out_ffn(h, attn, sga, gcv, w_attn_branch, w_out, g_mix, g_pre, g_post, w_in, w_ffn_out, tm):
    rows = h.shape[0]
    row_blk = lambda width: pl.BlockSpec((tm, width), lambda i: (i, 0))
    return pl.pallas_call(
        _mix_out_ffn_body,
        grid=(rows // tm,),
        in_specs=[row_blk(D_MODEL), row_blk(ATTN_WIDTH), row_blk(D_MODEL), row_blk(D_MODEL),
                  _resident((ATTN_WIDTH, D_MODEL)), _resident((D_MODEL, D_MODEL)),
                  _resident((1, D_MODEL)), _resident((1, D_MODEL)), _resident((1, D_MODEL)),
                  _resident((D_MODEL, 2 * D_FF)), _resident((D_FF, D_MODEL))],
        out_specs=row_blk(D_MODEL),
        out_shape=jax.ShapeDtypeStruct((rows, D_MODEL), f32),
        scratch_shapes=[pltpu.VMEM((tm, D_FF), bf16)],
        compiler_params=pltpu.CompilerParams(
            dimension_semantics=("arbitrary",), vmem_limit_bytes=VMEM_LIMIT_BYTES),
        name="mixer_output_ffn",
    )(h, attn, sga, gcv, w_attn_branch, w_out, g_mix, g_pre, g_post, w_in, w_ffn_out)


def kernel(x, meta_tokens, w_in, b_forget, conv_w, w_attn_branch, w_conv_branch, w_out,
           g_ffn1_pre, g_ffn1_post, w_ffn1_in, w_ffn1_out, g_mix_pre, g_mix_post,
           g_ffn2_pre, g_ffn2_post, w_ffn2_in, w_ffn2_out):
    batch, seq, d = x.shape
    assert d == D_MODEL and w_in.shape[0] == 1 and meta_tokens.shape == (N_META, D_MODEL)
    assert seq % ROW_TILE == 0 and seq % FFN_TILE == 0 and seq % ATTN_TQ == 0
    assert ATTN_TQ == ATTN_TK and ATTN_TQ % ATTN_ROWS == 0 and ATTN_ROWS % LANES == 0

    gain = lambda g: g[0].reshape(1, D_MODEL).astype(f32)
    w_qkv, w_mix, w_f = _regroup_input_projection(w_in[0])
    b_f = jnp.pad(b_forget[0].astype(f32), (0, LANES - ATTN_HEADS)).reshape(1, LANES)
    cw = conv_w[0].astype(f32)
    wc = w_conv_branch[0].astype(bf16)
    wa = w_attn_branch[0].astype(bf16)
    wo = w_out[0].astype(bf16)
    w1_in, w1_out = w_ffn1_in[0].astype(bf16), w_ffn1_out[0].astype(bf16)
    w2_in, w2_out = w_ffn2_in[0].astype(bf16), w_ffn2_out[0].astype(bf16)

    hm = _ffn(meta_tokens.astype(f32), gain(g_ffn1_pre), gain(g_ffn1_post), w1_in, w1_out, N_META)
    zeros_p = jnp.zeros((SUBLANES, CONV_WIDTH), f32)
    zeros_f = jnp.zeros((1, LANES), f32)
    _, kxm, vm, fm, _, _, pm_tail = _mix_in(hm, gain(g_mix_pre), w_qkv, w_mix, w_f, b_f, cw, wc,
                                           zeros_p, zeros_f, 1, N_META)
    pad_rows = lambda a: jnp.pad(a, ((0, LANES - N_META), (0, 0)))
    f_init = jnp.pad(fm[N_META - 1:N_META, :], ((0, 0), (0, LANES - ATTN_HEADS)))

    rows = batch * seq
    h1 = _ffn(x.reshape(rows, D_MODEL), gain(g_ffn1_pre), gain(g_ffn1_post), w1_in, w1_out, FFN_TILE)
    qx, kx, v, _, sga, gcv, _ = _mix_in(h1, gain(g_mix_pre), w_qkv, w_mix, w_f, b_f, cw, wc,
                                        pm_tail, f_init, batch, ROW_TILE)
    attn = _attention(qx, kx, v, pad_rows(kxm), pad_rows(vm),
                      batch, seq, ATTN_TQ, ATTN_TK, ATTN_ROWS, ATTN_PAIRS)
    h3 = _mix_out_ffn(h1, attn, sga, gcv, wa, wo, gain(g_mix_post), gain(g_ffn2_pre),
                      gain(g_ffn2_post), w2_in, w2_out, ROW_TILE)
    return h3.reshape(batch, seq, D_MODEL)
```

```python
import functools

import jax
import jax.numpy as jnp
import numpy as np
from jax import lax
from jax.experimental import pallas as pl
from jax.experimental.pallas import tpu as pltpu

D_MODEL = 1024
D_FF = 2816
N_META = 16
ATTN_HEADS = 8
HEAD_DIM = 64
ATTN_WIDTH = ATTN_HEADS * HEAD_DIM
CONV_WIDTH = 512
CONV_K = 3
NORM_EPS = 1e-6

LANES = 128
SUBLANES = 8
HEADS_PER_BLOCK = LANES // HEAD_DIM
VMEM_LIMIT_BYTES = 56 * 1024 * 1024
MASK_VALUE = -1e30
LOG2E = 1.4426950408889634
N_SPLIT = 3

ROW_TILE = 512
FFN_TILE = 1024
FF_CHUNK = 256
ATTN_TQ = 512
ATTN_TK = 512
ATTN_ROWS = 128
ATTN_PAIRS = 4
KV_BUFFERS = 1

C_Q, C_K, C_V = 0, 512, 1024
C_CB, C_CC, C_CIN, C_GA, C_GC = 0, 512, 1024, 1536, 2560
N_QKV = 3 * ATTN_WIDTH
N_MIX = 3 * CONV_WIDTH + 2 * D_MODEL

bf16 = jnp.bfloat16
f32 = jnp.float32


def _rms_norm(x, g):
    ms = jnp.mean(x * x, axis=-1, keepdims=True)
    return x * lax.rsqrt(ms + NORM_EPS) * g


def _resident(shape):
    nd = len(shape)
    return pl.BlockSpec(shape, lambda *_: (0,) * nd, pipeline_mode=pl.Buffered(1))


def _swiglu_residual(h, g_pre, g_post, w_in_ref, w_out_ref, act_ref):
    u = _rms_norm(h, g_pre).astype(bf16)
    for c in range(D_FF // FF_CHUNK):
        lo = c * FF_CHUNK
        a = jnp.dot(u, w_in_ref[:, lo:lo + FF_CHUNK], preferred_element_type=f32)
        b = jnp.dot(u, w_in_ref[:, D_FF + lo:D_FF + lo + FF_CHUNK], preferred_element_type=f32)
        act_ref[:, lo:lo + FF_CHUNK] = (a * jax.nn.sigmoid(a) * b).astype(bf16)
    y = jnp.dot(act_ref[...], w_out_ref[...], preferred_element_type=f32)
    return h + 0.5 * _rms_norm(y, g_post)


def _ffn_body(h_ref, g_pre_ref, g_post_ref, w_in_ref, w_out_ref, o_ref, act_ref):
    o_ref[...] = _swiglu_residual(h_ref[...], g_pre_ref[...], g_post_ref[...],
                                  w_in_ref, w_out_ref, act_ref)


def _ffn(h, g_pre, g_post, w_in, w_out, tm):
    rows = h.shape[0]
    return pl.pallas_call(
        _ffn_body,
        grid=(rows // tm,),
        in_specs=[
            pl.BlockSpec((tm, D_MODEL), lambda i: (i, 0)),
            _resident((1, D_MODEL)),
            _resident((1, D_MODEL)),
            _resident((D_MODEL, 2 * D_FF)),
            _resident((D_FF, D_MODEL)),
        ],
        out_specs=pl.BlockSpec((tm, D_MODEL), lambda i: (i, 0)),
        out_shape=jax.ShapeDtypeStruct((rows, D_MODEL), f32),
        scratch_shapes=[pltpu.VMEM((tm, D_FF), bf16)],
        compiler_params=pltpu.CompilerParams(
            dimension_semantics=("arbitrary",), vmem_limit_bytes=VMEM_LIMIT_BYTES),
        name="ffn_half_step",
    )(h, g_pre, g_post, w_in, w_out)


def _log_sigmoid(x):
    return jnp.minimum(x, 0.0) - jnp.log1p(jnp.exp(-jnp.abs(x)))


def _split3(x):
    hi = x.astype(bf16)
    r = x - hi.astype(f32)
    mid = r.astype(bf16)
    lo = (r - mid.astype(f32)).astype(bf16)
    return hi, mid, lo


def _pack3(x, head_lanes):
    hi, mid, lo = _split3(jnp.where(head_lanes, x, 0.0))
    packed = (hi.astype(f32) + pltpu.roll(mid.astype(f32), ATTN_HEADS, 1)
              + pltpu.roll(lo.astype(f32), 2 * ATTN_HEADS, 1))
    return packed.astype(bf16)


def _mix_in_body(h_ref, g_ref, w_qkv_ref, w_mix_ref, w_f_ref, bf_ref, cw_ref, wc_ref,
                 p_init_ref, f_init_ref, ksel_ref, qsel_ref,
                 qx_ref, kx_ref, v_ref, fsum_ref, sga_ref, gcv_ref, p_tail_ref,
                 p_scr, f_carry, *, tm):
    @pl.when(pl.program_id(1) == 0)
    def _():
        p_scr[0:SUBLANES, :] = p_init_ref[...]
        f_carry[...] = f_init_ref[...]

    u = _rms_norm(h_ref[...], g_ref[...]).astype(bf16)

    def proj(wt_ref, lo, width):
        return lax.dot_general(u, wt_ref[lo:lo + width, :], (((1,), (1,)), ((), ())),
                               preferred_element_type=f32)

    rb = min(tm, LANES)
    n_blk = tm // rb
    row_blk = [slice(blk * rb, (blk + 1) * rb) for blk in range(n_blk)]
    n_pairs = ATTN_HEADS // HEADS_PER_BLOCK
    head_lanes = lax.broadcasted_iota(jnp.int32, (rb, LANES), 1) < ATTN_HEADS
    log_f = _log_sigmoid(proj(w_f_ref, 0, LANES) + bf_ref[...]) * LOG2E

    q = (proj(w_qkv_ref, C_Q, ATTN_WIDTH) * (LOG2E * HEAD_DIM ** -0.5)).astype(bf16)
    lane = lax.broadcasted_iota(jnp.int32, (tm, LANES), 1)
    for pair in range(n_pairs):
        q2 = q[:, pair * LANES:(pair + 1) * LANES]
        zeros = jnp.zeros_like(q2)
        c0 = 2 * pair * HEADS_PER_BLOCK * LANES
        qx_ref[:, c0:c0 + LANES] = jnp.where(lane < HEAD_DIM, q2, zeros)
        qx_ref[:, c0 + 2 * LANES:c0 + 3 * LANES] = jnp.where(lane < HEAD_DIM, zeros, q2)

    tri = (lax.broadcasted_iota(jnp.int32, (rb, rb), 0)
           >= lax.broadcasted_iota(jnp.int32, (rb, rb), 1)).astype(bf16)
    c3 = [jnp.dot(tri, _pack3(log_f[rs, :], head_lanes), preferred_element_type=f32)
          for rs in row_blk]

    k = proj(w_qkv_ref, C_K, ATTN_WIDTH)
    for pair in range(n_pairs):
        c0 = pair * HEADS_PER_BLOCK * LANES
        kx_ref[:, c0:c0 + LANES] = k[:, pair * LANES:(pair + 1) * LANES].astype(bf16)

    carry = f_carry[...]
    pieces = []
    for rs, c in zip(row_blk, c3):
        cs = c + pltpu.roll(c, LANES - ATTN_HEADS, 1) + pltpu.roll(c, LANES - 2 * ATTN_HEADS, 1)
        cs = jnp.where(head_lanes, cs, 0.0) + carry
        carry = cs[rb - 1:rb, :]
        fsum_ref[rs, :] = cs[:, :ATTN_HEADS]
        pieces.append(_pack3(cs, head_lanes))
    f_carry[...] = carry

    v = proj(w_qkv_ref, C_V, ATTN_WIDTH)
    ones_a = jnp.where(lane == HEAD_DIM, 1.0, 0.0)
    ones_b = jnp.where(lane == 0, 1.0, 0.0)
    for pair in range(n_pairs):
        v2 = v[:, pair * LANES:(pair + 1) * LANES]
        c0 = pair * HEADS_PER_BLOCK * LANES
        v_ref[:, c0:c0 + LANES] = jnp.where(lane < HEAD_DIM, v2, ones_a).astype(bf16)
        v_ref[:, c0 + LANES:c0 + 2 * LANES] = jnp.where(lane < HEAD_DIM, ones_b, v2).astype(bf16)

    k_lane = lax.broadcasted_iota(jnp.int32, (rb, n_pairs * LANES), 1)
    k_ones = jnp.bitwise_and(k_lane, LANES - 1) < N_SPLIT
    q_lane = lax.broadcasted_iota(jnp.int32, (rb, ATTN_HEADS * LANES), 1)
    q_head = jnp.right_shift(q_lane, LANES.bit_length() - 1)
    q_ones_lo = N_SPLIT * (1 + jnp.bitwise_and(q_head, HEADS_PER_BLOCK - 1))
    q_in_blk = jnp.bitwise_and(q_lane, LANES - 1)
    q_ones = (q_in_blk >= q_ones_lo) & (q_in_blk < q_ones_lo + N_SPLIT)
    for rs, pc in zip(row_blk, pieces):
        k_bias = jnp.dot(pc, ksel_ref[...], preferred_element_type=f32)
        k_bias = jnp.where(k_ones, 1.0, k_bias).astype(bf16)
        for pair in range(n_pairs):
            c0 = pair * HEADS_PER_BLOCK * LANES + LANES
            kx_ref[rs, c0:c0 + LANES] = k_bias[:, pair * LANES:(pair + 1) * LANES]
        q_bias = jnp.dot(pc, qsel_ref[...], preferred_element_type=f32)
        q_bias = jnp.where(q_ones, 1.0, q_bias).astype(bf16)
        for head in range(ATTN_HEADS):
            c0 = head * HEADS_PER_BLOCK * LANES + LANES
            qx_ref[rs, c0:c0 + LANES] = q_bias[:, head * LANES:(head + 1) * LANES]

    p = proj(w_mix_ref, C_CC, CONV_WIDTH) * proj(w_mix_ref, C_CIN, CONV_WIDTH)
    p_scr[SUBLANES:SUBLANES + tm, :] = p
    conv = (p_scr[SUBLANES - 2:SUBLANES - 2 + tm, :] * cw_ref[0:1, :]
            + p_scr[SUBLANES - 1:SUBLANES - 1 + tm, :] * cw_ref[1:2, :]
            + p * cw_ref[2:3, :])
    tail = p_scr[tm:tm + SUBLANES, :]
    p_scr[0:SUBLANES, :] = tail
    p_tail_ref[...] = tail
    conv_in = (proj(w_mix_ref, C_CB, CONV_WIDTH) * conv).astype(bf16)
    y_conv = jnp.dot(conv_in, wc_ref[...], preferred_element_type=f32)
    sga_ref[...] = jax.nn.sigmoid(proj(w_mix_ref, C_GA, D_MODEL)).astype(bf16)
    gcv_ref[...] = (jax.nn.sigmoid(proj(w_mix_ref, C_GC, D_MODEL)) * y_conv).astype(bf16)


def _bias_selectors():
    ksel = np.zeros((LANES, ATTN_HEADS // HEADS_PER_BLOCK * LANES), np.float32)
    qsel = np.zeros((LANES, ATTN_HEADS * LANES), np.float32)
    for h in range(ATTN_HEADS):
        for t in range(N_SPLIT):
            col = (h // HEADS_PER_BLOCK) * LANES + N_SPLIT * (1 + h % HEADS_PER_BLOCK) + t
            ksel[t * ATTN_HEADS + h, col] = -1.0
            qsel[t * ATTN_HEADS + h, h * LANES + t] = 1.0
    return jnp.asarray(ksel, bf16), jnp.asarray(qsel, bf16)


def _mix_in(h, g, w_qkv, w_mix, w_f, b_f, conv_w, w_conv_branch, p_init, f_init, batch, tm):
    rows = h.shape[0]
    nt = rows // (batch * tm)
    row_blk = lambda width: pl.BlockSpec((tm, width), lambda b, t: (b * nt + t, 0))
    out_rows = lambda width, dt: jax.ShapeDtypeStruct((rows, width), dt)
    return pl.pallas_call(
        functools.partial(_mix_in_body, tm=tm),
        grid=(batch, nt),
        in_specs=[
            row_blk(D_MODEL),
            _resident((1, D_MODEL)),
            _resident((N_QKV, D_MODEL)),
            _resident((N_MIX, D_MODEL)),
            _resident((LANES, D_MODEL)),
            _resident((1, LANES)),
            _resident((CONV_K, CONV_WIDTH)),
            _resident((CONV_WIDTH, D_MODEL)),
            _resident((SUBLANES, CONV_WIDTH)),
            _resident((1, LANES)),
            _resident((LANES, ATTN_HEADS // HEADS_PER_BLOCK * LANES)),
            _resident((LANES, ATTN_HEADS * LANES)),
        ],
        out_specs=[
            row_blk(2 * ATTN_HEADS * LANES), row_blk(ATTN_HEADS * LANES),
            row_blk(ATTN_HEADS * LANES),
            row_blk(ATTN_HEADS), row_blk(D_MODEL), row_blk(D_MODEL),
            pl.BlockSpec((SUBLANES, CONV_WIDTH), lambda b, t: (b * nt + t, 0)),
        ],
        out_shape=[
            out_rows(2 * ATTN_HEADS * LANES, bf16), out_rows(ATTN_HEADS * LANES, bf16),
            out_rows(ATTN_HEADS * LANES, bf16),
            out_rows(ATTN_HEADS, f32), out_rows(D_MODEL, bf16), out_rows(D_MODEL, bf16),
            jax.ShapeDtypeStruct((batch * nt * SUBLANES, CONV_WIDTH), f32),
        ],
        scratch_shapes=[pltpu.VMEM((tm + SUBLANES, CONV_WIDTH), f32),
                        pltpu.VMEM((1, LANES), f32)],
        compiler_params=pltpu.CompilerParams(
            dimension_semantics=("arbitrary", "arbitrary"), vmem_limit_bytes=VMEM_LIMIT_BYTES),
        name="mixer_input",
    )(h, g, w_qkv, w_mix, w_f, b_f, conv_w, w_conv_branch, p_init, f_init, *_bias_selectors())


def _attn_body(q_ref, k_ref, v_ref, km_ref, vm_ref, o_ref,
               m_scr, alpha_scr, acc_scr, s_scr, sm_scr, p_scr, pm_scr, *, tq, tk, rows, pairs):
    i = pl.program_id(2)
    low_half = lax.broadcasted_iota(jnp.int32, (tq, LANES), 1) < HEAD_DIM
    head_lanes = HEADS_PER_BLOCK * LANES
    m_scr[...] = jnp.full(m_scr.shape, MASK_VALUE, f32)
    acc_scr[...] = jnp.zeros(acc_scr.shape, f32)
    contract_last = (((1,), (1,)), ((), ()))
    lane = lax.broadcasted_iota(jnp.int32, (rows, LANES), 1)
    sub = lax.broadcasted_iota(jnp.int32, (rows, LANES), 0)
    n_heads = pairs * HEADS_PER_BLOCK
    assert n_heads % 2 == 0

    def scores(hh, keys_ref, key_rows):
        pair = hh // HEADS_PER_BLOCK
        q_head = q_ref[:, hh * head_lanes:(hh + 1) * head_lanes]
        k_pair = keys_ref[key_rows, pair * head_lanes:(pair + 1) * head_lanes]
        return lax.dot_general(q_head, k_pair, contract_last, preferred_element_type=f32)

    def softmax_rows(hh, r0, parts):
        rs = slice(r0, r0 + rows)
        blocks = [blk if valid is None else jnp.where(valid, blk, MASK_VALUE)
                  for blk, valid, _, _ in parts]
        m_old = m_scr[hh, rs, :]
        m_blk = functools.reduce(jnp.maximum, blocks)
        m_new = jnp.maximum(
            m_old, jnp.broadcast_to(jnp.max(m_blk, axis=-1, keepdims=True), (rows, LANES)))
        for blk, (_, _, dst, c0) in zip(blocks, parts):
            dst[hh % 2, rs, c0:c0 + LANES] = jnp.exp2(blk - m_new).astype(bf16)
        alpha_scr[hh, rs, :] = jnp.exp2(m_old - m_new)
        m_scr[hh, rs, :] = m_new

    def accumulate(hh, pv):
        acc_scr[hh] = alpha_scr[hh] * acc_scr[hh] + pv

    s_scr[0] = scores(0, k_ref, pl.ds(0, tk))

    def full_step(j, carry):
        start = pl.multiple_of(j * tk, tk)
        for hh in range(n_heads):
            slot = hh % 2
            if hh + 1 < n_heads:
                s_scr[1 - slot] = scores(hh + 1, k_ref, pl.ds(start, tk))
            else:
                s_scr[1 - slot] = scores(0, k_ref, pl.ds(pl.multiple_of(start + tk, tk), tk))
            for r0 in range(0, tq, rows):
                softmax_rows(hh, r0, [
                    (s_scr[slot, r0:r0 + rows, c * LANES:(c + 1) * LANES], None, p_scr, c * LANES)
                    for c in range(tk // LANES)])
            accumulate(hh, jnp.dot(p_scr[slot], v_ref[pl.ds(start, tk), hh * LANES:(hh + 1) * LANES],
                                   preferred_element_type=f32))
        return carry

    lax.fori_loop(0, lax.shift_right_logical(i, 1),
                  lambda jj, c: full_step(2 * jj + 1, full_step(2 * jj, c)), 0)

    @pl.when(jnp.bitwise_and(i, 1) == 1)
    def _():
        full_step(i - 1, 0)

    start = pl.multiple_of(i * tk, tk)
    meta_valid = lane < N_META
    sm_scr[0] = scores(0, km_ref, slice(None))
    for hh in range(n_heads):
        slot = hh % 2
        if hh + 1 < n_heads:
            s_scr[1 - slot] = scores(hh + 1, k_ref, pl.ds(start, tk))
            sm_scr[1 - slot] = scores(hh + 1, km_ref, slice(None))
        for r0 in range(0, tq, rows):
            n_cols = r0 + rows
            parts = [(s_scr[slot, r0:r0 + rows, c * LANES:(c + 1) * LANES],
                      None if (c + 1) * LANES - 1 <= r0 else (lane + c * LANES <= sub + r0),
                      p_scr, c * LANES) for c in range(n_cols // LANES)]
            parts.append((sm_scr[slot, r0:r0 + rows, :], meta_valid, pm_scr, 0))
            softmax_rows(hh, r0, parts)
            if n_cols < tk:
                p_scr[slot, r0:r0 + rows, n_cols:tk] = jnp.zeros((rows, tk - n_cols), bf16)
        accumulate(hh, jnp.dot(p_scr[slot], v_ref[pl.ds(start, tk), hh * LANES:(hh + 1) * LANES],
                               preferred_element_type=f32)
                   + jnp.dot(pm_scr[slot], vm_ref[:, hh * LANES:(hh + 1) * LANES],
                             preferred_element_type=f32))

    for pair in range(pairs):
        acc_a, acc_b = acc_scr[HEADS_PER_BLOCK * pair], acc_scr[HEADS_PER_BLOCK * pair + 1]
        out = jnp.where(low_half, acc_a / acc_a[:, HEAD_DIM:HEAD_DIM + 1], acc_b / acc_b[:, 0:1])
        o_ref[:, pair * LANES:(pair + 1) * LANES] = out.astype(bf16)


def _attention(qx, kx, v_aug, kx_meta, v_meta, batch, seq, tq, tk, rows, pairs):
    nq = seq // tq
    n_heads = pairs * HEADS_PER_BLOCK
    n_blk = ATTN_HEADS // n_heads
    pair_lanes = n_heads * LANES
    q_spec = pl.BlockSpec((tq, HEADS_PER_BLOCK * pair_lanes), lambda b, p, i: (b * nq + i, p))
    o_spec = pl.BlockSpec((tq, pairs * LANES), lambda b, p, i: (b * nq + i, p))
    kv_spec = pl.BlockSpec((seq, pair_lanes), lambda b, p, i: (b, p),
                           pipeline_mode=pl.Buffered(KV_BUFFERS))
    meta_spec = pl.BlockSpec((LANES, pair_lanes), lambda b, p, i: (0, p))
    per_head = lambda width, dt: pltpu.VMEM((n_heads, tq, width), dt)
    ring = lambda width, dt: pltpu.VMEM((2, tq, width), dt)
    return pl.pallas_call(
        functools.partial(_attn_body, tq=tq, tk=tk, rows=rows, pairs=pairs),
        grid=(batch, n_blk, nq),
        in_specs=[q_spec, kv_spec, kv_spec, meta_spec, meta_spec],
        out_specs=o_spec,
        out_shape=jax.ShapeDtypeStruct((batch * seq, ATTN_WIDTH), bf16),
        scratch_shapes=[per_head(LANES, f32), per_head(LANES, f32), per_head(LANES, f32),
                        ring(tk, f32), ring(LANES, f32), ring(tk, bf16), ring(LANES, bf16)],
        compiler_params=pltpu.CompilerParams(
            dimension_semantics=("arbitrary", "arbitrary", "arbitrary"),
            vmem_limit_bytes=VMEM_LIMIT_BYTES),
        name="forgetting_attention",
    )(qx, kx, v_aug, kx_meta, v_meta)


def _mix_out_ffn_body(h_ref, attn_ref, sga_ref, gcv_ref, wa_ref, wo_ref, g_mix_ref,
                      g_pre_ref, g_post_ref, w_in_ref, w_out_ref, o_ref, act_ref):
    y_attn = jnp.dot(attn_ref[...], wa_ref[...], preferred_element_type=f32)
    gated = sga_ref[...].astype(f32) * y_attn + gcv_ref[...].astype(f32)
    mixed = jnp.dot(gated.astype(bf16), wo_ref[...], preferred_element_type=f32)
    h2 = h_ref[...] + _rms_norm(mixed, g_mix_ref[...])
    o_ref[...] = _swiglu_residual(h2, g_pre_ref[...], g_post_ref[...],
                                  w_in_ref, w_out_ref, act_ref)


def _mix_out_ffn(h, attn, sga, gcv, w_attn_branch, w_out, g_mix, g_pre, g_post, w_in, w_ffn_out, tm):
    rows = h.shape[0]
    row_blk = lambda width: pl.BlockSpec((tm, width), lambda i: (i, 0))
    return pl.pallas_call(
        _mix_out_ffn_body,
        grid=(rows // tm,),
        in_specs=[row_blk(D_MODEL), row_blk(ATTN_WIDTH), row_blk(D_MODEL), row_blk(D_MODEL),
                  _resident((ATTN_WIDTH, D_MODEL)), _resident((D_MODEL, D_MODEL)),
                  _resident((1, D_MODEL)), _resident((1, D_MODEL)), _resident((1, D_MODEL)),
                  _resident((D_MODEL, 2 * D_FF)), _resident((D_FF, D_MODEL))],
        out_specs=row_blk(D_MODEL),
        out_shape=jax.ShapeDtypeStruct((rows, D_MODEL), f32),
        scratch_shapes=[pltpu.VMEM((tm, D_FF), bf16)],
        compiler_params=pltpu.CompilerParams(
            dimension_semantics=("arbitrary",), vmem_limit_bytes=VMEM_LIMIT_BYTES),
        name="mixer_output_ffn",
    )(h, attn, sga, gcv, w_attn_branch, w_out, g_mix, g_pre, g_post, w_in, w_ffn_out)


def kernel(x, meta_tokens, w_in, b_forget, conv_w, w_attn_branch, w_conv_branch, w_out,
           g_ffn1_pre, g_ffn1_post, w_ffn1_in, w_ffn1_out, g_mix_pre, g_mix_post,
           g_ffn2_pre, g_ffn2_post, w_ffn2_in, w_ffn2_out):
    batch, seq, d = x.shape
    assert d == D_MODEL and w_in.shape[0] == 1 and meta_tokens.shape == (N_META, D_MODEL)
    assert seq % ROW_TILE == 0 and seq % FFN_TILE == 0 and seq % ATTN_TQ == 0
    assert ATTN_TQ == ATTN_TK and ATTN_TQ % ATTN_ROWS == 0 and ATTN_ROWS % LANES == 0

    gain = lambda g: g[0].reshape(1, D_MODEL).astype(f32)
    wt = w_in[0].T
    w_qkv = wt[:N_QKV].astype(bf16)
    w_mix = wt[N_QKV + ATTN_HEADS:].astype(bf16)
    w_f = jnp.pad(wt[N_QKV:N_QKV + ATTN_HEADS].astype(bf16), ((0, LANES - ATTN_HEADS), (0, 0)))
    b_f = jnp.pad(b_forget[0].astype(f32), (0, LANES - ATTN_HEADS)).reshape(1, LANES)
    cw = conv_w[0].astype(f32)
    wc = w_conv_branch[0].astype(bf16)
    wa = w_attn_branch[0].astype(bf16)
    wo = w_out[0].astype(bf16)
    w1_in, w1_out = w_ffn1_in[0].astype(bf16), w_ffn1_out[0].astype(bf16)
    w2_in, w2_out = w_ffn2_in[0].astype(bf16), w_ffn2_out[0].astype(bf16)

    hm = _ffn(meta_tokens.astype(f32), gain(g_ffn1_pre), gain(g_ffn1_post), w1_in, w1_out, N_META)
    zeros_p = jnp.zeros((SUBLANES, CONV_WIDTH), f32)
    zeros_f = jnp.zeros((1, LANES), f32)
    _, kxm, vm, fm, _, _, pm_tail = _mix_in(hm, gain(g_mix_pre), w_qkv, w_mix, w_f, b_f, cw, wc,
                                           zeros_p, zeros_f, 1, N_META)
    pad_rows = lambda a: jnp.pad(a, ((0, LANES - N_META), (0, 0)))
    f_init = jnp.pad(fm[N_META - 1:N_META, :], ((0, 0), (0, LANES - ATTN_HEADS)))

    rows = batch * seq
    h1 = _ffn(x.reshape(rows, D_MODEL), gain(g_ffn1_pre), gain(g_ffn1_post), w1_in, w1_out, FFN_TILE)
    qx, kx, v, _, sga, gcv, _ = _mix_in(h1, gain(g_mix_pre), w_qkv, w_mix, w_f, b_f, cw, wc,
                                        pm_tail, f_init, batch, ROW_TILE)
    attn = _attention(qx, kx, v, pad_rows(kxm), pad_rows(vm),
                      batch, seq, ATTN_TQ, ATTN_TK, ATTN_ROWS, ATTN_PAIRS)
    h3 = _mix_out_ffn(h1, attn, sga, gcv, wa, wo, gain(g_mix_post), gain(g_ffn2_pre),
                      gain(g_ffn2_post), w2_in, w2_out, ROW_TILE)
    return h3.reshape(batch, seq, D_MODEL)
```

```python
import functools

import jax
import jax.numpy as jnp
import numpy as np
from jax import lax
from jax.experimental import pallas as pl
from jax.experimental.pallas import tpu as pltpu

D_MODEL = 1024
D_FF = 2816
N_META = 16
ATTN_HEADS = 8
HEAD_DIM = 64
ATTN_WIDTH = ATTN_HEADS * HEAD_DIM
CONV_WIDTH = 512
CONV_K = 3
NORM_EPS = 1e-6

LANES = 128
SUBLANES = 8
HEADS_PER_BLOCK = LANES // HEAD_DIM
VMEM_LIMIT_BYTES = 56 * 1024 * 1024
MASK_VALUE = -1e30
LOG2E = 1.4426950408889634
N_SPLIT = 3

ROW_TILE = 512
FFN_TILE = 1024
FF_CHUNK = 256
ATTN_TQ = 512
ATTN_TK = 512
ATTN_ROWS = 128
ATTN_PAIRS = 4
KV_BUFFERS = 1

C_Q, C_K, C_V = 0, 512, 1024
C_CB, C_CC, C_CIN, C_GA, C_GC = 0, 512, 1024, 1536, 2560
N_QKV = 3 * ATTN_WIDTH
N_MIX = 3 * CONV_WIDTH + 2 * D_MODEL

bf16 = jnp.bfloat16
f32 = jnp.float32


def _rms_norm(x, g):
    ms = jnp.mean(x * x, axis=-1, keepdims=True)
    return x * lax.rsqrt(ms + NORM_EPS) * g


def _resident(shape):
    nd = len(shape)
    return pl.BlockSpec(shape, lambda *_: (0,) * nd, pipeline_mode=pl.Buffered(1))


def _swiglu_residual(h, g_pre, g_post, w_in_ref, w_out_ref, act_ref):
    u = _rms_norm(h, g_pre).astype(bf16)
    for c in range(D_FF // FF_CHUNK):
        lo = c * FF_CHUNK
        a = jnp.dot(u, w_in_ref[:, lo:lo + FF_CHUNK], preferred_element_type=f32)
        b = jnp.dot(u, w_in_ref[:, D_FF + lo:D_FF + lo + FF_CHUNK], preferred_element_type=f32)
        act_ref[:, lo:lo + FF_CHUNK] = (a * jax.nn.sigmoid(a) * b).astype(bf16)
    y = jnp.dot(act_ref[...], w_out_ref[...], preferred_element_type=f32)
    return h + 0.5 * _rms_norm(y, g_post)


def _ffn_body(h_ref, g_pre_ref, g_post_ref, w_in_ref, w_out_ref, *rest):
    n_cast = (len(rest) - 2) // 2
    o_ref, act_ref = rest[n_cast], rest[-1]
    for src, dst in zip(rest[:n_cast], rest[n_cast + 1:-1]):
        dst[...] = src[...].astype(bf16)
    o_ref[...] = _swiglu_residual(h_ref[...], g_pre_ref[...], g_post_ref[...],
                                  w_in_ref, w_out_ref, act_ref)


def _ffn(h, g_pre, g_post, w_in, w_out, tm, cast_along=()):
    rows = h.shape[0]
    steps = rows // tm
    row_blk = pl.BlockSpec((tm, D_MODEL), lambda i: (i, 0))
    cast_specs = [pl.BlockSpec((w.shape[0] // steps, w.shape[1]), lambda i: (i, 0))
                  for w in cast_along]
    outs = pl.pallas_call(
        _ffn_body,
        grid=(steps,),
        in_specs=[
            row_blk,
            _resident((1, D_MODEL)),
            _resident((1, D_MODEL)),
            _resident((D_MODEL, 2 * D_FF)),
            _resident((D_FF, D_MODEL)),
        ] + cast_specs,
        out_specs=[row_blk] + cast_specs,
        out_shape=[jax.ShapeDtypeStruct((rows, D_MODEL), f32)]
        + [jax.ShapeDtypeStruct(w.shape, bf16) for w in cast_along],
        scratch_shapes=[pltpu.VMEM((tm, D_FF), bf16)],
        compiler_params=pltpu.CompilerParams(
            dimension_semantics=("arbitrary",), vmem_limit_bytes=VMEM_LIMIT_BYTES),
        name="ffn_half_step",
    )(h, g_pre, g_post, w_in, w_out, *cast_along)
    return outs[0] if not cast_along else outs


def _log_sigmoid(x):
    return jnp.minimum(x, 0.0) - jnp.log1p(jnp.exp(-jnp.abs(x)))


def _split3(x):
    hi = x.astype(bf16)
    r = x - hi.astype(f32)
    mid = r.astype(bf16)
    lo = (r - mid.astype(f32)).astype(bf16)
    return hi, mid, lo


def _pack3(x, head_lanes):
    hi, mid, lo = _split3(jnp.where(head_lanes, x, 0.0))
    packed = (hi.astype(f32) + pltpu.roll(mid.astype(f32), ATTN_HEADS, 1)
              + pltpu.roll(lo.astype(f32), 2 * ATTN_HEADS, 1))
    return packed.astype(bf16)


def _mix_in_body(h_ref, g_ref, w_qkv_ref, w_mix_ref, w_f_ref, bf_ref, cw_ref, wc_ref,
                 p_init_ref, f_init_ref, ksel_ref, qsel_ref,
                 qx_ref, kx_ref, v_ref, fsum_ref, sga_ref, gcv_ref, p_tail_ref,
                 p_scr, f_carry, *, tm):
    @pl.when(pl.program_id(1) == 0)
    def _():
        p_scr[0:SUBLANES, :] = p_init_ref[...]
        f_carry[...] = f_init_ref[...]

    u = _rms_norm(h_ref[...], g_ref[...]).astype(bf16)

    def proj(wt_ref, lo, width):
        return lax.dot_general(u, wt_ref[lo:lo + width, :], (((1,), (1,)), ((), ())),
                               preferred_element_type=f32)

    rb = min(tm, LANES)
    n_blk = tm // rb
    row_blk = [slice(blk * rb, (blk + 1) * rb) for blk in range(n_blk)]
    n_pairs = ATTN_HEADS // HEADS_PER_BLOCK
    head_lanes = lax.broadcasted_iota(jnp.int32, (rb, LANES), 1) < ATTN_HEADS
    log_f = _log_sigmoid(proj(w_f_ref, 0, LANES) + bf_ref[...]) * LOG2E

    q = (proj(w_qkv_ref, C_Q, ATTN_WIDTH) * (LOG2E * HEAD_DIM ** -0.5)).astype(bf16)
    lane = lax.broadcasted_iota(jnp.int32, (tm, LANES), 1)
    for pair in range(n_pairs):
        q2 = q[:, pair * LANES:(pair + 1) * LANES]
        zeros = jnp.zeros_like(q2)
        c0 = 2 * pair * HEADS_PER_BLOCK * LANES
        qx_ref[:, c0:c0 + LANES] = jnp.where(lane < HEAD_DIM, q2, zeros)
        qx_ref[:, c0 + 2 * LANES:c0 + 3 * LANES] = jnp.where(lane < HEAD_DIM, zeros, q2)

    tri = (lax.broadcasted_iota(jnp.int32, (rb, rb), 0)
           >= lax.broadcasted_iota(jnp.int32, (rb, rb), 1)).astype(bf16)
    c3 = [jnp.dot(tri, _pack3(log_f[rs, :], head_lanes), preferred_element_type=f32)
          for rs in row_blk]

    k = proj(w_qkv_ref, C_K, ATTN_WIDTH)
    for pair in range(n_pairs):
        c0 = pair * HEADS_PER_BLOCK * LANES
        kx_ref[:, c0:c0 + LANES] = k[:, pair * LANES:(pair + 1) * LANES].astype(bf16)

    carry = f_carry[...]
    pieces = []
    for rs, c in zip(row_blk, c3):
        cs = c + pltpu.roll(c, LANES - ATTN_HEADS, 1) + pltpu.roll(c, LANES - 2 * ATTN_HEADS, 1)
        cs = jnp.where(head_lanes, cs, 0.0) + carry
        carry = cs[rb - 1:rb, :]
        fsum_ref[rs, :] = cs[:, :ATTN_HEADS]
        pieces.append(_pack3(cs, head_lanes))
    f_carry[...] = carry

    v = proj(w_qkv_ref, C_V, ATTN_WIDTH)
    ones_a = jnp.where(lane == HEAD_DIM, 1.0, 0.0)
    ones_b = jnp.where(lane == 0, 1.0, 0.0)
    for pair in range(n_pairs):
        v2 = v[:, pair * LANES:(pair + 1) * LANES]
        c0 = pair * HEADS_PER_BLOCK * LANES
        v_ref[:, c0:c0 + LANES] = jnp.where(lane < HEAD_DIM, v2, ones_a).astype(bf16)
        v_ref[:, c0 + LANES:c0 + 2 * LANES] = jnp.where(lane < HEAD_DIM, ones_b, v2).astype(bf16)

    k_lane = lax.broadcasted_iota(jnp.int32, (rb, n_pairs * LANES), 1)
    k_ones = jnp.bitwise_and(k_lane, LANES - 1) < N_SPLIT
    q_lane = lax.broadcasted_iota(jnp.int32, (rb, ATTN_HEADS * LANES), 1)
    q_head = jnp.right_shift(q_lane, LANES.bit_length() - 1)
    q_ones_lo = N_SPLIT * (1 + jnp.bitwise_and(q_head, HEADS_PER_BLOCK - 1))
    q_in_blk = jnp.bitwise_and(q_lane, LANES - 1)
    q_ones = (q_in_blk >= q_ones_lo) & (q_in_blk < q_ones_lo + N_SPLIT)
    for rs, pc in zip(row_blk, pieces):
        k_bias = jnp.dot(pc, ksel_ref[...], preferred_element_type=f32)
        k_bias = jnp.where(k_ones, 1.0, k_bias).astype(bf16)
        for pair in range(n_pairs):
            c0 = pair * HEADS_PER_BLOCK * LANES + LANES
            kx_ref[rs, c0:c0 + LANES] = k_bias[:, pair * LANES:(pair + 1) * LANES]
        q_bias = jnp.dot(pc, qsel_ref[...], preferred_element_type=f32)
        q_bias = jnp.where(q_ones, 1.0, q_bias).astype(bf16)
        for head in range(ATTN_HEADS):
            c0 = head * HEADS_PER_BLOCK * LANES + LANES
            qx_ref[rs, c0:c0 + LANES] = q_bias[:, head * LANES:(head + 1) * LANES]

    p = proj(w_mix_ref, C_CC, CONV_WIDTH) * proj(w_mix_ref, C_CIN, CONV_WIDTH)
    p_scr[SUBLANES:SUBLANES + tm, :] = p
    conv = (p_scr[SUBLANES - 2:SUBLANES - 2 + tm, :] * cw_ref[0:1, :]
            + p_scr[SUBLANES - 1:SUBLANES - 1 + tm, :] * cw_ref[1:2, :]
            + p * cw_ref[2:3, :])
    tail = p_scr[tm:tm + SUBLANES, :]
    p_scr[0:SUBLANES, :] = tail
    p_tail_ref[...] = tail
    conv_in = (proj(w_mix_ref, C_CB, CONV_WIDTH) * conv).astype(bf16)
    y_conv = jnp.dot(conv_in, wc_ref[...], preferred_element_type=f32)
    sga_ref[...] = jax.nn.sigmoid(proj(w_mix_ref, C_GA, D_MODEL)).astype(bf16)
    gcv_ref[...] = (jax.nn.sigmoid(proj(w_mix_ref, C_GC, D_MODEL)) * y_conv).astype(bf16)


def _bias_selectors():
    ksel = np.zeros((LANES, ATTN_HEADS // HEADS_PER_BLOCK * LANES), np.float32)
    qsel = np.zeros((LANES, ATTN_HEADS * LANES), np.float32)
    for h in range(ATTN_HEADS):
        for t in range(N_SPLIT):
            col = (h // HEADS_PER_BLOCK) * LANES + N_SPLIT * (1 + h % HEADS_PER_BLOCK) + t
            ksel[t * ATTN_HEADS + h, col] = -1.0
            qsel[t * ATTN_HEADS + h, h * LANES + t] = 1.0
    return jnp.asarray(ksel, bf16), jnp.asarray(qsel, bf16)


def _mix_in(h, g, w_qkv, w_mix, w_f, b_f, conv_w, w_conv_branch, p_init, f_init, batch, tm):
    rows = h.shape[0]
    nt = rows // (batch * tm)
    row_blk = lambda width: pl.BlockSpec((tm, width), lambda b, t: (b * nt + t, 0))
    out_rows = lambda width, dt: jax.ShapeDtypeStruct((rows, width), dt)
    return pl.pallas_call(
        functools.partial(_mix_in_body, tm=tm),
        grid=(batch, nt),
        in_specs=[
            row_blk(D_MODEL),
            _resident((1, D_MODEL)),
            _resident((N_QKV, D_MODEL)),
            _resident((N_MIX, D_MODEL)),
            _resident((LANES, D_MODEL)),
            _resident((1, LANES)),
            _resident((CONV_K, CONV_WIDTH)),
            _resident((CONV_WIDTH, D_MODEL)),
            _resident((SUBLANES, CONV_WIDTH)),
            _resident((1, LANES)),
            _resident((LANES, ATTN_HEADS // HEADS_PER_BLOCK * LANES)),
            _resident((LANES, ATTN_HEADS * LANES)),
        ],
        out_specs=[
            row_blk(2 * ATTN_HEADS * LANES), row_blk(ATTN_HEADS * LANES),
            row_blk(ATTN_HEADS * LANES),
            row_blk(ATTN_HEADS), row_blk(D_MODEL), row_blk(D_MODEL),
            pl.BlockSpec((SUBLANES, CONV_WIDTH), lambda b, t: (b * nt + t, 0)),
        ],
        out_shape=[
            out_rows(2 * ATTN_HEADS * LANES, bf16), out_rows(ATTN_HEADS * LANES, bf16),
            out_rows(ATTN_HEADS * LANES, bf16),
            out_rows(ATTN_HEADS, f32), out_rows(D_MODEL, bf16), out_rows(D_MODEL, bf16),
            jax.ShapeDtypeStruct((batch * nt * SUBLANES, CONV_WIDTH), f32),
        ],
        scratch_shapes=[pltpu.VMEM((tm + SUBLANES, CONV_WIDTH), f32),
                        pltpu.VMEM((1, LANES), f32)],
        compiler_params=pltpu.CompilerParams(
            dimension_semantics=("arbitrary", "arbitrary"), vmem_limit_bytes=VMEM_LIMIT_BYTES),
        name="mixer_input",
    )(h, g, w_qkv, w_mix, w_f, b_f, conv_w, w_conv_branch, p_init, f_init, *_bias_selectors())


def _attn_body(q_ref, k_ref, v_ref, km_ref, vm_ref, o_ref,
               m_scr, alpha_scr, acc_scr, s_scr, sm_scr, p_scr, pm_scr, *, tq, tk, rows, pairs):
    i = pl.program_id(2)
    low_half = lax.broadcasted_iota(jnp.int32, (tq, LANES), 1) < HEAD_DIM
    head_lanes = HEADS_PER_BLOCK * LANES
    m_scr[...] = jnp.full(m_scr.shape, MASK_VALUE, f32)
    acc_scr[...] = jnp.zeros(acc_scr.shape, f32)
    contract_last = (((1,), (1,)), ((), ()))
    lane = lax.broadcasted_iota(jnp.int32, (rows, LANES), 1)
    sub = lax.broadcasted_iota(jnp.int32, (rows, LANES), 0)
    n_heads = pairs * HEADS_PER_BLOCK
    assert n_heads % 2 == 0

    def scores(hh, keys_ref, key_rows):
        pair = hh // HEADS_PER_BLOCK
        q_head = q_ref[:, hh * head_lanes:(hh + 1) * head_lanes]
        k_pair = keys_ref[key_rows, pair * head_lanes:(pair + 1) * head_lanes]
        return lax.dot_general(q_head, k_pair, contract_last, preferred_element_type=f32)

    def softmax_rows(hh, r0, parts):
        rs = slice(r0, r0 + rows)
        blocks = [blk if valid is None else jnp.where(valid, blk, MASK_VALUE)
                  for blk, valid, _, _ in parts]
        m_old = m_scr[hh, rs, :]
        m_blk = functools.reduce(jnp.maximum, blocks)
        m_new = jnp.maximum(
            m_old, jnp.broadcast_to(jnp.max(m_blk, axis=-1, keepdims=True), (rows, LANES)))
        for blk, (_, _, dst, c0) in zip(blocks, parts):
            dst[hh % 2, rs, c0:c0 + LANES] = jnp.exp2(blk - m_new).astype(bf16)
        alpha_scr[hh, rs, :] = jnp.exp2(m_old - m_new)
        m_scr[hh, rs, :] = m_new

    def accumulate(hh, pv):
        acc_scr[hh] = alpha_scr[hh] * acc_scr[hh] + pv

    s_scr[0] = scores(0, k_ref, pl.ds(0, tk))

    def full_step(j, carry):
        start = pl.multiple_of(j * tk, tk)
        for hh in range(n_heads):
            slot = hh % 2
            if hh + 1 < n_heads:
                s_scr[1 - slot] = scores(hh + 1, k_ref, pl.ds(start, tk))
            else:
                s_scr[1 - slot] = scores(0, k_ref, pl.ds(pl.multiple_of(start + tk, tk), tk))
            for r0 in range(0, tq, rows):
                softmax_rows(hh, r0, [
                    (s_scr[slot, r0:r0 + rows, c * LANES:(c + 1) * LANES], None, p_scr, c * LANES)
                    for c in range(tk // LANES)])
            accumulate(hh, jnp.dot(p_scr[slot], v_ref[pl.ds(start, tk), hh * LANES:(hh + 1) * LANES],
                                   preferred_element_type=f32))
        return carry

    lax.fori_loop(0, lax.shift_right_logical(i, 1),
                  lambda jj, c: full_step(2 * jj + 1, full_step(2 * jj, c)), 0)

    @pl.when(jnp.bitwise_and(i, 1) == 1)
    def _():
        full_step(i - 1, 0)

    start = pl.multiple_of(i * tk, tk)
    meta_valid = lane < N_META
    sm_scr[0] = scores(0, km_ref, slice(None))
    for hh in range(n_heads):
        slot = hh % 2
        if hh + 1 < n_heads:
            s_scr[1 - slot] = scores(hh + 1, k_ref, pl.ds(start, tk))
            sm_scr[1 - slot] = scores(hh + 1, km_ref, slice(None))
        for r0 in range(0, tq, rows):
            n_cols = r0 + rows
            parts = [(s_scr[slot, r0:r0 + rows, c * LANES:(c + 1) * LANES],
                      None if (c + 1) * LANES - 1 <= r0 else (lane + c * LANES <= sub + r0),
                      p_scr, c * LANES) for c in range(n_cols // LANES)]
            parts.append((sm_scr[slot, r0:r0 + rows, :], meta_valid, pm_scr, 0))
            softmax_rows(hh, r0, parts)
            if n_cols < tk:
                p_scr[slot, r0:r0 + rows, n_cols:tk] = jnp.zeros((rows, tk - n_cols), bf16)
        accumulate(hh, jnp.dot(p_scr[slot], v_ref[pl.ds(start, tk), hh * LANES:(hh + 1) * LANES],
                               preferred_element_type=f32)
                   + jnp.dot(pm_scr[slot], vm_ref[:, hh * LANES:(hh + 1) * LANES],
                             preferred_element_type=f32))

    for pair in range(pairs):
        acc_a, acc_b = acc_scr[HEADS_PER_BLOCK * pair], acc_scr[HEADS_PER_BLOCK * pair + 1]
        out = jnp.where(low_half, acc_a / acc_a[:, HEAD_DIM:HEAD_DIM + 1], acc_b / acc_b[:, 0:1])
        o_ref[:, pair * LANES:(pair + 1) * LANES] = out.astype(bf16)


def _attention(qx, kx, v_aug, kx_meta, v_meta, batch, seq, tq, tk, rows, pairs):
    nq = seq // tq
    n_heads = pairs * HEADS_PER_BLOCK
    n_blk = ATTN_HEADS // n_heads
    pair_lanes = n_heads * LANES
    q_spec = pl.BlockSpec((tq, HEADS_PER_BLOCK * pair_lanes), lambda b, p, i: (b * nq + i, p))
    o_spec = pl.BlockSpec((tq, pairs * LANES), lambda b, p, i: (b * nq + i, p))
    kv_spec = pl.BlockSpec((seq, pair_lanes), lambda b, p, i: (b, p),
                           pipeline_mode=pl.Buffered(KV_BUFFERS))
    meta_spec = pl.BlockSpec((LANES, pair_lanes), lambda b, p, i: (0, p))
    per_head = lambda width, dt: pltpu.VMEM((n_heads, tq, width), dt)
    ring = lambda width, dt: pltpu.VMEM((2, tq, width), dt)
    return pl.pallas_call(
        functools.partial(_attn_body, tq=tq, tk=tk, rows=rows, pairs=pairs),
        grid=(batch, n_blk, nq),
        in_specs=[q_spec, kv_spec, kv_spec, meta_spec, meta_spec],
        out_specs=o_spec,
        out_shape=jax.ShapeDtypeStruct((batch * seq, ATTN_WIDTH), bf16),
        scratch_shapes=[per_head(LANES, f32), per_head(LANES, f32), per_head(LANES, f32),
                        ring(tk, f32), ring(LANES, f32), ring(tk, bf16), ring(LANES, bf16)],
        compiler_params=pltpu.CompilerParams(
            dimension_semantics=("arbitrary", "arbitrary", "arbitrary"),
            vmem_limit_bytes=VMEM_LIMIT_BYTES),
        name="forgetting_attention",
    )(qx, kx, v_aug, kx_meta, v_meta)


def _mix_out_ffn_body(h_ref, attn_ref, sga_ref, gcv_ref, wa_ref, wo_ref, g_mix_ref,
                      g_pre_ref, g_post_ref, w_in_ref, w_out_ref, o_ref, act_ref):
    y_attn = jnp.dot(attn_ref[...], wa_ref[...], preferred_element_type=f32)
    gated = sga_ref[...].astype(f32) * y_attn + gcv_ref[...].astype(f32)
    mixed = jnp.dot(gated.astype(bf16), wo_ref[...], preferred_element_type=f32)
    h2 = h_ref[...] + _rms_norm(mixed, g_mix_ref[...])
    o_ref[...] = _swiglu_residual(h2, g_pre_ref[...], g_post_ref[...],
                                  w_in_ref, w_out_ref, act_ref)


def _mix_out_ffn(h, attn, sga, gcv, w_attn_branch, w_out, g_mix, g_pre, g_post, w_in, w_ffn_out, tm):
    rows = h.shape[0]
    row_blk = lambda width: pl.BlockSpec((tm, width), lambda i: (i, 0))
    return pl.pallas_call(
        _mix_out_ffn_body,
        grid=(rows // tm,),
        in_specs=[row_blk(D_MODEL), row_blk(ATTN_WIDTH), row_blk(D_MODEL), row_blk(D_MODEL),
                  _resident((ATTN_WIDTH, D_MODEL)), _resident((D_MODEL, D_MODEL)),
                  _resident((1, D_MODEL)), _resident((1, D_MODEL)), _resident((1, D_MODEL)),
                  _resident((D_MODEL, 2 * D_FF)), _resident((D_FF, D_MODEL))],
        out_specs=row_blk(D_MODEL),
        out_shape=jax.ShapeDtypeStruct((rows, D_MODEL), f32),
        scratch_shapes=[pltpu.VMEM((tm, D_FF), bf16)],
        compiler_params=pltpu.CompilerParams(
            dimension_semantics=("arbitrary",), vmem_limit_bytes=VMEM_LIMIT_BYTES),
        name="mixer_output_ffn",
    )(h, attn, sga, gcv, w_attn_branch, w_out, g_mix, g_pre, g_post, w_in, w_ffn_out)


def kernel(x, meta_tokens, w_in, b_forget, conv_w, w_attn_branch, w_conv_branch, w_out,
           g_ffn1_pre, g_ffn1_post, w_ffn1_in, w_ffn1_out, g_mix_pre, g_mix_post,
           g_ffn2_pre, g_ffn2_post, w_ffn2_in, w_ffn2_out):
    batch, seq, d = x.shape
    assert d == D_MODEL and w_in.shape[0] == 1 and meta_tokens.shape == (N_META, D_MODEL)
    assert seq % ROW_TILE == 0 and seq % FFN_TILE == 0 and seq % ATTN_TQ == 0
    assert ATTN_TQ == ATTN_TK and ATTN_TQ % ATTN_ROWS == 0 and ATTN_ROWS % LANES == 0

    gain = lambda g: g[0].reshape(1, D_MODEL).astype(f32)
    wt = w_in[0].T
    w_qkv = wt[:N_QKV].astype(bf16)
    w_mix = wt[N_QKV + ATTN_HEADS:].astype(bf16)
    w_f = jnp.pad(wt[N_QKV:N_QKV + ATTN_HEADS].astype(bf16), ((0, LANES - ATTN_HEADS), (0, 0)))
    b_f = jnp.pad(b_forget[0].astype(f32), (0, LANES - ATTN_HEADS)).reshape(1, LANES)
    cw = conv_w[0].astype(f32)
    wc = w_conv_branch[0].astype(bf16)
    wa = w_attn_branch[0].astype(bf16)
    wo = w_out[0].astype(bf16)
    w1_in, w1_out = w_ffn1_in[0].astype(bf16), w_ffn1_out[0].astype(bf16)

    hm = _ffn(meta_tokens.astype(f32), gain(g_ffn1_pre), gain(g_ffn1_post), w1_in, w1_out, N_META)
    zeros_p = jnp.zeros((SUBLANES, CONV_WIDTH), f32)
    zeros_f = jnp.zeros((1, LANES), f32)
    _, kxm, vm, fm, _, _, pm_tail = _mix_in(hm, gain(g_mix_pre), w_qkv, w_mix, w_f, b_f, cw, wc,
                                           zeros_p, zeros_f, 1, N_META)
    pad_rows = lambda a: jnp.pad(a, ((0, LANES - N_META), (0, 0)))
    f_init = jnp.pad(fm[N_META - 1:N_META, :], ((0, 0), (0, LANES - ATTN_HEADS)))

    rows = batch * seq
    h1, w2_in, w2_out = _ffn(x.reshape(rows, D_MODEL), gain(g_ffn1_pre), gain(g_ffn1_post),
                             w1_in, w1_out, FFN_TILE, cast_along=(w_ffn2_in[0], w_ffn2_out[0]))
    qx, kx, v, _, sga, gcv, _ = _mix_in(h1, gain(g_mix_pre), w_qkv, w_mix, w_f, b_f, cw, wc,
                                        pm_tail, f_init, batch, ROW_TILE)
    attn = _attention(qx, kx, v, pad_rows(kxm), pad_rows(vm),
                      batch, seq, ATTN_TQ, ATTN_TK, ATTN_ROWS, ATTN_PAIRS)
    h3 = _mix_out_ffn(h1, attn, sga, gcv, wa, wo, gain(g_mix_post), gain(g_ffn2_pre),
                      gain(g_ffn2_post), w2_in, w2_out, ROW_TILE)
    return h3.reshape(batch, seq, D_MODEL)
```

```python
import functools

import jax
import jax.numpy as jnp
import numpy as np
from jax import lax
from jax.experimental import pallas as pl
from jax.experimental.pallas import tpu as pltpu

D_MODEL = 1024
D_FF = 2816
N_META = 16
ATTN_HEADS = 8
HEAD_DIM = 64
ATTN_WIDTH = ATTN_HEADS * HEAD_DIM
CONV_WIDTH = 512
CONV_K = 3
NORM_EPS = 1e-6

LANES = 128
SUBLANES = 8
HEADS_PER_BLOCK = LANES // HEAD_DIM
VMEM_LIMIT_BYTES = 56 * 1024 * 1024
MASK_VALUE = -1e30
LOG2E = 1.4426950408889634
N_SPLIT = 3

ROW_TILE = 512
FFN_TILE = 1024
FF_CHUNK = 256
ATTN_TQ = 512
ATTN_TK = 512
ATTN_ROWS = 128
ATTN_PAIRS = 4
KV_BUFFERS = 1

C_Q, C_K, C_V = 0, 512, 1024
C_CB, C_CC, C_CIN, C_GA, C_GC = 0, 512, 1024, 1536, 2560
N_QKV = 3 * ATTN_WIDTH
N_MIX = 3 * CONV_WIDTH + 2 * D_MODEL

bf16 = jnp.bfloat16
f32 = jnp.float32


def _rms_norm(x, g):
    ms = jnp.mean(x * x, axis=-1, keepdims=True)
    return x * lax.rsqrt(ms + NORM_EPS) * g


def _resident(shape):
    nd = len(shape)
    return pl.BlockSpec(shape, lambda *_: (0,) * nd, pipeline_mode=pl.Buffered(1))


def _swiglu_residual(h, g_pre, g_post, w_in_ref, w_out_ref, act_ref):
    u = _rms_norm(h, g_pre).astype(bf16)
    for c in range(D_FF // FF_CHUNK):
        lo = c * FF_CHUNK
        a = jnp.dot(u, w_in_ref[:, lo:lo + FF_CHUNK], preferred_element_type=f32)
        b = jnp.dot(u, w_in_ref[:, D_FF + lo:D_FF + lo + FF_CHUNK], preferred_element_type=f32)
        act_ref[:, lo:lo + FF_CHUNK] = (a * jax.nn.sigmoid(a) * b).astype(bf16)
    y = jnp.dot(act_ref[...], w_out_ref[...], preferred_element_type=f32)
    return h + 0.5 * _rms_norm(y, g_post)


def _ffn_body(h_ref, meta_ref, g_pre_ref, g_post_ref, w_in_ref, w_out_ref, *rest):
    n_cast = (len(rest) - 4) // 2
    o_ref, meta_o_ref = rest[n_cast], rest[n_cast + 1]
    act_ref, meta_act_ref = rest[-2], rest[-1]
    for src, dst in zip(rest[:n_cast], rest[n_cast + 2:-2]):
        dst[...] = src[...].astype(bf16)

    @pl.when(pl.program_id(0) == 0)
    def _():
        meta_o_ref[...] = _swiglu_residual(meta_ref[...], g_pre_ref[...], g_post_ref[...],
                                           w_in_ref, w_out_ref, meta_act_ref)

    o_ref[...] = _swiglu_residual(h_ref[...], g_pre_ref[...], g_post_ref[...],
                                  w_in_ref, w_out_ref, act_ref)


def _ffn(h, meta, g_pre, g_post, w_in, w_out, tm, cast_along):
    rows = h.shape[0]
    steps = rows // tm
    row_blk = pl.BlockSpec((tm, D_MODEL), lambda i: (i, 0))
    cast_in, cast_out = [], []
    for w, first, count in cast_along:
        per, width = count // steps, w.shape[1]
        assert count % steps == 0 and per % (2 * SUBLANES) == 0 and first % SUBLANES == 0
        cast_in.append(pl.BlockSpec(
            (pl.Element(per), pl.Element(width)),
            functools.partial(lambda f, p, i: (pl.multiple_of(f + i * p, SUBLANES), 0), first, per)))
        cast_out.append(pl.BlockSpec((per, width), lambda i: (i, 0)))
    meta_blk = pl.BlockSpec(meta.shape, lambda i: (0, 0))
    return pl.pallas_call(
        _ffn_body,
        grid=(steps,),
        in_specs=[
            row_blk,
            _resident(meta.shape),
            _resident((1, D_MODEL)),
            _resident((1, D_MODEL)),
            _resident((D_MODEL, 2 * D_FF)),
            _resident((D_FF, D_MODEL)),
        ] + cast_in,
        out_specs=[row_blk, meta_blk] + cast_out,
        out_shape=[jax.ShapeDtypeStruct((rows, D_MODEL), f32),
                   jax.ShapeDtypeStruct(meta.shape, f32)]
        + [jax.ShapeDtypeStruct((count, w.shape[1]), bf16) for w, _, count in cast_along],
        scratch_shapes=[pltpu.VMEM((tm, D_FF), bf16), pltpu.VMEM((meta.shape[0], D_FF), bf16)],
        compiler_params=pltpu.CompilerParams(
            dimension_semantics=("arbitrary",), vmem_limit_bytes=VMEM_LIMIT_BYTES),
        name="ffn_half_step",
    )(h, meta, g_pre, g_post, w_in, w_out, *[w for w, _, _ in cast_along])


def _log_sigmoid(x):
    return jnp.minimum(x, 0.0) - jnp.log1p(jnp.exp(-jnp.abs(x)))


def _split3(x):
    hi = x.astype(bf16)
    r = x - hi.astype(f32)
    mid = r.astype(bf16)
    lo = (r - mid.astype(f32)).astype(bf16)
    return hi, mid, lo


def _pack3(x, head_lanes):
    hi, mid, lo = _split3(jnp.where(head_lanes, x, 0.0))
    packed = (hi.astype(f32) + pltpu.roll(mid.astype(f32), ATTN_HEADS, 1)
              + pltpu.roll(lo.astype(f32), 2 * ATTN_HEADS, 1))
    return packed.astype(bf16)


def _mix_in_body(h_ref, g_ref, w_qkv_ref, w_mix_ref, w_f_ref, bf_ref, cw_ref, wc_ref,
                 p_init_ref, f_init_ref, ksel_ref, qsel_ref,
                 qx_ref, kx_ref, v_ref, fsum_ref, sga_ref, gcv_ref, p_tail_ref,
                 p_scr, f_carry, *, tm):
    @pl.when(pl.program_id(1) == 0)
    def _():
        p_scr[0:SUBLANES, :] = p_init_ref[...]
        f_carry[...] = f_init_ref[...]

    u = _rms_norm(h_ref[...], g_ref[...]).astype(bf16)

    def proj(wt_ref, lo, width):
        return lax.dot_general(u, wt_ref[lo:lo + width, :], (((1,), (1,)), ((), ())),
                               preferred_element_type=f32)

    rb = min(tm, LANES)
    n_blk = tm // rb
    row_blk = [slice(blk * rb, (blk + 1) * rb) for blk in range(n_blk)]
    n_pairs = ATTN_HEADS // HEADS_PER_BLOCK
    head_lanes = lax.broadcasted_iota(jnp.int32, (rb, LANES), 1) < ATTN_HEADS
    log_f = _log_sigmoid(proj(w_f_ref, 0, LANES) + bf_ref[...]) * LOG2E

    q = (proj(w_qkv_ref, C_Q, ATTN_WIDTH) * (LOG2E * HEAD_DIM ** -0.5)).astype(bf16)
    lane = lax.broadcasted_iota(jnp.int32, (tm, LANES), 1)
    for pair in range(n_pairs):
        q2 = q[:, pair * LANES:(pair + 1) * LANES]
        zeros = jnp.zeros_like(q2)
        c0 = 2 * pair * HEADS_PER_BLOCK * LANES
        qx_ref[:, c0:c0 + LANES] = jnp.where(lane < HEAD_DIM, q2, zeros)
        qx_ref[:, c0 + 2 * LANES:c0 + 3 * LANES] = jnp.where(lane < HEAD_DIM, zeros, q2)

    tri = (lax.broadcasted_iota(jnp.int32, (rb, rb), 0)
           >= lax.broadcasted_iota(jnp.int32, (rb, rb), 1)).astype(bf16)
    c3 = [jnp.dot(tri, _pack3(log_f[rs, :], head_lanes), preferred_element_type=f32)
          for rs in row_blk]

    k = proj(w_qkv_ref, C_K, ATTN_WIDTH)
    for pair in range(n_pairs):
        c0 = pair * HEADS_PER_BLOCK * LANES
        kx_ref[:, c0:c0 + LANES] = k[:, pair * LANES:(pair + 1) * LANES].astype(bf16)

    carry = f_carry[...]
    pieces = []
    for rs, c in zip(row_blk, c3):
        cs = c + pltpu.roll(c, LANES - ATTN_HEADS, 1) + pltpu.roll(c, LANES - 2 * ATTN_HEADS, 1)
        cs = jnp.where(head_lanes, cs, 0.0) + carry
        carry = cs[rb - 1:rb, :]
        fsum_ref[rs, :] = cs[:, :ATTN_HEADS]
        pieces.append(_pack3(cs, head_lanes))
    f_carry[...] = carry

    v = proj(w_qkv_ref, C_V, ATTN_WIDTH)
    ones_a = jnp.where(lane == HEAD_DIM, 1.0, 0.0)
    ones_b = jnp.where(lane == 0, 1.0, 0.0)
    for pair in range(n_pairs):
        v2 = v[:, pair * LANES:(pair + 1) * LANES]
        c0 = pair * HEADS_PER_BLOCK * LANES
        v_ref[:, c0:c0 + LANES] = jnp.where(lane < HEAD_DIM, v2, ones_a).astype(bf16)
        v_ref[:, c0 + LANES:c0 + 2 * LANES] = jnp.where(lane < HEAD_DIM, ones_b, v2).astype(bf16)

    k_lane = lax.broadcasted_iota(jnp.int32, (rb, n_pairs * LANES), 1)
    k_ones = jnp.bitwise_and(k_lane, LANES - 1) < N_SPLIT
    q_lane = lax.broadcasted_iota(jnp.int32, (rb, ATTN_HEADS * LANES), 1)
    q_head = jnp.right_shift(q_lane, LANES.bit_length() - 1)
    q_ones_lo = N_SPLIT * (1 + jnp.bitwise_and(q_head, HEADS_PER_BLOCK - 1))
    q_in_blk = jnp.bitwise_and(q_lane, LANES - 1)
    q_ones = (q_in_blk >= q_ones_lo) & (q_in_blk < q_ones_lo + N_SPLIT)
    for rs, pc in zip(row_blk, pieces):
        k_bias = jnp.dot(pc, ksel_ref[...], preferred_element_type=f32)
        k_bias = jnp.where(k_ones, 1.0, k_bias).astype(bf16)
        for pair in range(n_pairs):
            c0 = pair * HEADS_PER_BLOCK * LANES + LANES
            kx_ref[rs, c0:c0 + LANES] = k_bias[:, pair * LANES:(pair + 1) * LANES]
        q_bias = jnp.dot(pc, qsel_ref[...], preferred_element_type=f32)
        q_bias = jnp.where(q_ones, 1.0, q_bias).astype(bf16)
        for head in range(ATTN_HEADS):
            c0 = head * HEADS_PER_BLOCK * LANES + LANES
            qx_ref[rs, c0:c0 + LANES] = q_bias[:, head * LANES:(head + 1) * LANES]

    p = proj(w_mix_ref, C_CC, CONV_WIDTH) * proj(w_mix_ref, C_CIN, CONV_WIDTH)
    p_scr[SUBLANES:SUBLANES + tm, :] = p
    conv = (p_scr[SUBLANES - 2:SUBLANES - 2 + tm, :] * cw_ref[0:1, :]
            + p_scr[SUBLANES - 1:SUBLANES - 1 + tm, :] * cw_ref[1:2, :]
            + p * cw_ref[2:3, :])
    tail = p_scr[tm:tm + SUBLANES, :]
    p_scr[0:SUBLANES, :] = tail
    p_tail_ref[...] = tail
    conv_in = (proj(w_mix_ref, C_CB, CONV_WIDTH) * conv).astype(bf16)
    y_conv = jnp.dot(conv_in, wc_ref[...], preferred_element_type=f32)
    sga_ref[...] = jax.nn.sigmoid(proj(w_mix_ref, C_GA, D_MODEL)).astype(bf16)
    gcv_ref[...] = (jax.nn.sigmoid(proj(w_mix_ref, C_GC, D_MODEL)) * y_conv).astype(bf16)


def _bias_selectors():
    ksel = np.zeros((LANES, ATTN_HEADS // HEADS_PER_BLOCK * LANES), np.float32)
    qsel = np.zeros((LANES, ATTN_HEADS * LANES), np.float32)
    for h in range(ATTN_HEADS):
        for t in range(N_SPLIT):
            col = (h // HEADS_PER_BLOCK) * LANES + N_SPLIT * (1 + h % HEADS_PER_BLOCK) + t
            ksel[t * ATTN_HEADS + h, col] = -1.0
            qsel[t * ATTN_HEADS + h, h * LANES + t] = 1.0
    return jnp.asarray(ksel, bf16), jnp.asarray(qsel, bf16)


def _mix_in(h, g, w_qkv, w_mix, w_f, b_f, conv_w, w_conv_branch, p_init, f_init, batch, tm):
    rows = h.shape[0]
    nt = rows // (batch * tm)
    row_blk = lambda width: pl.BlockSpec((tm, width), lambda b, t: (b * nt + t, 0))
    out_rows = lambda width, dt: jax.ShapeDtypeStruct((rows, width), dt)
    return pl.pallas_call(
        functools.partial(_mix_in_body, tm=tm),
        grid=(batch, nt),
        in_specs=[
            row_blk(D_MODEL),
            _resident((1, D_MODEL)),
            _resident((N_QKV, D_MODEL)),
            _resident((N_MIX, D_MODEL)),
            _resident((LANES, D_MODEL)),
            _resident((1, LANES)),
            _resident((CONV_K, CONV_WIDTH)),
            _resident((CONV_WIDTH, D_MODEL)),
            _resident((SUBLANES, CONV_WIDTH)),
            _resident((1, LANES)),
            _resident((LANES, ATTN_HEADS // HEADS_PER_BLOCK * LANES)),
            _resident((LANES, ATTN_HEADS * LANES)),
        ],
        out_specs=[
            row_blk(2 * ATTN_HEADS * LANES), row_blk(ATTN_HEADS * LANES),
            row_blk(ATTN_HEADS * LANES),
            row_blk(ATTN_HEADS), row_blk(D_MODEL), row_blk(D_MODEL),
            pl.BlockSpec((SUBLANES, CONV_WIDTH), lambda b, t: (b * nt + t, 0)),
        ],
        out_shape=[
            out_rows(2 * ATTN_HEADS * LANES, bf16), out_rows(ATTN_HEADS * LANES, bf16),
            out_rows(ATTN_HEADS * LANES, bf16),
            out_rows(ATTN_HEADS, f32), out_rows(D_MODEL, bf16), out_rows(D_MODEL, bf16),
            jax.ShapeDtypeStruct((batch * nt * SUBLANES, CONV_WIDTH), f32),
        ],
        scratch_shapes=[pltpu.VMEM((tm + SUBLANES, CONV_WIDTH), f32),
                        pltpu.VMEM((1, LANES), f32)],
        compiler_params=pltpu.CompilerParams(
            dimension_semantics=("arbitrary", "arbitrary"), vmem_limit_bytes=VMEM_LIMIT_BYTES),
        name="mixer_input",
    )(h, g, w_qkv, w_mix, w_f, b_f, conv_w, w_conv_branch, p_init, f_init, *_bias_selectors())


def _attn_body(q_ref, k_ref, v_ref, km_ref, vm_ref, o_ref,
               m_scr, alpha_scr, acc_scr, s_scr, sm_scr, p_scr, pm_scr, *, tq, tk, rows, pairs):
    i = pl.program_id(2)
    low_half = lax.broadcasted_iota(jnp.int32, (tq, LANES), 1) < HEAD_DIM
    head_lanes = HEADS_PER_BLOCK * LANES
    m_scr[...] = jnp.full(m_scr.shape, MASK_VALUE, f32)
    acc_scr[...] = jnp.zeros(acc_scr.shape, f32)
    contract_last = (((1,), (1,)), ((), ()))
    lane = lax.broadcasted_iota(jnp.int32, (rows, LANES), 1)
    sub = lax.broadcasted_iota(jnp.int32, (rows, LANES), 0)
    n_heads = pairs * HEADS_PER_BLOCK
    assert n_heads % 2 == 0

    def scores(hh, keys_ref, key_rows):
        pair = hh // HEADS_PER_BLOCK
        q_head = q_ref[:, hh * head_lanes:(hh + 1) * head_lanes]
        k_pair = keys_ref[key_rows, pair * head_lanes:(pair + 1) * head_lanes]
        return lax.dot_general(q_head, k_pair, contract_last, preferred_element_type=f32)

    def softmax_rows(hh, r0, parts):
        rs = slice(r0, r0 + rows)
        blocks = [blk if valid is None else jnp.where(valid, blk, MASK_VALUE)
                  for blk, valid, _, _ in parts]
        m_old = m_scr[hh, rs, :]
        m_blk = functools.reduce(jnp.maximum, blocks)
        m_new = jnp.maximum(
            m_old, jnp.broadcast_to(jnp.max(m_blk, axis=-1, keepdims=True), (rows, LANES)))
        for blk, (_, _, dst, c0) in zip(blocks, parts):
            dst[hh % 2, rs, c0:c0 + LANES] = jnp.exp2(blk - m_new).astype(bf16)
        alpha_scr[hh, rs, :] = jnp.exp2(m_old - m_new)
        m_scr[hh, rs, :] = m_new

    def accumulate(hh, pv):
        acc_scr[hh] = alpha_scr[hh] * acc_scr[hh] + pv

    s_scr[0] = scores(0, k_ref, pl.ds(0, tk))

    def full_step(j, carry):
        start = pl.multiple_of(j * tk, tk)
        for hh in range(n_heads):
            slot = hh % 2
            if hh + 1 < n_heads:
                s_scr[1 - slot] = scores(hh + 1, k_ref, pl.ds(start, tk))
            else:
                s_scr[1 - slot] = scores(0, k_ref, pl.ds(pl.multiple_of(start + tk, tk), tk))
            for r0 in range(0, tq, rows):
                softmax_rows(hh, r0, [
                    (s_scr[slot, r0:r0 + rows, c * LANES:(c + 1) * LANES], None, p_scr, c * LANES)
                    for c in range(tk // LANES)])
            accumulate(hh, jnp.dot(p_scr[slot], v_ref[pl.ds(start, tk), hh * LANES:(hh + 1) * LANES],
                                   preferred_element_type=f32))
        return carry

    lax.fori_loop(0, lax.shift_right_logical(i, 1),
                  lambda jj, c: full_step(2 * jj + 1, full_step(2 * jj, c)), 0)

    @pl.when(jnp.bitwise_and(i, 1) == 1)
    def _():
        full_step(i - 1, 0)

    start = pl.multiple_of(i * tk, tk)
    meta_valid = lane < N_META
    sm_scr[0] = scores(0, km_ref, slice(None))
    for hh in range(n_heads):
        slot = hh % 2
        if hh + 1 < n_heads:
            s_scr[1 - slot] = scores(hh + 1, k_ref, pl.ds(start, tk))
            sm_scr[1 - slot] = scores(hh + 1, km_ref, slice(None))
        for r0 in range(0, tq, rows):
            n_cols = r0 + rows
            parts = [(s_scr[slot, r0:r0 + rows, c * LANES:(c + 1) * LANES],
                      None if (c + 1) * LANES - 1 <= r0 else (lane + c * LANES <= sub + r0),
                      p_scr, c * LANES) for c in range(n_cols // LANES)]
            parts.append((sm_scr[slot, r0:r0 + rows, :], meta_valid, pm_scr, 0))
            softmax_rows(hh, r0, parts)
            if n_cols < tk:
                p_scr[slot, r0:r0 + rows, n_cols:tk] = jnp.zeros((rows, tk - n_cols), bf16)
        accumulate(hh, jnp.dot(p_scr[slot], v_ref[pl.ds(start, tk), hh * LANES:(hh + 1) * LANES],
                               preferred_element_type=f32)
                   + jnp.dot(pm_scr[slot], vm_ref[:, hh * LANES:(hh + 1) * LANES],
                             preferred_element_type=f32))

    for pair in range(pairs):
        acc_a, acc_b = acc_scr[HEADS_PER_BLOCK * pair], acc_scr[HEADS_PER_BLOCK * pair + 1]
        out = jnp.where(low_half, acc_a / acc_a[:, HEAD_DIM:HEAD_DIM + 1], acc_b / acc_b[:, 0:1])
        o_ref[:, pair * LANES:(pair + 1) * LANES] = out.astype(bf16)


def _attention(qx, kx, v_aug, kx_meta, v_meta, batch, seq, tq, tk, rows, pairs):
    nq = seq // tq
    n_heads = pairs * HEADS_PER_BLOCK
    n_blk = ATTN_HEADS // n_heads
    pair_lanes = n_heads * LANES
    q_spec = pl.BlockSpec((tq, HEADS_PER_BLOCK * pair_lanes), lambda b, p, i: (b * nq + i, p))
    o_spec = pl.BlockSpec((tq, pairs * LANES), lambda b, p, i: (b * nq + i, p))
    kv_spec = pl.BlockSpec((seq, pair_lanes), lambda b, p, i: (b, p),
                           pipeline_mode=pl.Buffered(KV_BUFFERS))
    meta_spec = pl.BlockSpec((LANES, pair_lanes), lambda b, p, i: (0, p))
    per_head = lambda width, dt: pltpu.VMEM((n_heads, tq, width), dt)
    ring = lambda width, dt: pltpu.VMEM((2, tq, width), dt)
    return pl.pallas_call(
        functools.partial(_attn_body, tq=tq, tk=tk, rows=rows, pairs=pairs),
        grid=(batch, n_blk, nq),
        in_specs=[q_spec, kv_spec, kv_spec, meta_spec, meta_spec],
        out_specs=o_spec,
        out_shape=jax.ShapeDtypeStruct((batch * seq, ATTN_WIDTH), bf16),
        scratch_shapes=[per_head(LANES, f32), per_head(LANES, f32), per_head(LANES, f32),
                        ring(tk, f32), ring(LANES, f32), ring(tk, bf16), ring(LANES, bf16)],
        compiler_params=pltpu.CompilerParams(
            dimension_semantics=("arbitrary", "arbitrary", "arbitrary"),
            vmem_limit_bytes=VMEM_LIMIT_BYTES),
        name="forgetting_attention",
    )(qx, kx, v_aug, kx_meta, v_meta)


def _mix_out_ffn_body(h_ref, attn_ref, sga_ref, gcv_ref, wa_ref, wo_ref, g_mix_ref,
                      g_pre_ref, g_post_ref, w_in_ref, w_out_ref, o_ref, act_ref):
    y_attn = jnp.dot(attn_ref[...], wa_ref[...], preferred_element_type=f32)
    gated = sga_ref[...].astype(f32) * y_attn + gcv_ref[...].astype(f32)
    mixed = jnp.dot(gated.astype(bf16), wo_ref[...], preferred_element_type=f32)
    h2 = h_ref[...] + _rms_norm(mixed, g_mix_ref[...])
    o_ref[...] = _swiglu_residual(h2, g_pre_ref[...], g_post_ref[...],
                                  w_in_ref, w_out_ref, act_ref)


def _mix_out_ffn(h, attn, sga, gcv, w_attn_branch, w_out, g_mix, g_pre, g_post, w_in, w_ffn_out, tm):
    rows = h.shape[0]
    row_blk = lambda width: pl.BlockSpec((tm, width), lambda i: (i, 0))
    return pl.pallas_call(
        _mix_out_ffn_body,
        grid=(rows // tm,),
        in_specs=[row_blk(D_MODEL), row_blk(ATTN_WIDTH), row_blk(D_MODEL), row_blk(D_MODEL),
                  _resident((ATTN_WIDTH, D_MODEL)), _resident((D_MODEL, D_MODEL)),
                  _resident((1, D_MODEL)), _resident((1, D_MODEL)), _resident((1, D_MODEL)),
                  _resident((D_MODEL, 2 * D_FF)), _resident((D_FF, D_MODEL))],
        out_specs=row_blk(D_MODEL),
        out_shape=jax.ShapeDtypeStruct((rows, D_MODEL), f32),
        scratch_shapes=[pltpu.VMEM((tm, D_FF), bf16)],
        compiler_params=pltpu.CompilerParams(
            dimension_semantics=("arbitrary",), vmem_limit_bytes=VMEM_LIMIT_BYTES),
        name="mixer_output_ffn",
    )(h, attn, sga, gcv, w_attn_branch, w_out, g_mix, g_pre, g_post, w_in, w_ffn_out)


def kernel(x, meta_tokens, w_in, b_forget, conv_w, w_attn_branch, w_conv_branch, w_out,
           g_ffn1_pre, g_ffn1_post, w_ffn1_in, w_ffn1_out, g_mix_pre, g_mix_post,
           g_ffn2_pre, g_ffn2_post, w_ffn2_in, w_ffn2_out):
    batch, seq, d = x.shape
    assert d == D_MODEL and w_in.shape[0] == 1 and meta_tokens.shape == (N_META, D_MODEL)
    assert seq % ROW_TILE == 0 and seq % FFN_TILE == 0 and seq % ATTN_TQ == 0
    assert ATTN_TQ == ATTN_TK and ATTN_TQ % ATTN_ROWS == 0 and ATTN_ROWS % LANES == 0

    gain = lambda g: g[0].reshape(1, D_MODEL).astype(f32)
    wt = w_in[0].T
    w_f = jnp.pad(wt[N_QKV:N_QKV + ATTN_HEADS].astype(bf16), ((0, LANES - ATTN_HEADS), (0, 0)))
    b_f = jnp.pad(b_forget[0].astype(f32), (0, LANES - ATTN_HEADS)).reshape(1, LANES)
    cw = conv_w[0].astype(f32)
    w1_in, w1_out = w_ffn1_in[0].astype(bf16), w_ffn1_out[0].astype(bf16)

    rows = batch * seq
    later_weights = (
        (wt, 0, N_QKV), (wt, N_QKV + ATTN_HEADS, N_MIX),
        (w_conv_branch[0], 0, CONV_WIDTH), (w_attn_branch[0], 0, ATTN_WIDTH),
        (w_out[0], 0, D_MODEL), (w_ffn2_in[0], 0, D_MODEL), (w_ffn2_out[0], 0, D_FF))
    h1, hm, w_qkv, w_mix, wc, wa, wo, w2_in, w2_out = _ffn(
        x.reshape(rows, D_MODEL), meta_tokens.astype(f32), gain(g_ffn1_pre), gain(g_ffn1_post),
        w1_in, w1_out, FFN_TILE, later_weights)

    zeros_p = jnp.zeros((SUBLANES, CONV_WIDTH), f32)
    zeros_f = jnp.zeros((1, LANES), f32)
    _, kxm, vm, fm, _, _, pm_tail = _mix_in(hm, gain(g_mix_pre), w_qkv, w_mix, w_f, b_f, cw, wc,
                                           zeros_p, zeros_f, 1, N_META)
    pad_rows = lambda a: jnp.pad(a, ((0, LANES - N_META), (0, 0)))
    f_init = jnp.pad(fm[N_META - 1:N_META, :], ((0, 0), (0, LANES - ATTN_HEADS)))

    qx, kx, v, _, sga, gcv, _ = _mix_in(h1, gain(g_mix_pre), w_qkv, w_mix, w_f, b_f, cw, wc,
                                        pm_tail, f_init, batch, ROW_TILE)
    attn = _attention(qx, kx, v, pad_rows(kxm), pad_rows(vm),
                      batch, seq, ATTN_TQ, ATTN_TK, ATTN_ROWS, ATTN_PAIRS)
    h3 = _mix_out_ffn(h1, attn, sga, gcv, wa, wo, gain(g_mix_post), gain(g_ffn2_pre),
                      gain(g_ffn2_post), w2_in, w2_out, ROW_TILE)
    return h3.reshape(batch, seq, D_MODEL)
```

```python
import functools

import jax
import jax.numpy as jnp
import numpy as np
from jax import lax
from jax.experimental import pallas as pl
from jax.experimental.pallas import tpu as pltpu

D_MODEL = 1024
D_FF = 2816
N_META = 16
ATTN_HEADS = 8
HEAD_DIM = 64
ATTN_WIDTH = ATTN_HEADS * HEAD_DIM
CONV_WIDTH = 512
CONV_K = 3
NORM_EPS = 1e-6

LANES = 128
SUBLANES = 8
HEADS_PER_BLOCK = LANES // HEAD_DIM
VMEM_LIMIT_BYTES = 56 * 1024 * 1024
MASK_VALUE = -1e30
LOG2E = 1.4426950408889634
N_SPLIT = 3

ROW_TILE = 512
FFN_TILE = 1024
FF_CHUNK = 256
ATTN_TQ = 512
ATTN_TK = 512
ATTN_ROWS = 128
ATTN_PAIRS = 2
KV_BUFFERS = 2

C_Q, C_K, C_V = 0, 512, 1024
C_CB, C_CC, C_CIN, C_GA, C_GC = 0, 512, 1024, 1536, 2560
N_QKV = 3 * ATTN_WIDTH
N_MIX = 3 * CONV_WIDTH + 2 * D_MODEL

bf16 = jnp.bfloat16
f32 = jnp.float32


def _rms_norm(x, g):
    ms = jnp.mean(x * x, axis=-1, keepdims=True)
    return x * lax.rsqrt(ms + NORM_EPS) * g


def _resident(shape):
    nd = len(shape)
    return pl.BlockSpec(shape, lambda *_: (0,) * nd, pipeline_mode=pl.Buffered(1))


def _swiglu_residual(h, g_pre, g_post, w_in_ref, w_out_ref, act_ref):
    u = _rms_norm(h, g_pre).astype(bf16)
    for c in range(D_FF // FF_CHUNK):
        lo = c * FF_CHUNK
        a = jnp.dot(u, w_in_ref[:, lo:lo + FF_CHUNK], preferred_element_type=f32)
        b = jnp.dot(u, w_in_ref[:, D_FF + lo:D_FF + lo + FF_CHUNK], preferred_element_type=f32)
        act_ref[:, lo:lo + FF_CHUNK] = (a * jax.nn.sigmoid(a) * b).astype(bf16)
    y = jnp.dot(act_ref[...], w_out_ref[...], preferred_element_type=f32)
    return h + 0.5 * _rms_norm(y, g_post)


def _ffn_body(h_ref, meta_ref, g_pre_ref, g_post_ref, w_in_ref, w_out_ref, *rest):
    n_cast = (len(rest) - 4) // 2
    o_ref, meta_o_ref = rest[n_cast], rest[n_cast + 1]
    act_ref, meta_act_ref = rest[-2], rest[-1]
    for src, dst in zip(rest[:n_cast], rest[n_cast + 2:-2]):
        dst[...] = src[...].astype(bf16)

    @pl.when(pl.program_id(0) == 0)
    def _():
        meta_o_ref[...] = _swiglu_residual(meta_ref[...], g_pre_ref[...], g_post_ref[...],
                                           w_in_ref, w_out_ref, meta_act_ref)

    o_ref[...] = _swiglu_residual(h_ref[...], g_pre_ref[...], g_post_ref[...],
                                  w_in_ref, w_out_ref, act_ref)


def _ffn(h, meta, g_pre, g_post, w_in, w_out, tm, cast_along):
    rows = h.shape[0]
    steps = rows // tm
    row_blk = pl.BlockSpec((tm, D_MODEL), lambda i: (i, 0))
    cast_in, cast_out = [], []
    for w, first, count in cast_along:
        per, width = count // steps, w.shape[1]
        assert count % steps == 0 and per % (2 * SUBLANES) == 0 and first % SUBLANES == 0
        cast_in.append(pl.BlockSpec(
            (pl.Element(per), pl.Element(width)),
            functools.partial(lambda f, p, i: (pl.multiple_of(f + i * p, SUBLANES), 0), first, per)))
        cast_out.append(pl.BlockSpec((per, width), lambda i: (i, 0)))
    meta_blk = pl.BlockSpec(meta.shape, lambda i: (0, 0))
    return pl.pallas_call(
        _ffn_body,
        grid=(steps,),
        in_specs=[
            row_blk,
            _resident(meta.shape),
            _resident((1, D_MODEL)),
            _resident((1, D_MODEL)),
            _resident((D_MODEL, 2 * D_FF)),
            _resident((D_FF, D_MODEL)),
        ] + cast_in,
        out_specs=[row_blk, meta_blk] + cast_out,
        out_shape=[jax.ShapeDtypeStruct((rows, D_MODEL), f32),
                   jax.ShapeDtypeStruct(meta.shape, f32)]
        + [jax.ShapeDtypeStruct((count, w.shape[1]), bf16) for w, _, count in cast_along],
        scratch_shapes=[pltpu.VMEM((tm, D_FF), bf16), pltpu.VMEM((meta.shape[0], D_FF), bf16)],
        compiler_params=pltpu.CompilerParams(
            dimension_semantics=("arbitrary",), vmem_limit_bytes=VMEM_LIMIT_BYTES),
        name="ffn_half_step",
    )(h, meta, g_pre, g_post, w_in, w_out, *[w for w, _, _ in cast_along])


def _log_sigmoid(x):
    return jnp.minimum(x, 0.0) - jnp.log1p(jnp.exp(-jnp.abs(x)))


def _split3(x):
    hi = x.astype(bf16)
    r = x - hi.astype(f32)
    mid = r.astype(bf16)
    lo = (r - mid.astype(f32)).astype(bf16)
    return hi, mid, lo


def _pack3(x, head_lanes):
    hi, mid, lo = _split3(jnp.where(head_lanes, x, 0.0))
    packed = (hi.astype(f32) + pltpu.roll(mid.astype(f32), ATTN_HEADS, 1)
              + pltpu.roll(lo.astype(f32), 2 * ATTN_HEADS, 1))
    return packed.astype(bf16)


def _mix_in_body(h_ref, g_ref, w_qkv_ref, w_mix_ref, w_f_ref, bf_ref, cw_ref, wc_ref,
                 p_init_ref, f_init_ref, ksel_ref, qsel_ref,
                 qx_ref, kx_ref, v_ref, fsum_ref, sga_ref, gcv_ref, p_tail_ref,
                 p_scr, f_carry, *, tm):
    @pl.when(pl.program_id(1) == 0)
    def _():
        p_scr[0:SUBLANES, :] = p_init_ref[...]
        f_carry[...] = f_init_ref[...]

    u = _rms_norm(h_ref[...], g_ref[...]).astype(bf16)

    def proj(wt_ref, lo, width):
        return lax.dot_general(u, wt_ref[lo:lo + width, :], (((1,), (1,)), ((), ())),
                               preferred_element_type=f32)

    rb = min(tm, LANES)
    n_blk = tm // rb
    row_blk = [slice(blk * rb, (blk + 1) * rb) for blk in range(n_blk)]
    n_pairs = ATTN_HEADS // HEADS_PER_BLOCK
    head_lanes = lax.broadcasted_iota(jnp.int32, (rb, LANES), 1) < ATTN_HEADS
    log_f = _log_sigmoid(proj(w_f_ref, 0, LANES) + bf_ref[...]) * LOG2E

    q = (proj(w_qkv_ref, C_Q, ATTN_WIDTH) * (LOG2E * HEAD_DIM ** -0.5)).astype(bf16)
    lane = lax.broadcasted_iota(jnp.int32, (tm, LANES), 1)
    for pair in range(n_pairs):
        q2 = q[:, pair * LANES:(pair + 1) * LANES]
        zeros = jnp.zeros_like(q2)
        c0 = 2 * pair * HEADS_PER_BLOCK * LANES
        qx_ref[:, c0:c0 + LANES] = jnp.where(lane < HEAD_DIM, q2, zeros)
        qx_ref[:, c0 + 2 * LANES:c0 + 3 * LANES] = jnp.where(lane < HEAD_DIM, zeros, q2)

    tri = (lax.broadcasted_iota(jnp.int32, (rb, rb), 0)
           >= lax.broadcasted_iota(jnp.int32, (rb, rb), 1)).astype(bf16)
    c3 = [jnp.dot(tri, _pack3(log_f[rs, :], head_lanes), preferred_element_type=f32)
          for rs in row_blk]

    k = proj(w_qkv_ref, C_K, ATTN_WIDTH)
    for pair in range(n_pairs):
        c0 = pair * HEADS_PER_BLOCK * LANES
        kx_ref[:, c0:c0 + LANES] = k[:, pair * LANES:(pair + 1) * LANES].astype(bf16)

    carry = f_carry[...]
    pieces = []
    for rs, c in zip(row_blk, c3):
        cs = c + pltpu.roll(c, LANES - ATTN_HEADS, 1) + pltpu.roll(c, LANES - 2 * ATTN_HEADS, 1)
        cs = jnp.where(head_lanes, cs, 0.0) + carry
        carry = cs[rb - 1:rb, :]
        fsum_ref[rs, :] = cs[:, :ATTN_HEADS]
        pieces.append(_pack3(cs, head_lanes))
    f_carry[...] = carry

    v = proj(w_qkv_ref, C_V, ATTN_WIDTH)
    ones_a = jnp.where(lane == HEAD_DIM, 1.0, 0.0)
    ones_b = jnp.where(lane == 0, 1.0, 0.0)
    for pair in range(n_pairs):
        v2 = v[:, pair * LANES:(pair + 1) * LANES]
        c0 = pair * HEADS_PER_BLOCK * LANES
        v_ref[:, c0:c0 + LANES] = jnp.where(lane < HEAD_DIM, v2, ones_a).astype(bf16)
        v_ref[:, c0 + LANES:c0 + 2 * LANES] = jnp.where(lane < HEAD_DIM, ones_b, v2).astype(bf16)

    k_lane = lax.broadcasted_iota(jnp.int32, (rb, n_pairs * LANES), 1)
    k_ones = jnp.bitwise_and(k_lane, LANES - 1) < N_SPLIT
    q_lane = lax.broadcasted_iota(jnp.int32, (rb, ATTN_HEADS * LANES), 1)
    q_head = jnp.right_shift(q_lane, LANES.bit_length() - 1)
    q_ones_lo = N_SPLIT * (1 + jnp.bitwise_and(q_head, HEADS_PER_BLOCK - 1))
    q_in_blk = jnp.bitwise_and(q_lane, LANES - 1)
    q_ones = (q_in_blk >= q_ones_lo) & (q_in_blk < q_ones_lo + N_SPLIT)
    for rs, pc in zip(row_blk, pieces):
        k_bias = jnp.dot(pc, ksel_ref[...], preferred_element_type=f32)
        k_bias = jnp.where(k_ones, 1.0, k_bias).astype(bf16)
        for pair in range(n_pairs):
            c0 = pair * HEADS_PER_BLOCK * LANES + LANES
            kx_ref[rs, c0:c0 + LANES] = k_bias[:, pair * LANES:(pair + 1) * LANES]
        q_bias = jnp.dot(pc, qsel_ref[...], preferred_element_type=f32)
        q_bias = jnp.where(q_ones, 1.0, q_bias).astype(bf16)
        for head in range(ATTN_HEADS):
            c0 = head * HEADS_PER_BLOCK * LANES + LANES
            qx_ref[rs, c0:c0 + LANES] = q_bias[:, head * LANES:(head + 1) * LANES]

    p = proj(w_mix_ref, C_CC, CONV_WIDTH) * proj(w_mix_ref, C_CIN, CONV_WIDTH)
    p_scr[SUBLANES:SUBLANES + tm, :] = p
    conv = (p_scr[SUBLANES - 2:SUBLANES - 2 + tm, :] * cw_ref[0:1, :]
            + p_scr[SUBLANES - 1:SUBLANES - 1 + tm, :] * cw_ref[1:2, :]
            + p * cw_ref[2:3, :])
    tail = p_scr[tm:tm + SUBLANES, :]
    p_scr[0:SUBLANES, :] = tail
    p_tail_ref[...] = tail
    conv_in = (proj(w_mix_ref, C_CB, CONV_WIDTH) * conv).astype(bf16)
    y_conv = jnp.dot(conv_in, wc_ref[...], preferred_element_type=f32)
    sga_ref[...] = jax.nn.sigmoid(proj(w_mix_ref, C_GA, D_MODEL)).astype(bf16)
    gcv_ref[...] = (jax.nn.sigmoid(proj(w_mix_ref, C_GC, D_MODEL)) * y_conv).astype(bf16)


def _bias_selectors():
    ksel = np.zeros((LANES, ATTN_HEADS // HEADS_PER_BLOCK * LANES), np.float32)
    qsel = np.zeros((LANES, ATTN_HEADS * LANES), np.float32)
    for h in range(ATTN_HEADS):
        for t in range(N_SPLIT):
            col = (h // HEADS_PER_BLOCK) * LANES + N_SPLIT * (1 + h % HEADS_PER_BLOCK) + t
            ksel[t * ATTN_HEADS + h, col] = -1.0
            qsel[t * ATTN_HEADS + h, h * LANES + t] = 1.0
    return jnp.asarray(ksel, bf16), jnp.asarray(qsel, bf16)


def _mix_in(h, g, w_qkv, w_mix, w_f, b_f, conv_w, w_conv_branch, p_init, f_init, batch, tm):
    rows = h.shape[0]
    nt = rows // (batch * tm)
    row_blk = lambda width: pl.BlockSpec((tm, width), lambda b, t: (b * nt + t, 0))
    out_rows = lambda width, dt: jax.ShapeDtypeStruct((rows, width), dt)
    return pl.pallas_call(
        functools.partial(_mix_in_body, tm=tm),
        grid=(batch, nt),
        in_specs=[
            row_blk(D_MODEL),
            _resident((1, D_MODEL)),
            _resident((N_QKV, D_MODEL)),
            _resident((N_MIX, D_MODEL)),
            _resident((LANES, D_MODEL)),
            _resident((1, LANES)),
            _resident((CONV_K, CONV_WIDTH)),
            _resident((CONV_WIDTH, D_MODEL)),
            _resident((SUBLANES, CONV_WIDTH)),
            _resident((1, LANES)),
            _resident((LANES, ATTN_HEADS // HEADS_PER_BLOCK * LANES)),
            _resident((LANES, ATTN_HEADS * LANES)),
        ],
        out_specs=[
            row_blk(2 * ATTN_HEADS * LANES), row_blk(ATTN_HEADS * LANES),
            row_blk(ATTN_HEADS * LANES),
            row_blk(ATTN_HEADS), row_blk(D_MODEL), row_blk(D_MODEL),
            pl.BlockSpec((SUBLANES, CONV_WIDTH), lambda b, t: (b * nt + t, 0)),
        ],
        out_shape=[
            out_rows(2 * ATTN_HEADS * LANES, bf16), out_rows(ATTN_HEADS * LANES, bf16),
            out_rows(ATTN_HEADS * LANES, bf16),
            out_rows(ATTN_HEADS, f32), out_rows(D_MODEL, bf16), out_rows(D_MODEL, bf16),
            jax.ShapeDtypeStruct((batch * nt * SUBLANES, CONV_WIDTH), f32),
        ],
        scratch_shapes=[pltpu.VMEM((tm + SUBLANES, CONV_WIDTH), f32),
                        pltpu.VMEM((1, LANES), f32)],
        compiler_params=pltpu.CompilerParams(
            dimension_semantics=("arbitrary", "arbitrary"), vmem_limit_bytes=VMEM_LIMIT_BYTES),
        name="mixer_input",
    )(h, g, w_qkv, w_mix, w_f, b_f, conv_w, w_conv_branch, p_init, f_init, *_bias_selectors())


def _attn_body(q_ref, k_ref, v_ref, km_ref, vm_ref, o_ref,
               m_scr, alpha_scr, acc_scr, s_scr, sm_scr, p_scr, pm_scr, *, tq, tk, rows, pairs):
    i = pl.program_id(2)
    low_half = lax.broadcasted_iota(jnp.int32, (tq, LANES), 1) < HEAD_DIM
    head_lanes = HEADS_PER_BLOCK * LANES
    m_scr[...] = jnp.full(m_scr.shape, MASK_VALUE, f32)
    acc_scr[...] = jnp.zeros(acc_scr.shape, f32)
    contract_last = (((1,), (1,)), ((), ()))
    lane = lax.broadcasted_iota(jnp.int32, (rows, LANES), 1)
    sub = lax.broadcasted_iota(jnp.int32, (rows, LANES), 0)
    n_heads = pairs * HEADS_PER_BLOCK
    assert n_heads % 2 == 0

    def scores(hh, keys_ref, key_rows):
        pair = hh // HEADS_PER_BLOCK
        q_head = q_ref[:, hh * head_lanes:(hh + 1) * head_lanes]
        k_pair = keys_ref[key_rows, pair * head_lanes:(pair + 1) * head_lanes]
        return lax.dot_general(q_head, k_pair, contract_last, preferred_element_type=f32)

    def softmax_rows(hh, r0, parts):
        rs = slice(r0, r0 + rows)
        blocks = [blk if valid is None else jnp.where(valid, blk, MASK_VALUE)
                  for blk, valid, _, _ in parts]
        m_old = m_scr[hh, rs, :]
        m_blk = functools.reduce(jnp.maximum, blocks)
        m_new = jnp.maximum(
            m_old, jnp.broadcast_to(jnp.max(m_blk, axis=-1, keepdims=True), (rows, LANES)))
        for blk, (_, _, dst, c0) in zip(blocks, parts):
            dst[hh % 2, rs, c0:c0 + LANES] = jnp.exp2(blk - m_new).astype(bf16)
        alpha_scr[hh, rs, :] = jnp.exp2(m_old - m_new)
        m_scr[hh, rs, :] = m_new

    def accumulate(hh, pv):
        acc_scr[hh] = alpha_scr[hh] * acc_scr[hh] + pv

    s_scr[0] = scores(0, k_ref, pl.ds(0, tk))

    def full_step(j, carry):
        start = pl.multiple_of(j * tk, tk)
        for hh in range(n_heads):
            slot = hh % 2
            if hh + 1 < n_heads:
                s_scr[1 - slot] = scores(hh + 1, k_ref, pl.ds(start, tk))
            else:
                s_scr[1 - slot] = scores(0, k_ref, pl.ds(pl.multiple_of(start + tk, tk), tk))
            for r0 in range(0, tq, rows):
                softmax_rows(hh, r0, [
                    (s_scr[slot, r0:r0 + rows, c * LANES:(c + 1) * LANES], None, p_scr, c * LANES)
                    for c in range(tk // LANES)])
            accumulate(hh, jnp.dot(p_scr[slot], v_ref[pl.ds(start, tk), hh * LANES:(hh + 1) * LANES],
                                   preferred_element_type=f32))
        return carry

    lax.fori_loop(0, lax.shift_right_logical(i, 1),
                  lambda jj, c: full_step(2 * jj + 1, full_step(2 * jj, c)), 0)

    @pl.when(jnp.bitwise_and(i, 1) == 1)
    def _():
        full_step(i - 1, 0)

    start = pl.multiple_of(i * tk, tk)
    meta_valid = lane < N_META
    sm_scr[0] = scores(0, km_ref, slice(None))
    for hh in range(n_heads):
        slot = hh % 2
        if hh + 1 < n_heads:
            s_scr[1 - slot] = scores(hh + 1, k_ref, pl.ds(start, tk))
            sm_scr[1 - slot] = scores(hh + 1, km_ref, slice(None))
        for r0 in range(0, tq, rows):
            n_cols = r0 + rows
            parts = [(s_scr[slot, r0:r0 + rows, c * LANES:(c + 1) * LANES],
                      None if (c + 1) * LANES - 1 <= r0 else (lane + c * LANES <= sub + r0),
                      p_scr, c * LANES) for c in range(n_cols // LANES)]
            parts.append((sm_scr[slot, r0:r0 + rows, :], meta_valid, pm_scr, 0))
            softmax_rows(hh, r0, parts)
            if n_cols < tk:
                p_scr[slot, r0:r0 + rows, n_cols:tk] = jnp.zeros((rows, tk - n_cols), bf16)
        accumulate(hh, jnp.dot(p_scr[slot], v_ref[pl.ds(start, tk), hh * LANES:(hh + 1) * LANES],
                               preferred_element_type=f32)
                   + jnp.dot(pm_scr[slot], vm_ref[:, hh * LANES:(hh + 1) * LANES],
                             preferred_element_type=f32))

    for pair in range(pairs):
        acc_a, acc_b = acc_scr[HEADS_PER_BLOCK * pair], acc_scr[HEADS_PER_BLOCK * pair + 1]
        out = jnp.where(low_half, acc_a / acc_a[:, HEAD_DIM:HEAD_DIM + 1], acc_b / acc_b[:, 0:1])
        o_ref[:, pair * LANES:(pair + 1) * LANES] = out.astype(bf16)


def _attention(qx, kx, v_aug, kx_meta, v_meta, batch, seq, tq, tk, rows, pairs):
    nq = seq // tq
    n_heads = pairs * HEADS_PER_BLOCK
    n_blk = ATTN_HEADS // n_heads
    pair_lanes = n_heads * LANES
    q_spec = pl.BlockSpec((tq, HEADS_PER_BLOCK * pair_lanes), lambda b, p, i: (b * nq + i, p))
    o_spec = pl.BlockSpec((tq, pairs * LANES), lambda b, p, i: (b * nq + i, p))
    kv_spec = pl.BlockSpec((seq, pair_lanes), lambda b, p, i: (b, p),
                           pipeline_mode=pl.Buffered(KV_BUFFERS))
    meta_spec = pl.BlockSpec((LANES, pair_lanes), lambda b, p, i: (0, p))
    per_head = lambda width, dt: pltpu.VMEM((n_heads, tq, width), dt)
    ring = lambda width, dt: pltpu.VMEM((2, tq, width), dt)
    return pl.pallas_call(
        functools.partial(_attn_body, tq=tq, tk=tk, rows=rows, pairs=pairs),
        grid=(batch, n_blk, nq),
        in_specs=[q_spec, kv_spec, kv_spec, meta_spec, meta_spec],
        out_specs=o_spec,
        out_shape=jax.ShapeDtypeStruct((batch * seq, ATTN_WIDTH), bf16),
        scratch_shapes=[per_head(LANES, f32), per_head(LANES, f32), per_head(LANES, f32),
                        ring(tk, f32), ring(LANES, f32), ring(tk, bf16), ring(LANES, bf16)],
        compiler_params=pltpu.CompilerParams(
            dimension_semantics=("arbitrary", "arbitrary", "arbitrary"),
            vmem_limit_bytes=VMEM_LIMIT_BYTES),
        name="forgetting_attention",
    )(qx, kx, v_aug, kx_meta, v_meta)


def _mix_out_ffn_body(h_ref, attn_ref, sga_ref, gcv_ref, wa_ref, wo_ref, g_mix_ref,
                      g_pre_ref, g_post_ref, w_in_ref, w_out_ref, o_ref, act_ref):
    y_attn = jnp.dot(attn_ref[...], wa_ref[...], preferred_element_type=f32)
    gated = sga_ref[...].astype(f32) * y_attn + gcv_ref[...].astype(f32)
    mixed = jnp.dot(gated.astype(bf16), wo_ref[...], preferred_element_type=f32)
    h2 = h_ref[...] + _rms_norm(mixed, g_mix_ref[...])
    o_ref[...] = _swiglu_residual(h2, g_pre_ref[...], g_post_ref[...],
                                  w_in_ref, w_out_ref, act_ref)


def _mix_out_ffn(h, attn, sga, gcv, w_attn_branch, w_out, g_mix, g_pre, g_post, w_in, w_ffn_out, tm):
    rows = h.shape[0]
    row_blk = lambda width: pl.BlockSpec((tm, width), lambda i: (i, 0))
    return pl.pallas_call(
        _mix_out_ffn_body,
        grid=(rows // tm,),
        in_specs=[row_blk(D_MODEL), row_blk(ATTN_WIDTH), row_blk(D_MODEL), row_blk(D_MODEL),
                  _resident((ATTN_WIDTH, D_MODEL)), _resident((D_MODEL, D_MODEL)),
                  _resident((1, D_MODEL)), _resident((1, D_MODEL)), _resident((1, D_MODEL)),
                  _resident((D_MODEL, 2 * D_FF)), _resident((D_FF, D_MODEL))],
        out_specs=row_blk(D_MODEL),
        out_shape=jax.ShapeDtypeStruct((rows, D_MODEL), f32),
        scratch_shapes=[pltpu.VMEM((tm, D_FF), bf16)],
        compiler_params=pltpu.CompilerParams(
            dimension_semantics=("arbitrary",), vmem_limit_bytes=VMEM_LIMIT_BYTES),
        name="mixer_output_ffn",
    )(h, attn, sga, gcv, w_attn_branch, w_out, g_mix, g_pre, g_post, w_in, w_ffn_out)


def kernel(x, meta_tokens, w_in, b_forget, conv_w, w_attn_branch, w_conv_branch, w_out,
           g_ffn1_pre, g_ffn1_post, w_ffn1_in, w_ffn1_out, g_mix_pre, g_mix_post,
           g_ffn2_pre, g_ffn2_post, w_ffn2_in, w_ffn2_out):
    batch, seq, d = x.shape
    assert d == D_MODEL and w_in.shape[0] == 1 and meta_tokens.shape == (N_META, D_MODEL)
    assert seq % ROW_TILE == 0 and seq % FFN_TILE == 0 and seq % ATTN_TQ == 0
    assert ATTN_TQ == ATTN_TK and ATTN_TQ % ATTN_ROWS == 0 and ATTN_ROWS % LANES == 0

    gain = lambda g: g[0].reshape(1, D_MODEL).astype(f32)
    wt = w_in[0].T
    w_f = jnp.pad(wt[N_QKV:N_QKV + ATTN_HEADS].astype(bf16), ((0, LANES - ATTN_HEADS), (0, 0)))
    b_f = jnp.pad(b_forget[0].astype(f32), (0, LANES - ATTN_HEADS)).reshape(1, LANES)
    cw = conv_w[0].astype(f32)
    w1_in, w1_out = w_ffn1_in[0].astype(bf16), w_ffn1_out[0].astype(bf16)

    rows = batch * seq
    later_weights = (
        (wt, 0, N_QKV), (wt, N_QKV + ATTN_HEADS, N_MIX),
        (w_conv_branch[0], 0, CONV_WIDTH), (w_attn_branch[0], 0, ATTN_WIDTH),
        (w_out[0], 0, D_MODEL), (w_ffn2_in[0], 0, D_MODEL), (w_ffn2_out[0], 0, D_FF))
    h1, hm, w_qkv, w_mix, wc, wa, wo, w2_in, w2_out = _ffn(
        x.reshape(rows, D_MODEL), meta_tokens.astype(f32), gain(g_ffn1_pre), gain(g_ffn1_post),
        w1_in, w1_out, FFN_TILE, later_weights)

    zeros_p = jnp.zeros((SUBLANES, CONV_WIDTH), f32)
    zeros_f = jnp.zeros((1, LANES), f32)
    _, kxm, vm, fm, _, _, pm_tail = _mix_in(hm, gain(g_mix_pre), w_qkv, w_mix, w_f, b_f, cw, wc,
                                           zeros_p, zeros_f, 1, N_META)
    pad_rows = lambda a: jnp.pad(a, ((0, LANES - N_META), (0, 0)))
    f_init = jnp.pad(fm[N_META - 1:N_META, :], ((0, 0), (0, LANES - ATTN_HEADS)))

    qx, kx, v, _, sga, gcv, _ = _mix_in(h1, gain(g_mix_pre), w_qkv, w_mix, w_f, b_f, cw, wc,
                                        pm_tail, f_init, batch, ROW_TILE)
    attn = _attention(qx, kx, v, pad_rows(kxm), pad_rows(vm),
                      batch, seq, ATTN_TQ, ATTN_TK, ATTN_ROWS, ATTN_PAIRS)
    h3 = _mix_out_ffn(h1, attn, sga, gcv, wa, wo, gain(g_mix_post), gain(g_ffn2_pre),
                      gain(g_ffn2_post), w2_in, w2_out, ROW_TILE)
    return h3.reshape(batch, seq, D_MODEL)
```

```python
import functools

import jax
import jax.numpy as jnp
import numpy as np
from jax import lax
from jax.experimental import pallas as pl
from jax.experimental.pallas import tpu as pltpu

D_MODEL = 1024
D_FF = 2816
N_META = 16
ATTN_HEADS = 8
HEAD_DIM = 64
ATTN_WIDTH = ATTN_HEADS * HEAD_DIM
CONV_WIDTH = 512
CONV_K = 3
NORM_EPS = 1e-6

LANES = 128
SUBLANES = 8
HEADS_PER_BLOCK = LANES // HEAD_DIM
VMEM_LIMIT_BYTES = 56 * 1024 * 1024
MASK_VALUE = -1e30
LOG2E = 1.4426950408889634
N_SPLIT = 3

ROW_TILE = 512
FFN_TILE = 1024
FF_CHUNK = 256
ATTN_TQ = 512
ATTN_TK = 512
ATTN_ROWS = 128
ATTN_PAIRS = 4

C_Q, C_K, C_V = 0, 512, 1024
C_CB, C_CC, C_CIN, C_GA, C_GC = 0, 512, 1024, 1536, 2560
N_QKV = 3 * ATTN_WIDTH
N_MIX = 3 * CONV_WIDTH + 2 * D_MODEL

bf16 = jnp.bfloat16
f32 = jnp.float32


def _rms_norm(x, g):
    ms = jnp.mean(x * x, axis=-1, keepdims=True)
    return x * lax.rsqrt(ms + NORM_EPS) * g


def _resident(shape):
    nd = len(shape)
    return pl.BlockSpec(shape, lambda *_: (0,) * nd, pipeline_mode=pl.Buffered(1))


def _swiglu_residual(h, g_pre, g_post, w_in_ref, w_out_ref, act_ref):
    u = _rms_norm(h, g_pre).astype(bf16)
    for c in range(D_FF // FF_CHUNK):
        lo = c * FF_CHUNK
        a = jnp.dot(u, w_in_ref[:, lo:lo + FF_CHUNK], preferred_element_type=f32)
        b = jnp.dot(u, w_in_ref[:, D_FF + lo:D_FF + lo + FF_CHUNK], preferred_element_type=f32)
        act_ref[:, lo:lo + FF_CHUNK] = (a * jax.nn.sigmoid(a) * b).astype(bf16)
    y = jnp.dot(act_ref[...], w_out_ref[...], preferred_element_type=f32)
    return h + 0.5 * _rms_norm(y, g_post)


def _ffn_body(h_ref, meta_ref, g_pre_ref, g_post_ref, w_in_ref, w_out_ref, *rest):
    n_cast = (len(rest) - 4) // 2
    o_ref, meta_o_ref = rest[n_cast], rest[n_cast + 1]
    act_ref, meta_act_ref = rest[-2], rest[-1]
    for src, dst in zip(rest[:n_cast], rest[n_cast + 2:-2]):
        dst[...] = src[...].astype(bf16)

    @pl.when(pl.program_id(0) == 0)
    def _():
        meta_o_ref[...] = _swiglu_residual(meta_ref[...], g_pre_ref[...], g_post_ref[...],
                                           w_in_ref, w_out_ref, meta_act_ref)

    o_ref[...] = _swiglu_residual(h_ref[...], g_pre_ref[...], g_post_ref[...],
                                  w_in_ref, w_out_ref, act_ref)


def _ffn(h, meta, g_pre, g_post, w_in, w_out, tm, cast_along):
    rows = h.shape[0]
    steps = rows // tm
    row_blk = pl.BlockSpec((tm, D_MODEL), lambda i: (i, 0))
    cast_in, cast_out = [], []
    for w, first, count in cast_along:
        per, width = count // steps, w.shape[1]
        assert count % steps == 0 and per % (2 * SUBLANES) == 0 and first % SUBLANES == 0
        cast_in.append(pl.BlockSpec(
            (pl.Element(per), pl.Element(width)),
            functools.partial(lambda f, p, i: (pl.multiple_of(f + i * p, SUBLANES), 0), first, per)))
        cast_out.append(pl.BlockSpec((per, width), lambda i: (i, 0)))
    meta_blk = pl.BlockSpec(meta.shape, lambda i: (0, 0))
    return pl.pallas_call(
        _ffn_body,
        grid=(steps,),
        in_specs=[
            row_blk,
            _resident(meta.shape),
            _resident((1, D_MODEL)),
            _resident((1, D_MODEL)),
            _resident((D_MODEL, 2 * D_FF)),
            _resident((D_FF, D_MODEL)),
        ] + cast_in,
        out_specs=[row_blk, meta_blk] + cast_out,
        out_shape=[jax.ShapeDtypeStruct((rows, D_MODEL), f32),
                   jax.ShapeDtypeStruct(meta.shape, f32)]
        + [jax.ShapeDtypeStruct((count, w.shape[1]), bf16) for w, _, count in cast_along],
        scratch_shapes=[pltpu.VMEM((tm, D_FF), bf16), pltpu.VMEM((meta.shape[0], D_FF), bf16)],
        compiler_params=pltpu.CompilerParams(
            dimension_semantics=("arbitrary",), vmem_limit_bytes=VMEM_LIMIT_BYTES),
        name="ffn_half_step",
    )(h, meta, g_pre, g_post, w_in, w_out, *[w for w, _, _ in cast_along])


def _log_sigmoid(x):
    return jnp.minimum(x, 0.0) - jnp.log1p(jnp.exp(-jnp.abs(x)))


def _split3(x):
    hi = x.astype(bf16)
    r = x - hi.astype(f32)
    mid = r.astype(bf16)
    lo = (r - mid.astype(f32)).astype(bf16)
    return hi, mid, lo


def _pack3(x, head_lanes):
    hi, mid, lo = _split3(jnp.where(head_lanes, x, 0.0))
    packed = (hi.astype(f32) + pltpu.roll(mid.astype(f32), ATTN_HEADS, 1)
              + pltpu.roll(lo.astype(f32), 2 * ATTN_HEADS, 1))
    return packed.astype(bf16)


def _mix_in_body(h_ref, g_ref, w_qkv_ref, w_mix_ref, w_f_ref, bf_ref, cw_ref, wc_ref,
                 p_init_ref, f_init_ref, ksel_ref, qsel_ref,
                 qx_ref, kx_ref, v_ref, fsum_ref, sga_ref, gcv_ref, p_tail_ref,
                 p_scr, f_carry, *, tm):
    @pl.when(pl.program_id(1) == 0)
    def _():
        p_scr[0:SUBLANES, :] = p_init_ref[...]
        f_carry[...] = f_init_ref[...]

    u = _rms_norm(h_ref[...], g_ref[...]).astype(bf16)

    def proj(wt_ref, lo, width):
        return lax.dot_general(u, wt_ref[lo:lo + width, :], (((1,), (1,)), ((), ())),
                               preferred_element_type=f32)

    rb = min(tm, LANES)
    n_blk = tm // rb
    row_blk = [slice(blk * rb, (blk + 1) * rb) for blk in range(n_blk)]
    n_pairs = ATTN_HEADS // HEADS_PER_BLOCK
    head_lanes = lax.broadcasted_iota(jnp.int32, (rb, LANES), 1) < ATTN_HEADS
    log_f = _log_sigmoid(proj(w_f_ref, 0, LANES) + bf_ref[...]) * LOG2E

    q = (proj(w_qkv_ref, C_Q, ATTN_WIDTH) * (LOG2E * HEAD_DIM ** -0.5)).astype(bf16)
    lane = lax.broadcasted_iota(jnp.int32, (tm, LANES), 1)
    for pair in range(n_pairs):
        q2 = q[:, pair * LANES:(pair + 1) * LANES]
        zeros = jnp.zeros_like(q2)
        c0 = 2 * pair * HEADS_PER_BLOCK * LANES
        qx_ref[:, c0:c0 + LANES] = jnp.where(lane < HEAD_DIM, q2, zeros)
        qx_ref[:, c0 + 2 * LANES:c0 + 3 * LANES] = jnp.where(lane < HEAD_DIM, zeros, q2)

    tri = (lax.broadcasted_iota(jnp.int32, (rb, rb), 0)
           >= lax.broadcasted_iota(jnp.int32, (rb, rb), 1)).astype(bf16)
    c3 = [jnp.dot(tri, _pack3(log_f[rs, :], head_lanes), preferred_element_type=f32)
          for rs in row_blk]

    k = proj(w_qkv_ref, C_K, ATTN_WIDTH)
    for pair in range(n_pairs):
        c0 = pair * HEADS_PER_BLOCK * LANES
        kx_ref[:, c0:c0 + LANES] = k[:, pair * LANES:(pair + 1) * LANES].astype(bf16)

    carry = f_carry[...]
    pieces = []
    for rs, c in zip(row_blk, c3):
        cs = c + pltpu.roll(c, LANES - ATTN_HEADS, 1) + pltpu.roll(c, LANES - 2 * ATTN_HEADS, 1)
        cs = jnp.where(head_lanes, cs, 0.0) + carry
        carry = cs[rb - 1:rb, :]
        fsum_ref[rs, :] = cs[:, :ATTN_HEADS]
        pieces.append(_pack3(cs, head_lanes))
    f_carry[...] = carry

    v = proj(w_qkv_ref, C_V, ATTN_WIDTH)
    ones_a = jnp.where(lane == HEAD_DIM, 1.0, 0.0)
    ones_b = jnp.where(lane == 0, 1.0, 0.0)
    for pair in range(n_pairs):
        v2 = v[:, pair * LANES:(pair + 1) * LANES]
        c0 = pair * HEADS_PER_BLOCK * LANES
        v_ref[:, c0:c0 + LANES] = jnp.where(lane < HEAD_DIM, v2, ones_a).astype(bf16)
        v_ref[:, c0 + LANES:c0 + 2 * LANES] = jnp.where(lane < HEAD_DIM, ones_b, v2).astype(bf16)

    k_lane = lax.broadcasted_iota(jnp.int32, (rb, n_pairs * LANES), 1)
    k_ones = jnp.bitwise_and(k_lane, LANES - 1) < N_SPLIT
    q_lane = lax.broadcasted_iota(jnp.int32, (rb, ATTN_HEADS * LANES), 1)
    q_head = jnp.right_shift(q_lane, LANES.bit_length() - 1)
    q_ones_lo = N_SPLIT * (1 + jnp.bitwise_and(q_head, HEADS_PER_BLOCK - 1))
    q_in_blk = jnp.bitwise_and(q_lane, LANES - 1)
    q_ones = (q_in_blk >= q_ones_lo) & (q_in_blk < q_ones_lo + N_SPLIT)
    for rs, pc in zip(row_blk, pieces):
        k_bias = jnp.dot(pc, ksel_ref[...], preferred_element_type=f32)
        k_bias = jnp.where(k_ones, 1.0, k_bias).astype(bf16)
        for pair in range(n_pairs):
            c0 = pair * HEADS_PER_BLOCK * LANES + LANES
            kx_ref[rs, c0:c0 + LANES] = k_bias[:, pair * LANES:(pair + 1) * LANES]
        q_bias = jnp.dot(pc, qsel_ref[...], preferred_element_type=f32)
        q_bias = jnp.where(q_ones, 1.0, q_bias).astype(bf16)
        for head in range(ATTN_HEADS):
            c0 = head * HEADS_PER_BLOCK * LANES + LANES
            qx_ref[rs, c0:c0 + LANES] = q_bias[:, head * LANES:(head + 1) * LANES]

    p = proj(w_mix_ref, C_CC, CONV_WIDTH) * proj(w_mix_ref, C_CIN, CONV_WIDTH)
    p_scr[SUBLANES:SUBLANES + tm, :] = p
    conv = (p_scr[SUBLANES - 2:SUBLANES - 2 + tm, :] * cw_ref[0:1, :]
            + p_scr[SUBLANES - 1:SUBLANES - 1 + tm, :] * cw_ref[1:2, :]
            + p * cw_ref[2:3, :])
    tail = p_scr[tm:tm + SUBLANES, :]
    p_scr[0:SUBLANES, :] = tail
    p_tail_ref[...] = tail
    conv_in = (proj(w_mix_ref, C_CB, CONV_WIDTH) * conv).astype(bf16)
    y_conv = jnp.dot(conv_in, wc_ref[...], preferred_element_type=f32)
    sga_ref[...] = jax.nn.sigmoid(proj(w_mix_ref, C_GA, D_MODEL)).astype(bf16)
    gcv_ref[...] = (jax.nn.sigmoid(proj(w_mix_ref, C_GC, D_MODEL)) * y_conv).astype(bf16)


def _bias_selectors():
    ksel = np.zeros((LANES, ATTN_HEADS // HEADS_PER_BLOCK * LANES), np.float32)
    qsel = np.zeros((LANES, ATTN_HEADS * LANES), np.float32)
    for h in range(ATTN_HEADS):
        for t in range(N_SPLIT):
            col = (h // HEADS_PER_BLOCK) * LANES + N_SPLIT * (1 + h % HEADS_PER_BLOCK) + t
            ksel[t * ATTN_HEADS + h, col] = -1.0
            qsel[t * ATTN_HEADS + h, h * LANES + t] = 1.0
    return jnp.asarray(ksel, bf16), jnp.asarray(qsel, bf16)


def _mix_in(h, g, w_qkv, w_mix, w_f, b_f, conv_w, w_conv_branch, p_init, f_init, batch, tm):
    rows = h.shape[0]
    nt = rows // (batch * tm)
    row_blk = lambda width: pl.BlockSpec((tm, width), lambda b, t: (b * nt + t, 0))
    out_rows = lambda width, dt: jax.ShapeDtypeStruct((rows, width), dt)
    return pl.pallas_call(
        functools.partial(_mix_in_body, tm=tm),
        grid=(batch, nt),
        in_specs=[
            row_blk(D_MODEL),
            _resident((1, D_MODEL)),
            _resident((N_QKV, D_MODEL)),
            _resident((N_MIX, D_MODEL)),
            _resident((LANES, D_MODEL)),
            _resident((1, LANES)),
            _resident((CONV_K, CONV_WIDTH)),
            _resident((CONV_WIDTH, D_MODEL)),
            _resident((SUBLANES, CONV_WIDTH)),
            _resident((1, LANES)),
            _resident((LANES, ATTN_HEADS // HEADS_PER_BLOCK * LANES)),
            _resident((LANES, ATTN_HEADS * LANES)),
        ],
        out_specs=[
            row_blk(2 * ATTN_HEADS * LANES), row_blk(ATTN_HEADS * LANES),
            row_blk(ATTN_HEADS * LANES),
            row_blk(ATTN_HEADS), row_blk(D_MODEL), row_blk(D_MODEL),
            pl.BlockSpec((SUBLANES, CONV_WIDTH), lambda b, t: (b * nt + t, 0)),
        ],
        out_shape=[
            out_rows(2 * ATTN_HEADS * LANES, bf16), out_rows(ATTN_HEADS * LANES, bf16),
            out_rows(ATTN_HEADS * LANES, bf16),
            out_rows(ATTN_HEADS, f32), out_rows(D_MODEL, bf16), out_rows(D_MODEL, bf16),
            jax.ShapeDtypeStruct((batch * nt * SUBLANES, CONV_WIDTH), f32),
        ],
        scratch_shapes=[pltpu.VMEM((tm + SUBLANES, CONV_WIDTH), f32),
                        pltpu.VMEM((1, LANES), f32)],
        compiler_params=pltpu.CompilerParams(
            dimension_semantics=("arbitrary", "arbitrary"), vmem_limit_bytes=VMEM_LIMIT_BYTES),
        name="mixer_input",
    )(h, g, w_qkv, w_mix, w_f, b_f, conv_w, w_conv_branch, p_init, f_init, *_bias_selectors())


def _attn_body(q_ref, k_hbm, v_hbm, km_ref, vm_ref, o_ref,
               m_scr, alpha_scr, acc_scr, s_scr, sm_scr, p_scr, pm_scr, k_ref, v_ref, kv_sems,
               *, tq, tk, rows, pairs):
    i = pl.program_id(2)

    seq = k_ref.shape[0]
    half = seq // 2

    def kv_copy(part, src, dst, sem):
        src_rows = pl.ds(pl.multiple_of(pl.program_id(0) * seq + part * half, half), half)
        return pltpu.make_async_copy(src.at[src_rows, :], dst.at[pl.ds(part * half, half), :],
                                     kv_sems.at[sem])

    copies = [[kv_copy(part, k_hbm, k_ref, 2 * part), kv_copy(part, v_hbm, v_ref, 2 * part + 1)]
              for part in range(2)]

    @pl.when(i == 0)
    def _():
        for part in copies:
            for copy in part:
                copy.start()
        for copy in copies[0]:
            copy.wait()

    @pl.when(i == half // tq)
    def _():
        for copy in copies[1]:
            copy.wait()

    low_half = lax.broadcasted_iota(jnp.int32, (tq, LANES), 1) < HEAD_DIM
    head_lanes = HEADS_PER_BLOCK * LANES
    m_scr[...] = jnp.full(m_scr.shape, MASK_VALUE, f32)
    acc_scr[...] = jnp.zeros(acc_scr.shape, f32)
    contract_last = (((1,), (1,)), ((), ()))
    lane = lax.broadcasted_iota(jnp.int32, (rows, LANES), 1)
    sub = lax.broadcasted_iota(jnp.int32, (rows, LANES), 0)
    n_heads = pairs * HEADS_PER_BLOCK
    assert n_heads % 2 == 0

    def scores(hh, keys_ref, key_rows):
        pair = hh // HEADS_PER_BLOCK
        q_head = q_ref[:, hh * head_lanes:(hh + 1) * head_lanes]
        k_pair = keys_ref[key_rows, pair * head_lanes:(pair + 1) * head_lanes]
        return lax.dot_general(q_head, k_pair, contract_last, preferred_element_type=f32)

    def softmax_rows(hh, r0, parts):
        rs = slice(r0, r0 + rows)
        blocks = [blk if valid is None else jnp.where(valid, blk, MASK_VALUE)
                  for blk, valid, _, _ in parts]
        m_old = m_scr[hh, rs, :]
        m_blk = functools.reduce(jnp.maximum, blocks)
        m_new = jnp.maximum(
            m_old, jnp.broadcast_to(jnp.max(m_blk, axis=-1, keepdims=True), (rows, LANES)))
        for blk, (_, _, dst, c0) in zip(blocks, parts):
            dst[hh % 2, rs, c0:c0 + LANES] = jnp.exp2(blk - m_new).astype(bf16)
        alpha_scr[hh, rs, :] = jnp.exp2(m_old - m_new)
        m_scr[hh, rs, :] = m_new

    def accumulate(hh, pv):
        acc_scr[hh] = alpha_scr[hh] * acc_scr[hh] + pv

    s_scr[0] = scores(0, k_ref, pl.ds(0, tk))

    def full_step(j, carry):
        start = pl.multiple_of(j * tk, tk)
        for hh in range(n_heads):
            slot = hh % 2
            if hh + 1 < n_heads:
                s_scr[1 - slot] = scores(hh + 1, k_ref, pl.ds(start, tk))
            else:
                s_scr[1 - slot] = scores(0, k_ref, pl.ds(pl.multiple_of(start + tk, tk), tk))
            for r0 in range(0, tq, rows):
                softmax_rows(hh, r0, [
                    (s_scr[slot, r0:r0 + rows, c * LANES:(c + 1) * LANES], None, p_scr, c * LANES)
                    for c in range(tk // LANES)])
            accumulate(hh, jnp.dot(p_scr[slot], v_ref[pl.ds(start, tk), hh * LANES:(hh + 1) * LANES],
                                   preferred_element_type=f32))
        return carry

    lax.fori_loop(0, lax.shift_right_logical(i, 1),
                  lambda jj, c: full_step(2 * jj + 1, full_step(2 * jj, c)), 0)

    @pl.when(jnp.bitwise_and(i, 1) == 1)
    def _():
        full_step(i - 1, 0)

    start = pl.multiple_of(i * tk, tk)
    meta_valid = lane < N_META
    sm_scr[0] = scores(0, km_ref, slice(None))
    for hh in range(n_heads):
        slot = hh % 2
        if hh + 1 < n_heads:
            s_scr[1 - slot] = scores(hh + 1, k_ref, pl.ds(start, tk))
            sm_scr[1 - slot] = scores(hh + 1, km_ref, slice(None))
        for r0 in range(0, tq, rows):
            n_cols = r0 + rows
            parts = [(s_scr[slot, r0:r0 + rows, c * LANES:(c + 1) * LANES],
                      None if (c + 1) * LANES - 1 <= r0 else (lane + c * LANES <= sub + r0),
                      p_scr, c * LANES) for c in range(n_cols // LANES)]
            parts.append((sm_scr[slot, r0:r0 + rows, :], meta_valid, pm_scr, 0))
            softmax_rows(hh, r0, parts)
            if n_cols < tk:
                p_scr[slot, r0:r0 + rows, n_cols:tk] = jnp.zeros((rows, tk - n_cols), bf16)
        accumulate(hh, jnp.dot(p_scr[slot], v_ref[pl.ds(start, tk), hh * LANES:(hh + 1) * LANES],
                               preferred_element_type=f32)
                   + jnp.dot(pm_scr[slot], vm_ref[:, hh * LANES:(hh + 1) * LANES],
                             preferred_element_type=f32))

    for pair in range(pairs):
        acc_a, acc_b = acc_scr[HEADS_PER_BLOCK * pair], acc_scr[HEADS_PER_BLOCK * pair + 1]
        out = jnp.where(low_half, acc_a / acc_a[:, HEAD_DIM:HEAD_DIM + 1], acc_b / acc_b[:, 0:1])
        o_ref[:, pair * LANES:(pair + 1) * LANES] = out.astype(bf16)


def _attention(qx, kx, v_aug, kx_meta, v_meta, batch, seq, tq, tk, rows, pairs):
    nq = seq // tq
    n_heads = pairs * HEADS_PER_BLOCK
    n_blk = ATTN_HEADS // n_heads
    pair_lanes = n_heads * LANES
    q_spec = pl.BlockSpec((tq, HEADS_PER_BLOCK * pair_lanes), lambda b, p, i: (b * nq + i, p))
    o_spec = pl.BlockSpec((tq, pairs * LANES), lambda b, p, i: (b * nq + i, p))
    assert n_blk == 1 and (seq // 2) % tq == 0
    kv_spec = pl.BlockSpec(memory_space=pl.ANY)
    meta_spec = pl.BlockSpec((LANES, pair_lanes), lambda b, p, i: (0, p))
    per_head = lambda width, dt: pltpu.VMEM((n_heads, tq, width), dt)
    ring = lambda width, dt: pltpu.VMEM((2, tq, width), dt)
    return pl.pallas_call(
        functools.partial(_attn_body, tq=tq, tk=tk, rows=rows, pairs=pairs),
        grid=(batch, n_blk, nq),
        in_specs=[q_spec, kv_spec, kv_spec, meta_spec, meta_spec],
        out_specs=o_spec,
        out_shape=jax.ShapeDtypeStruct((batch * seq, ATTN_WIDTH), bf16),
        scratch_shapes=[per_head(LANES, f32), per_head(LANES, f32), per_head(LANES, f32),
                        ring(tk, f32), ring(LANES, f32), ring(tk, bf16), ring(LANES, bf16),
                        pltpu.VMEM((seq, pair_lanes), bf16), pltpu.VMEM((seq, pair_lanes), bf16),
                        pltpu.SemaphoreType.DMA((4,))],
        compiler_params=pltpu.CompilerParams(
            dimension_semantics=("arbitrary", "arbitrary", "arbitrary"),
            vmem_limit_bytes=VMEM_LIMIT_BYTES),
        name="forgetting_attention",
    )(qx, kx, v_aug, kx_meta, v_meta)


def _mix_out_ffn_body(h_ref, attn_ref, sga_ref, gcv_ref, wa_ref, wo_ref, g_mix_ref,
                      g_pre_ref, g_post_ref, w_in_ref, w_out_ref, o_ref, act_ref):
    y_attn = jnp.dot(attn_ref[...], wa_ref[...], preferred_element_type=f32)
    gated = sga_ref[...].astype(f32) * y_attn + gcv_ref[...].astype(f32)
    mixed = jnp.dot(gated.astype(bf16), wo_ref[...], preferred_element_type=f32)
    h2 = h_ref[...] + _rms_norm(mixed, g_mix_ref[...])
    o_ref[...] = _swiglu_residual(h2, g_pre_ref[...], g_post_ref[...],
                                  w_in_ref, w_out_ref, act_ref)


def _mix_out_ffn(h, attn, sga, gcv, w_attn_branch, w_out, g_mix, g_pre, g_post, w_in, w_ffn_out, tm):
    rows = h.shape[0]
    row_blk = lambda width: pl.BlockSpec((tm, width), lambda i: (i, 0))
    return pl.pallas_call(
        _mix_out_ffn_body,
        grid=(rows // tm,),
        in_specs=[row_blk(D_MODEL), row_blk(ATTN_WIDTH), row_blk(D_MODEL), row_blk(D_MODEL),
                  _resident((ATTN_WIDTH, D_MODEL)), _resident((D_MODEL, D_MODEL)),
                  _resident((1, D_MODEL)), _resident((1, D_MODEL)), _resident((1, D_MODEL)),
                  _resident((D_MODEL, 2 * D_FF)), _resident((D_FF, D_MODEL))],
        out_specs=row_blk(D_MODEL),
        out_shape=jax.ShapeDtypeStruct((rows, D_MODEL), f32),
        scratch_shapes=[pltpu.VMEM((tm, D_FF), bf16)],
        compiler_params=pltpu.CompilerParams(
            dimension_semantics=("arbitrary",), vmem_limit_bytes=VMEM_LIMIT_BYTES),
        name="mixer_output_ffn",
    )(h, attn, sga, gcv, w_attn_branch, w_out, g_mix, g_pre, g_post, w_in, w_ffn_out)


def kernel(x, meta_tokens, w_in, b_forget, conv_w, w_attn_branch, w_conv_branch, w_out,
           g_ffn1_pre, g_ffn1_post, w_ffn1_in, w_ffn1_out, g_mix_pre, g_mix_post,
           g_ffn2_pre, g_ffn2_post, w_ffn2_in, w_ffn2_out):
    batch, seq, d = x.shape
    assert d == D_MODEL and w_in.shape[0] == 1 and meta_tokens.shape == (N_META, D_MODEL)
    assert seq % ROW_TILE == 0 and seq % FFN_TILE == 0 and seq % ATTN_TQ == 0
    assert ATTN_TQ == ATTN_TK and ATTN_TQ % ATTN_ROWS == 0 and ATTN_ROWS % LANES == 0

    gain = lambda g: g[0].reshape(1, D_MODEL).astype(f32)
    wt = w_in[0].T
    w_f = jnp.pad(wt[N_QKV:N_QKV + ATTN_HEADS].astype(bf16), ((0, LANES - ATTN_HEADS), (0, 0)))
    b_f = jnp.pad(b_forget[0].astype(f32), (0, LANES - ATTN_HEADS)).reshape(1, LANES)
    cw = conv_w[0].astype(f32)
    w1_in, w1_out = w_ffn1_in[0].astype(bf16), w_ffn1_out[0].astype(bf16)

    rows = batch * seq
    later_weights = (
        (wt, 0, N_QKV), (wt, N_QKV + ATTN_HEADS, N_MIX),
        (w_conv_branch[0], 0, CONV_WIDTH), (w_attn_branch[0], 0, ATTN_WIDTH),
        (w_out[0], 0, D_MODEL), (w_ffn2_in[0], 0, D_MODEL), (w_ffn2_out[0], 0, D_FF))
    h1, hm, w_qkv, w_mix, wc, wa, wo, w2_in, w2_out = _ffn(
        x.reshape(rows, D_MODEL), meta_tokens.astype(f32), gain(g_ffn1_pre), gain(g_ffn1_post),
        w1_in, w1_out, FFN_TILE, later_weights)

    zeros_p = jnp.zeros((SUBLANES, CONV_WIDTH), f32)
    zeros_f = jnp.zeros((1, LANES), f32)
    _, kxm, vm, fm, _, _, pm_tail = _mix_in(hm, gain(g_mix_pre), w_qkv, w_mix, w_f, b_f, cw, wc,
                                           zeros_p, zeros_f, 1, N_META)
    pad_rows = lambda a: jnp.pad(a, ((0, LANES - N_META), (0, 0)))
    f_init = jnp.pad(fm[N_META - 1:N_META, :], ((0, 0), (0, LANES - ATTN_HEADS)))

    qx, kx, v, _, sga, gcv, _ = _mix_in(h1, gain(g_mix_pre), w_qkv, w_mix, w_f, b_f, cw, wc,
                                        pm_tail, f_init, batch, ROW_TILE)
    attn = _attention(qx, kx, v, pad_rows(kxm), pad_rows(vm),
                      batch, seq, ATTN_TQ, ATTN_TK, ATTN_ROWS, ATTN_PAIRS)
    h3 = _mix_out_ffn(h1, attn, sga, gcv, wa, wo, gain(g_mix_post), gain(g_ffn2_pre),
                      gain(g_ffn2_post), w2_in, w2_out, ROW_TILE)
    return h3.reshape(batch, seq, D_MODEL)
```

```python
import functools

import jax
import jax.numpy as jnp
import numpy as np
from jax import lax
from jax.experimental import pallas as pl
from jax.experimental.pallas import tpu as pltpu

D_MODEL = 1024
D_FF = 2816
N_META = 16
ATTN_HEADS = 8
HEAD_DIM = 64
ATTN_WIDTH = ATTN_HEADS * HEAD_DIM
CONV_WIDTH = 512
CONV_K = 3
NORM_EPS = 1e-6

LANES = 128
SUBLANES = 8
HEADS_PER_BLOCK = LANES // HEAD_DIM
VMEM_LIMIT_BYTES = 56 * 1024 * 1024
MASK_VALUE = -1e30
LOG2E = 1.4426950408889634
N_SPLIT = 3

ROW_TILE = 512
FFN_TILE = 1024
FF_CHUNK = 256
ATTN_TQ = 512
ATTN_TK = 512
ATTN_ROWS = 128
ATTN_PAIRS = 4
KV_PARTS = 4

C_Q, C_K, C_V = 0, 512, 1024
C_CB, C_CC, C_CIN, C_GA, C_GC = 0, 512, 1024, 1536, 2560
N_QKV = 3 * ATTN_WIDTH
N_MIX = 3 * CONV_WIDTH + 2 * D_MODEL

bf16 = jnp.bfloat16
f32 = jnp.float32


def _rms_norm(x, g):
    ms = jnp.mean(x * x, axis=-1, keepdims=True)
    return x * lax.rsqrt(ms + NORM_EPS) * g


def _resident(shape):
    nd = len(shape)
    return pl.BlockSpec(shape, lambda *_: (0,) * nd, pipeline_mode=pl.Buffered(1))


def _swiglu_residual(h, g_pre, g_post, w_in_ref, w_out_ref, act_ref):
    u = _rms_norm(h, g_pre).astype(bf16)
    for c in range(D_FF // FF_CHUNK):
        lo = c * FF_CHUNK
        a = jnp.dot(u, w_in_ref[:, lo:lo + FF_CHUNK], preferred_element_type=f32)
        b = jnp.dot(u, w_in_ref[:, D_FF + lo:D_FF + lo + FF_CHUNK], preferred_element_type=f32)
        act_ref[:, lo:lo + FF_CHUNK] = (a * jax.nn.sigmoid(a) * b).astype(bf16)
    y = jnp.dot(act_ref[...], w_out_ref[...], preferred_element_type=f32)
    return h + 0.5 * _rms_norm(y, g_post)


def _ffn_body(h_ref, meta_ref, g_pre_ref, g_post_ref, w_in_ref, w_out_ref, *rest):
    n_cast = (len(rest) - 4) // 2
    o_ref, meta_o_ref = rest[n_cast], rest[n_cast + 1]
    act_ref, meta_act_ref = rest[-2], rest[-1]
    for src, dst in zip(rest[:n_cast], rest[n_cast + 2:-2]):
        dst[...] = src[...].astype(bf16)

    @pl.when(pl.program_id(0) == 0)
    def _():
        meta_o_ref[...] = _swiglu_residual(meta_ref[...], g_pre_ref[...], g_post_ref[...],
                                           w_in_ref, w_out_ref, meta_act_ref)

    o_ref[...] = _swiglu_residual(h_ref[...], g_pre_ref[...], g_post_ref[...],
                                  w_in_ref, w_out_ref, act_ref)


def _ffn(h, meta, g_pre, g_post, w_in, w_out, tm, cast_along):
    rows = h.shape[0]
    steps = rows // tm
    row_blk = pl.BlockSpec((tm, D_MODEL), lambda i: (i, 0))
    cast_in, cast_out = [], []
    for w, first, count in cast_along:
        per, width = count // steps, w.shape[1]
        assert count % steps == 0 and per % (2 * SUBLANES) == 0 and first % SUBLANES == 0
        cast_in.append(pl.BlockSpec(
            (pl.Element(per), pl.Element(width)),
            functools.partial(lambda f, p, i: (pl.multiple_of(f + i * p, SUBLANES), 0), first, per)))
        cast_out.append(pl.BlockSpec((per, width), lambda i: (i, 0)))
    meta_blk = pl.BlockSpec(meta.shape, lambda i: (0, 0))
    return pl.pallas_call(
        _ffn_body,
        grid=(steps,),
        in_specs=[
            row_blk,
            _resident(meta.shape),
            _resident((1, D_MODEL)),
            _resident((1, D_MODEL)),
            _resident((D_MODEL, 2 * D_FF)),
            _resident((D_FF, D_MODEL)),
        ] + cast_in,
        out_specs=[row_blk, meta_blk] + cast_out,
        out_shape=[jax.ShapeDtypeStruct((rows, D_MODEL), f32),
                   jax.ShapeDtypeStruct(meta.shape, f32)]
        + [jax.ShapeDtypeStruct((count, w.shape[1]), bf16) for w, _, count in cast_along],
        scratch_shapes=[pltpu.VMEM((tm, D_FF), bf16), pltpu.VMEM((meta.shape[0], D_FF), bf16)],
        compiler_params=pltpu.CompilerParams(
            dimension_semantics=("arbitrary",), vmem_limit_bytes=VMEM_LIMIT_BYTES),
        name="ffn_half_step",
    )(h, meta, g_pre, g_post, w_in, w_out, *[w for w, _, _ in cast_along])


def _log_sigmoid(x):
    return jnp.minimum(x, 0.0) - jnp.log1p(jnp.exp(-jnp.abs(x)))


def _split3(x):
    hi = x.astype(bf16)
    r = x - hi.astype(f32)
    mid = r.astype(bf16)
    lo = (r - mid.astype(f32)).astype(bf16)
    return hi, mid, lo


def _pack3(x, head_lanes):
    hi, mid, lo = _split3(jnp.where(head_lanes, x, 0.0))
    packed = (hi.astype(f32) + pltpu.roll(mid.astype(f32), ATTN_HEADS, 1)
              + pltpu.roll(lo.astype(f32), 2 * ATTN_HEADS, 1))
    return packed.astype(bf16)


def _mix_in_body(h_ref, g_ref, w_qkv_ref, w_mix_ref, w_f_ref, bf_ref, cw_ref, wc_ref,
                 p_init_ref, f_init_ref, ksel_ref, qsel_ref,
                 qx_ref, kx_ref, v_ref, fsum_ref, sga_ref, gcv_ref, p_tail_ref,
                 p_scr, f_carry, *, tm):
    @pl.when(pl.program_id(1) == 0)
    def _():
        p_scr[0:SUBLANES, :] = p_init_ref[...]
        f_carry[...] = f_init_ref[...]

    u = _rms_norm(h_ref[...], g_ref[...]).astype(bf16)

    def proj(wt_ref, lo, width):
        return lax.dot_general(u, wt_ref[lo:lo + width, :], (((1,), (1,)), ((), ())),
                               preferred_element_type=f32)

    rb = min(tm, LANES)
    n_blk = tm // rb
    row_blk = [slice(blk * rb, (blk + 1) * rb) for blk in range(n_blk)]
    n_pairs = ATTN_HEADS // HEADS_PER_BLOCK
    head_lanes = lax.broadcasted_iota(jnp.int32, (rb, LANES), 1) < ATTN_HEADS
    log_f = _log_sigmoid(proj(w_f_ref, 0, LANES) + bf_ref[...]) * LOG2E

    q = (proj(w_qkv_ref, C_Q, ATTN_WIDTH) * (LOG2E * HEAD_DIM ** -0.5)).astype(bf16)
    lane = lax.broadcasted_iota(jnp.int32, (tm, LANES), 1)
    for pair in range(n_pairs):
        q2 = q[:, pair * LANES:(pair + 1) * LANES]
        zeros = jnp.zeros_like(q2)
        c0 = 2 * pair * HEADS_PER_BLOCK * LANES
        qx_ref[:, c0:c0 + LANES] = jnp.where(lane < HEAD_DIM, q2, zeros)
        qx_ref[:, c0 + 2 * LANES:c0 + 3 * LANES] = jnp.where(lane < HEAD_DIM, zeros, q2)

    tri = (lax.broadcasted_iota(jnp.int32, (rb, rb), 0)
           >= lax.broadcasted_iota(jnp.int32, (rb, rb), 1)).astype(bf16)
    c3 = [jnp.dot(tri, _pack3(log_f[rs, :], head_lanes), preferred_element_type=f32)
          for rs in row_blk]

    k = proj(w_qkv_ref, C_K, ATTN_WIDTH)
    for pair in range(n_pairs):
        c0 = pair * HEADS_PER_BLOCK * LANES
        kx_ref[:, c0:c0 + LANES] = k[:, pair * LANES:(pair + 1) * LANES].astype(bf16)

    carry = f_carry[...]
    pieces = []
    for rs, c in zip(row_blk, c3):
        cs = c + pltpu.roll(c, LANES - ATTN_HEADS, 1) + pltpu.roll(c, LANES - 2 * ATTN_HEADS, 1)
        cs = jnp.where(head_lanes, cs, 0.0) + carry
        carry = cs[rb - 1:rb, :]
        fsum_ref[rs, :] = cs[:, :ATTN_HEADS]
        pieces.append(_pack3(cs, head_lanes))
    f_carry[...] = carry

    v = proj(w_qkv_ref, C_V, ATTN_WIDTH)
    ones_a = jnp.where(lane == HEAD_DIM, 1.0, 0.0)
    ones_b = jnp.where(lane == 0, 1.0, 0.0)
    for pair in range(n_pairs):
        v2 = v[:, pair * LANES:(pair + 1) * LANES]
        c0 = pair * HEADS_PER_BLOCK * LANES
        v_ref[:, c0:c0 + LANES] = jnp.where(lane < HEAD_DIM, v2, ones_a).astype(bf16)
        v_ref[:, c0 + LANES:c0 + 2 * LANES] = jnp.where(lane < HEAD_DIM, ones_b, v2).astype(bf16)

    k_lane = lax.broadcasted_iota(jnp.int32, (rb, n_pairs * LANES), 1)
    k_ones = jnp.bitwise_and(k_lane, LANES - 1) < N_SPLIT
    q_lane = lax.broadcasted_iota(jnp.int32, (rb, ATTN_HEADS * LANES), 1)
    q_head = jnp.right_shift(q_lane, LANES.bit_length() - 1)
    q_ones_lo = N_SPLIT * (1 + jnp.bitwise_and(q_head, HEADS_PER_BLOCK - 1))
    q_in_blk = jnp.bitwise_and(q_lane, LANES - 1)
    q_ones = (q_in_blk >= q_ones_lo) & (q_in_blk < q_ones_lo + N_SPLIT)
    for rs, pc in zip(row_blk, pieces):
        k_bias = jnp.dot(pc, ksel_ref[...], preferred_element_type=f32)
        k_bias = jnp.where(k_ones, 1.0, k_bias).astype(bf16)
        for pair in range(n_pairs):
            c0 = pair * HEADS_PER_BLOCK * LANES + LANES
            kx_ref[rs, c0:c0 + LANES] = k_bias[:, pair * LANES:(pair + 1) * LANES]
        q_bias = jnp.dot(pc, qsel_ref[...], preferred_element_type=f32)
        q_bias = jnp.where(q_ones, 1.0, q_bias).astype(bf16)
        for head in range(ATTN_HEADS):
            c0 = head * HEADS_PER_BLOCK * LANES + LANES
            qx_ref[rs, c0:c0 + LANES] = q_bias[:, head * LANES:(head + 1) * LANES]

    p = proj(w_mix_ref, C_CC, CONV_WIDTH) * proj(w_mix_ref, C_CIN, CONV_WIDTH)
    p_scr[SUBLANES:SUBLANES + tm, :] = p
    conv = (p_scr[SUBLANES - 2:SUBLANES - 2 + tm, :] * cw_ref[0:1, :]
            + p_scr[SUBLANES - 1:SUBLANES - 1 + tm, :] * cw_ref[1:2, :]
            + p * cw_ref[2:3, :])
    tail = p_scr[tm:tm + SUBLANES, :]
    p_scr[0:SUBLANES, :] = tail
    p_tail_ref[...] = tail
    conv_in = (proj(w_mix_ref, C_CB, CONV_WIDTH) * conv).astype(bf16)
    y_conv = jnp.dot(conv_in, wc_ref[...], preferred_element_type=f32)
    sga_ref[...] = jax.nn.sigmoid(proj(w_mix_ref, C_GA, D_MODEL)).astype(bf16)
    gcv_ref[...] = (jax.nn.sigmoid(proj(w_mix_ref, C_GC, D_MODEL)) * y_conv).astype(bf16)


def _bias_selectors():
    ksel = np.zeros((LANES, ATTN_HEADS // HEADS_PER_BLOCK * LANES), np.float32)
    qsel = np.zeros((LANES, ATTN_HEADS * LANES), np.float32)
    for h in range(ATTN_HEADS):
        for t in range(N_SPLIT):
            col = (h // HEADS_PER_BLOCK) * LANES + N_SPLIT * (1 + h % HEADS_PER_BLOCK) + t
            ksel[t * ATTN_HEADS + h, col] = -1.0
            qsel[t * ATTN_HEADS + h, h * LANES + t] = 1.0
    return jnp.asarray(ksel, bf16), jnp.asarray(qsel, bf16)


def _mix_in(h, g, w_qkv, w_mix, w_f, b_f, conv_w, w_conv_branch, p_init, f_init, batch, tm):
    rows = h.shape[0]
    nt = rows // (batch * tm)
    row_blk = lambda width: pl.BlockSpec((tm, width), lambda b, t: (b * nt + t, 0))
    out_rows = lambda width, dt: jax.ShapeDtypeStruct((rows, width), dt)
    return pl.pallas_call(
        functools.partial(_mix_in_body, tm=tm),
        grid=(batch, nt),
        in_specs=[
            row_blk(D_MODEL),
            _resident((1, D_MODEL)),
            _resident((N_QKV, D_MODEL)),
            _resident((N_MIX, D_MODEL)),
            _resident((LANES, D_MODEL)),
            _resident((1, LANES)),
            _resident((CONV_K, CONV_WIDTH)),
            _resident((CONV_WIDTH, D_MODEL)),
            _resident((SUBLANES, CONV_WIDTH)),
            _resident((1, LANES)),
            _resident((LANES, ATTN_HEADS // HEADS_PER_BLOCK * LANES)),
            _resident((LANES, ATTN_HEADS * LANES)),
        ],
        out_specs=[
            row_blk(2 * ATTN_HEADS * LANES), row_blk(ATTN_HEADS * LANES),
            row_blk(ATTN_HEADS * LANES),
            row_blk(ATTN_HEADS), row_blk(D_MODEL), row_blk(D_MODEL),
            pl.BlockSpec((SUBLANES, CONV_WIDTH), lambda b, t: (b * nt + t, 0)),
        ],
        out_shape=[
            out_rows(2 * ATTN_HEADS * LANES, bf16), out_rows(ATTN_HEADS * LANES, bf16),
            out_rows(ATTN_HEADS * LANES, bf16),
            out_rows(ATTN_HEADS, f32), out_rows(D_MODEL, bf16), out_rows(D_MODEL, bf16),
            jax.ShapeDtypeStruct((batch * nt * SUBLANES, CONV_WIDTH), f32),
        ],
        scratch_shapes=[pltpu.VMEM((tm + SUBLANES, CONV_WIDTH), f32),
                        pltpu.VMEM((1, LANES), f32)],
        compiler_params=pltpu.CompilerParams(
            dimension_semantics=("arbitrary", "arbitrary"), vmem_limit_bytes=VMEM_LIMIT_BYTES),
        name="mixer_input",
    )(h, g, w_qkv, w_mix, w_f, b_f, conv_w, w_conv_branch, p_init, f_init, *_bias_selectors())


def _attn_body(q_ref, k_hbm, v_hbm, km_ref, vm_ref, o_ref,
               m_scr, alpha_scr, acc_scr, s_scr, sm_scr, p_scr, pm_scr, k_ref, v_ref, kv_sems,
               *, tq, tk, rows, pairs):
    i = pl.program_id(2)

    seq = k_ref.shape[0]
    part_rows = seq // KV_PARTS

    def kv_copy(part, src, dst, sem):
        src_rows = pl.ds(pl.multiple_of(pl.program_id(0) * seq + part * part_rows, part_rows),
                         part_rows)
        return pltpu.make_async_copy(src.at[src_rows, :],
                                     dst.at[pl.ds(part * part_rows, part_rows), :],
                                     kv_sems.at[sem])

    copies = [[kv_copy(part, k_hbm, k_ref, 2 * part), kv_copy(part, v_hbm, v_ref, 2 * part + 1)]
              for part in range(KV_PARTS)]

    @pl.when(i == 0)
    def _():
        for part in copies:
            for copy in part:
                copy.start()

    for part in range(KV_PARTS):
        @pl.when(i == part * (part_rows // tq))
        def _(part=part):
            for copy in copies[part]:
                copy.wait()

    low_half = lax.broadcasted_iota(jnp.int32, (tq, LANES), 1) < HEAD_DIM
    head_lanes = HEADS_PER_BLOCK * LANES
    m_scr[...] = jnp.full(m_scr.shape, MASK_VALUE, f32)
    acc_scr[...] = jnp.zeros(acc_scr.shape, f32)
    contract_last = (((1,), (1,)), ((), ()))
    lane = lax.broadcasted_iota(jnp.int32, (rows, LANES), 1)
    sub = lax.broadcasted_iota(jnp.int32, (rows, LANES), 0)
    n_heads = pairs * HEADS_PER_BLOCK
    assert n_heads % 2 == 0

    def scores(hh, keys_ref, key_rows):
        pair = hh // HEADS_PER_BLOCK
        q_head = q_ref[:, hh * head_lanes:(hh + 1) * head_lanes]
        k_pair = keys_ref[key_rows, pair * head_lanes:(pair + 1) * head_lanes]
        return lax.dot_general(q_head, k_pair, contract_last, preferred_element_type=f32)

    def softmax_rows(hh, r0, parts):
        rs = slice(r0, r0 + rows)
        blocks = [blk if valid is None else jnp.where(valid, blk, MASK_VALUE)
                  for blk, valid, _, _ in parts]
        m_old = m_scr[hh, rs, :]
        m_blk = functools.reduce(jnp.maximum, blocks)
        m_new = jnp.maximum(
            m_old, jnp.broadcast_to(jnp.max(m_blk, axis=-1, keepdims=True), (rows, LANES)))
        for blk, (_, _, dst, c0) in zip(blocks, parts):
            dst[hh % 2, rs, c0:c0 + LANES] = jnp.exp2(blk - m_new).astype(bf16)
        alpha_scr[hh, rs, :] = jnp.exp2(m_old - m_new)
        m_scr[hh, rs, :] = m_new

    def accumulate(hh, pv):
        acc_scr[hh] = alpha_scr[hh] * acc_scr[hh] + pv

    s_scr[0] = scores(0, k_ref, pl.ds(0, tk))

    def full_step(j, carry):
        start = pl.multiple_of(j * tk, tk)
        for hh in range(n_heads):
            slot = hh % 2
            if hh + 1 < n_heads:
                s_scr[1 - slot] = scores(hh + 1, k_ref, pl.ds(start, tk))
            else:
                s_scr[1 - slot] = scores(0, k_ref, pl.ds(pl.multiple_of(start + tk, tk), tk))
            for r0 in range(0, tq, rows):
                softmax_rows(hh, r0, [
                    (s_scr[slot, r0:r0 + rows, c * LANES:(c + 1) * LANES], None, p_scr, c * LANES)
                    for c in range(tk // LANES)])
            accumulate(hh, jnp.dot(p_scr[slot], v_ref[pl.ds(start, tk), hh * LANES:(hh + 1) * LANES],
                                   preferred_element_type=f32))
        return carry

    lax.fori_loop(0, lax.shift_right_logical(i, 1),
                  lambda jj, c: full_step(2 * jj + 1, full_step(2 * jj, c)), 0)

    @pl.when(jnp.bitwise_and(i, 1) == 1)
    def _():
        full_step(i - 1, 0)

    start = pl.multiple_of(i * tk, tk)
    meta_valid = lane < N_META
    sm_scr[0] = scores(0, km_ref, slice(None))
    for hh in range(n_heads):
        slot = hh % 2
        if hh + 1 < n_heads:
            s_scr[1 - slot] = scores(hh + 1, k_ref, pl.ds(start, tk))
            sm_scr[1 - slot] = scores(hh + 1, km_ref, slice(None))
        for r0 in range(0, tq, rows):
            n_cols = r0 + rows
            parts = [(s_scr[slot, r0:r0 + rows, c * LANES:(c + 1) * LANES],
                      None if (c + 1) * LANES - 1 <= r0 else (lane + c * LANES <= sub + r0),
                      p_scr, c * LANES) for c in range(n_cols // LANES)]
            parts.append((sm_scr[slot, r0:r0 + rows, :], meta_valid, pm_scr, 0))
            softmax_rows(hh, r0, parts)
            if n_cols < tk:
                p_scr[slot, r0:r0 + rows, n_cols:tk] = jnp.zeros((rows, tk - n_cols), bf16)
        accumulate(hh, jnp.dot(p_scr[slot], v_ref[pl.ds(start, tk), hh * LANES:(hh + 1) * LANES],
                               preferred_element_type=f32)
                   + jnp.dot(pm_scr[slot], vm_ref[:, hh * LANES:(hh + 1) * LANES],
                             preferred_element_type=f32))

    for pair in range(pairs):
        acc_a, acc_b = acc_scr[HEADS_PER_BLOCK * pair], acc_scr[HEADS_PER_BLOCK * pair + 1]
        out = jnp.where(low_half, acc_a / acc_a[:, HEAD_DIM:HEAD_DIM + 1], acc_b / acc_b[:, 0:1])
        o_ref[:, pair * LANES:(pair + 1) * LANES] = out.astype(bf16)


def _attention(qx, kx, v_aug, kx_meta, v_meta, batch, seq, tq, tk, rows, pairs):
    nq = seq // tq
    n_heads = pairs * HEADS_PER_BLOCK
    n_blk = ATTN_HEADS // n_heads
    pair_lanes = n_heads * LANES
    q_spec = pl.BlockSpec((tq, HEADS_PER_BLOCK * pair_lanes), lambda b, p, i: (b * nq + i, p))
    o_spec = pl.BlockSpec((tq, pairs * LANES), lambda b, p, i: (b * nq + i, p))
    assert n_blk == 1 and seq % (KV_PARTS * tq) == 0
    kv_spec = pl.BlockSpec(memory_space=pl.ANY)
    meta_spec = pl.BlockSpec((LANES, pair_lanes), lambda b, p, i: (0, p))
    per_head = lambda width, dt: pltpu.VMEM((n_heads, tq, width), dt)
    ring = lambda width, dt: pltpu.VMEM((2, tq, width), dt)
    return pl.pallas_call(
        functools.partial(_attn_body, tq=tq, tk=tk, rows=rows, pairs=pairs),
        grid=(batch, n_blk, nq),
        in_specs=[q_spec, kv_spec, kv_spec, meta_spec, meta_spec],
        out_specs=o_spec,
        out_shape=jax.ShapeDtypeStruct((batch * seq, ATTN_WIDTH), bf16),
        scratch_shapes=[per_head(LANES, f32), per_head(LANES, f32), per_head(LANES, f32),
                        ring(tk, f32), ring(LANES, f32), ring(tk, bf16), ring(LANES, bf16),
                        pltpu.VMEM((seq, pair_lanes), bf16), pltpu.VMEM((seq, pair_lanes), bf16),
                        pltpu.SemaphoreType.DMA((2 * KV_PARTS,))],
        compiler_params=pltpu.CompilerParams(
            dimension_semantics=("arbitrary", "arbitrary", "arbitrary"),
            vmem_limit_bytes=VMEM_LIMIT_BYTES),
        name="forgetting_attention",
    )(qx, kx, v_aug, kx_meta, v_meta)


def _mix_out_ffn_body(h_ref, attn_ref, sga_ref, gcv_ref, wa_ref, wo_ref, g_mix_ref,
                      g_pre_ref, g_post_ref, w_in_ref, w_out_ref, o_ref, act_ref):
    y_attn = jnp.dot(attn_ref[...], wa_ref[...], preferred_element_type=f32)
    gated = sga_ref[...].astype(f32) * y_attn + gcv_ref[...].astype(f32)
    mixed = jnp.dot(gated.astype(bf16), wo_ref[...], preferred_element_type=f32)
    h2 = h_ref[...] + _rms_norm(mixed, g_mix_ref[...])
    o_ref[...] = _swiglu_residual(h2, g_pre_ref[...], g_post_ref[...],
                                  w_in_ref, w_out_ref, act_ref)


def _mix_out_ffn(h, attn, sga, gcv, w_attn_branch, w_out, g_mix, g_pre, g_post, w_in, w_ffn_out, tm):
    rows = h.shape[0]
    row_blk = lambda width: pl.BlockSpec((tm, width), lambda i: (i, 0))
    return pl.pallas_call(
        _mix_out_ffn_body,
        grid=(rows // tm,),
        in_specs=[row_blk(D_MODEL), row_blk(ATTN_WIDTH), row_blk(D_MODEL), row_blk(D_MODEL),
                  _resident((ATTN_WIDTH, D_MODEL)), _resident((D_MODEL, D_MODEL)),
                  _resident((1, D_MODEL)), _resident((1, D_MODEL)), _resident((1, D_MODEL)),
                  _resident((D_MODEL, 2 * D_FF)), _resident((D_FF, D_MODEL))],
        out_specs=row_blk(D_MODEL),
        out_shape=jax.ShapeDtypeStruct((rows, D_MODEL), f32),
        scratch_shapes=[pltpu.VMEM((tm, D_FF), bf16)],
        compiler_params=pltpu.CompilerParams(
            dimension_semantics=("arbitrary",), vmem_limit_bytes=VMEM_LIMIT_BYTES),
        name="mixer_output_ffn",
    )(h, attn, sga, gcv, w_attn_branch, w_out, g_mix, g_pre, g_post, w_in, w_ffn_out)


def kernel(x, meta_tokens, w_in, b_forget, conv_w, w_attn_branch, w_conv_branch, w_out,
           g_ffn1_pre, g_ffn1_post, w_ffn1_in, w_ffn1_out, g_mix_pre, g_mix_post,
           g_ffn2_pre, g_ffn2_post, w_ffn2_in, w_ffn2_out):
    batch, seq, d = x.shape
    assert d == D_MODEL and w_in.shape[0] == 1 and meta_tokens.shape == (N_META, D_MODEL)
    assert seq % ROW_TILE == 0 and seq % FFN_TILE == 0 and seq % ATTN_TQ == 0
    assert ATTN_TQ == ATTN_TK and ATTN_TQ % ATTN_ROWS == 0 and ATTN_ROWS % LANES == 0

    gain = lambda g: g[0].reshape(1, D_MODEL).astype(f32)
    wt = w_in[0].T
    w_f = jnp.pad(wt[N_QKV:N_QKV + ATTN_HEADS].astype(bf16), ((0, LANES - ATTN_HEADS), (0, 0)))
    b_f = jnp.pad(b_forget[0].astype(f32), (0, LANES - ATTN_HEADS)).reshape(1, LANES)
    cw = conv_w[0].astype(f32)
    w1_in, w1_out = w_ffn1_in[0].astype(bf16), w_ffn1_out[0].astype(bf16)

    rows = batch * seq
    later_weights = (
        (wt, 0, N_QKV), (wt, N_QKV + ATTN_HEADS, N_MIX),
        (w_conv_branch[0], 0, CONV_WIDTH), (w_attn_branch[0], 0, ATTN_WIDTH),
        (w_out[0], 0, D_MODEL), (w_ffn2_in[0], 0, D_MODEL), (w_ffn2_out[0], 0, D_FF))
    h1, hm, w_qkv, w_mix, wc, wa, wo, w2_in, w2_out = _ffn(
        x.reshape(rows, D_MODEL), meta_tokens.astype(f32), gain(g_ffn1_pre), gain(g_ffn1_post),
        w1_in, w1_out, FFN_TILE, later_weights)

    zeros_p = jnp.zeros((SUBLANES, CONV_WIDTH), f32)
    zeros_f = jnp.zeros((1, LANES), f32)
    _, kxm, vm, fm, _, _, pm_tail = _mix_in(hm, gain(g_mix_pre), w_qkv, w_mix, w_f, b_f, cw, wc,
                                           zeros_p, zeros_f, 1, N_META)
    pad_rows = lambda a: jnp.pad(a, ((0, LANES - N_META), (0, 0)))
    f_init = jnp.pad(fm[N_META - 1:N_META, :], ((0, 0), (0, LANES - ATTN_HEADS)))

    qx, kx, v, _, sga, gcv, _ = _mix_in(h1, gain(g_mix_pre), w_qkv, w_mix, w_f, b_f, cw, wc,
                                        pm_tail, f_init, batch, ROW_TILE)
    attn = _attention(qx, kx, v, pad_rows(kxm), pad_rows(vm),
                      batch, seq, ATTN_TQ, ATTN_TK, ATTN_ROWS, ATTN_PAIRS)
    h3 = _mix_out_ffn(h1, attn, sga, gcv, wa, wo, gain(g_mix_post), gain(g_ffn2_pre),
                      gain(g_ffn2_post), w2_in, w2_out, ROW_TILE)
    return h3.reshape(batch, seq, D_MODEL)
```

```python
import functools

import jax
import jax.numpy as jnp
import numpy as np
from jax import lax
from jax.experimental import pallas as pl
from jax.experimental.pallas import tpu as pltpu

D_MODEL = 1024
D_FF = 2816
N_META = 16
ATTN_HEADS = 8
HEAD_DIM = 64
ATTN_WIDTH = ATTN_HEADS * HEAD_DIM
CONV_WIDTH = 512
CONV_K = 3
NORM_EPS = 1e-6

LANES = 128
SUBLANES = 8
HEADS_PER_BLOCK = LANES // HEAD_DIM
VMEM_LIMIT_BYTES = 56 * 1024 * 1024
MASK_VALUE = -1e30
LOG2E = 1.4426950408889634
N_SPLIT = 3

ROW_TILE = 512
FFN_TILE = 1024
FF_CHUNK = 256
ATTN_TQ = 512
ATTN_TK = 512
ATTN_ROWS = 128
ATTN_PAIRS = 4
KV_PARTS = 16

C_Q, C_K, C_V = 0, 512, 1024
C_CB, C_CC, C_CIN, C_GA, C_GC = 0, 512, 1024, 1536, 2560
N_QKV = 3 * ATTN_WIDTH
N_MIX = 3 * CONV_WIDTH + 2 * D_MODEL

bf16 = jnp.bfloat16
f32 = jnp.float32


def _rms_norm(x, g):
    ms = jnp.mean(x * x, axis=-1, keepdims=True)
    return x * lax.rsqrt(ms + NORM_EPS) * g


def _resident(shape):
    nd = len(shape)
    return pl.BlockSpec(shape, lambda *_: (0,) * nd, pipeline_mode=pl.Buffered(1))


def _swiglu_residual(h, g_pre, g_post, w_in_ref, w_out_ref, act_ref):
    u = _rms_norm(h, g_pre).astype(bf16)
    for c in range(D_FF // FF_CHUNK):
        lo = c * FF_CHUNK
        a = jnp.dot(u, w_in_ref[:, lo:lo + FF_CHUNK], preferred_element_type=f32)
        b = jnp.dot(u, w_in_ref[:, D_FF + lo:D_FF + lo + FF_CHUNK], preferred_element_type=f32)
        act_ref[:, lo:lo + FF_CHUNK] = (a * jax.nn.sigmoid(a) * b).astype(bf16)
    y = jnp.dot(act_ref[...], w_out_ref[...], preferred_element_type=f32)
    return h + 0.5 * _rms_norm(y, g_post)


def _ffn_body(h_ref, meta_ref, g_pre_ref, g_post_ref, w_in_ref, w_out_ref, *rest):
    n_cast = (len(rest) - 4) // 2
    o_ref, meta_o_ref = rest[n_cast], rest[n_cast + 1]
    act_ref, meta_act_ref = rest[-2], rest[-1]
    for src, dst in zip(rest[:n_cast], rest[n_cast + 2:-2]):
        dst[...] = src[...].astype(bf16)

    @pl.when(pl.program_id(0) == 0)
    def _():
        meta_o_ref[...] = _swiglu_residual(meta_ref[...], g_pre_ref[...], g_post_ref[...],
                                           w_in_ref, w_out_ref, meta_act_ref)

    o_ref[...] = _swiglu_residual(h_ref[...], g_pre_ref[...], g_post_ref[...],
                                  w_in_ref, w_out_ref, act_ref)


def _ffn(h, meta, g_pre, g_post, w_in, w_out, tm, cast_along):
    rows = h.shape[0]
    steps = rows // tm
    row_blk = pl.BlockSpec((tm, D_MODEL), lambda i: (i, 0))
    cast_in, cast_out = [], []
    for w, first, count in cast_along:
        per, width = count // steps, w.shape[1]
        assert count % steps == 0 and per % (2 * SUBLANES) == 0 and first % SUBLANES == 0
        cast_in.append(pl.BlockSpec(
            (pl.Element(per), pl.Element(width)),
            functools.partial(lambda f, p, i: (pl.multiple_of(f + i * p, SUBLANES), 0), first, per)))
        cast_out.append(pl.BlockSpec((per, width), lambda i: (i, 0)))
    meta_blk = pl.BlockSpec(meta.shape, lambda i: (0, 0))
    return pl.pallas_call(
        _ffn_body,
        grid=(steps,),
        in_specs=[
            row_blk,
            _resident(meta.shape),
            _resident((1, D_MODEL)),
            _resident((1, D_MODEL)),
            _resident((D_MODEL, 2 * D_FF)),
            _resident((D_FF, D_MODEL)),
        ] + cast_in,
        out_specs=[row_blk, meta_blk] + cast_out,
        out_shape=[jax.ShapeDtypeStruct((rows, D_MODEL), f32),
                   jax.ShapeDtypeStruct(meta.shape, f32)]
        + [jax.ShapeDtypeStruct((count, w.shape[1]), bf16) for w, _, count in cast_along],
        scratch_shapes=[pltpu.VMEM((tm, D_FF), bf16), pltpu.VMEM((meta.shape[0], D_FF), bf16)],
        compiler_params=pltpu.CompilerParams(
            dimension_semantics=("arbitrary",), vmem_limit_bytes=VMEM_LIMIT_BYTES),
        name="ffn_half_step",
    )(h, meta, g_pre, g_post, w_in, w_out, *[w for w, _, _ in cast_along])


def _log_sigmoid(x):
    return jnp.minimum(x, 0.0) - jnp.log1p(jnp.exp(-jnp.abs(x)))


def _split3(x):
    hi = x.astype(bf16)
    r = x - hi.astype(f32)
    mid = r.astype(bf16)
    lo = (r - mid.astype(f32)).astype(bf16)
    return hi, mid, lo


def _pack3(x, head_lanes):
    hi, mid, lo = _split3(jnp.where(head_lanes, x, 0.0))
    packed = (hi.astype(f32) + pltpu.roll(mid.astype(f32), ATTN_HEADS, 1)
              + pltpu.roll(lo.astype(f32), 2 * ATTN_HEADS, 1))
    return packed.astype(bf16)


def _mix_in_body(h_ref, g_ref, w_qkv_ref, w_mix_ref, w_f_ref, bf_ref, cw_ref, wc_ref,
                 p_init_ref, f_init_ref, ksel_ref, qsel_ref,
                 qx_ref, kx_ref, v_ref, fsum_ref, sga_ref, gcv_ref, p_tail_ref,
                 p_scr, f_carry, *, tm):
    @pl.when(pl.program_id(1) == 0)
    def _():
        p_scr[0:SUBLANES, :] = p_init_ref[...]
        f_carry[...] = f_init_ref[...]

    u = _rms_norm(h_ref[...], g_ref[...]).astype(bf16)

    def proj(wt_ref, lo, width):
        return lax.dot_general(u, wt_ref[lo:lo + width, :], (((1,), (1,)), ((), ())),
                               preferred_element_type=f32)

    rb = min(tm, LANES)
    n_blk = tm // rb
    row_blk = [slice(blk * rb, (blk + 1) * rb) for blk in range(n_blk)]
    n_pairs = ATTN_HEADS // HEADS_PER_BLOCK
    head_lanes = lax.broadcasted_iota(jnp.int32, (rb, LANES), 1) < ATTN_HEADS
    log_f = _log_sigmoid(proj(w_f_ref, 0, LANES) + bf_ref[...]) * LOG2E

    q = (proj(w_qkv_ref, C_Q, ATTN_WIDTH) * (LOG2E * HEAD_DIM ** -0.5)).astype(bf16)
    lane = lax.broadcasted_iota(jnp.int32, (tm, LANES), 1)
    for pair in range(n_pairs):
        q2 = q[:, pair * LANES:(pair + 1) * LANES]
        zeros = jnp.zeros_like(q2)
        c0 = 2 * pair * HEADS_PER_BLOCK * LANES
        qx_ref[:, c0:c0 + LANES] = jnp.where(lane < HEAD_DIM, q2, zeros)
        qx_ref[:, c0 + 2 * LANES:c0 + 3 * LANES] = jnp.where(lane < HEAD_DIM, zeros, q2)

    tri = (lax.broadcasted_iota(jnp.int32, (rb, rb), 0)
           >= lax.broadcasted_iota(jnp.int32, (rb, rb), 1)).astype(bf16)
    c3 = [jnp.dot(tri, _pack3(log_f[rs, :], head_lanes), preferred_element_type=f32)
          for rs in row_blk]

    k = proj(w_qkv_ref, C_K, ATTN_WIDTH)
    for pair in range(n_pairs):
        c0 = pair * HEADS_PER_BLOCK * LANES
        kx_ref[:, c0:c0 + LANES] = k[:, pair * LANES:(pair + 1) * LANES].astype(bf16)

    carry = f_carry[...]
    pieces = []
    for rs, c in zip(row_blk, c3):
        cs = c + pltpu.roll(c, LANES - ATTN_HEADS, 1) + pltpu.roll(c, LANES - 2 * ATTN_HEADS, 1)
        cs = jnp.where(head_lanes, cs, 0.0) + carry
        carry = cs[rb - 1:rb, :]
        fsum_ref[rs, :] = cs[:, :ATTN_HEADS]
        pieces.append(_pack3(cs, head_lanes))
    f_carry[...] = carry

    v = proj(w_qkv_ref, C_V, ATTN_WIDTH)
    ones_a = jnp.where(lane == HEAD_DIM, 1.0, 0.0)
    ones_b = jnp.where(lane == 0, 1.0, 0.0)
    for pair in range(n_pairs):
        v2 = v[:, pair * LANES:(pair + 1) * LANES]
        c0 = pair * HEADS_PER_BLOCK * LANES
        v_ref[:, c0:c0 + LANES] = jnp.where(lane < HEAD_DIM, v2, ones_a).astype(bf16)
        v_ref[:, c0 + LANES:c0 + 2 * LANES] = jnp.where(lane < HEAD_DIM, ones_b, v2).astype(bf16)

    k_lane = lax.broadcasted_iota(jnp.int32, (rb, n_pairs * LANES), 1)
    k_ones = jnp.bitwise_and(k_lane, LANES - 1) < N_SPLIT
    q_lane = lax.broadcasted_iota(jnp.int32, (rb, ATTN_HEADS * LANES), 1)
    q_head = jnp.right_shift(q_lane, LANES.bit_length() - 1)
    q_ones_lo = N_SPLIT * (1 + jnp.bitwise_and(q_head, HEADS_PER_BLOCK - 1))
    q_in_blk = jnp.bitwise_and(q_lane, LANES - 1)
    q_ones = (q_in_blk >= q_ones_lo) & (q_in_blk < q_ones_lo + N_SPLIT)
    for rs, pc in zip(row_blk, pieces):
        k_bias = jnp.dot(pc, ksel_ref[...], preferred_element_type=f32)
        k_bias = jnp.where(k_ones, 1.0, k_bias).astype(bf16)
        for pair in range(n_pairs):
            c0 = pair * HEADS_PER_BLOCK * LANES + LANES
            kx_ref[rs, c0:c0 + LANES] = k_bias[:, pair * LANES:(pair + 1) * LANES]
        q_bias = jnp.dot(pc, qsel_ref[...], preferred_element_type=f32)
        q_bias = jnp.where(q_ones, 1.0, q_bias).astype(bf16)
        for head in range(ATTN_HEADS):
            c0 = head * HEADS_PER_BLOCK * LANES + LANES
            qx_ref[rs, c0:c0 + LANES] = q_bias[:, head * LANES:(head + 1) * LANES]

    p = proj(w_mix_ref, C_CC, CONV_WIDTH) * proj(w_mix_ref, C_CIN, CONV_WIDTH)
    p_scr[SUBLANES:SUBLANES + tm, :] = p
    conv = (p_scr[SUBLANES - 2:SUBLANES - 2 + tm, :] * cw_ref[0:1, :]
            + p_scr[SUBLANES - 1:SUBLANES - 1 + tm, :] * cw_ref[1:2, :]
            + p * cw_ref[2:3, :])
    tail = p_scr[tm:tm + SUBLANES, :]
    p_scr[0:SUBLANES, :] = tail
    p_tail_ref[...] = tail
    conv_in = (proj(w_mix_ref, C_CB, CONV_WIDTH) * conv).astype(bf16)
    y_conv = jnp.dot(conv_in, wc_ref[...], preferred_element_type=f32)
    sga_ref[...] = jax.nn.sigmoid(proj(w_mix_ref, C_GA, D_MODEL)).astype(bf16)
    gcv_ref[...] = (jax.nn.sigmoid(proj(w_mix_ref, C_GC, D_MODEL)) * y_conv).astype(bf16)


def _bias_selectors():
    ksel = np.zeros((LANES, ATTN_HEADS // HEADS_PER_BLOCK * LANES), np.float32)
    qsel = np.zeros((LANES, ATTN_HEADS * LANES), np.float32)
    for h in range(ATTN_HEADS):
        for t in range(N_SPLIT):
            col = (h // HEADS_PER_BLOCK) * LANES + N_SPLIT * (1 + h % HEADS_PER_BLOCK) + t
            ksel[t * ATTN_HEADS + h, col] = -1.0
            qsel[t * ATTN_HEADS + h, h * LANES + t] = 1.0
    return jnp.asarray(ksel, bf16), jnp.asarray(qsel, bf16)


def _mix_in(h, g, w_qkv, w_mix, w_f, b_f, conv_w, w_conv_branch, p_init, f_init, batch, tm):
    rows = h.shape[0]
    nt = rows // (batch * tm)
    row_blk = lambda width: pl.BlockSpec((tm, width), lambda b, t: (b * nt + t, 0))
    out_rows = lambda width, dt: jax.ShapeDtypeStruct((rows, width), dt)
    return pl.pallas_call(
        functools.partial(_mix_in_body, tm=tm),
        grid=(batch, nt),
        in_specs=[
            row_blk(D_MODEL),
            _resident((1, D_MODEL)),
            _resident((N_QKV, D_MODEL)),
            _resident((N_MIX, D_MODEL)),
            _resident((LANES, D_MODEL)),
            _resident((1, LANES)),
            _resident((CONV_K, CONV_WIDTH)),
            _resident((CONV_WIDTH, D_MODEL)),
            _resident((SUBLANES, CONV_WIDTH)),
            _resident((1, LANES)),
            _resident((LANES, ATTN_HEADS // HEADS_PER_BLOCK * LANES)),
            _resident((LANES, ATTN_HEADS * LANES)),
        ],
        out_specs=[
            row_blk(2 * ATTN_HEADS * LANES), row_blk(ATTN_HEADS * LANES),
            row_blk(ATTN_HEADS * LANES),
            row_blk(ATTN_HEADS), row_blk(D_MODEL), row_blk(D_MODEL),
            pl.BlockSpec((SUBLANES, CONV_WIDTH), lambda b, t: (b * nt + t, 0)),
        ],
        out_shape=[
            out_rows(2 * ATTN_HEADS * LANES, bf16), out_rows(ATTN_HEADS * LANES, bf16),
            out_rows(ATTN_HEADS * LANES, bf16),
            out_rows(ATTN_HEADS, f32), out_rows(D_MODEL, bf16), out_rows(D_MODEL, bf16),
            jax.ShapeDtypeStruct((batch * nt * SUBLANES, CONV_WIDTH), f32),
        ],
        scratch_shapes=[pltpu.VMEM((tm + SUBLANES, CONV_WIDTH), f32),
                        pltpu.VMEM((1, LANES), f32)],
        compiler_params=pltpu.CompilerParams(
            dimension_semantics=("arbitrary", "arbitrary"), vmem_limit_bytes=VMEM_LIMIT_BYTES),
        name="mixer_input",
    )(h, g, w_qkv, w_mix, w_f, b_f, conv_w, w_conv_branch, p_init, f_init, *_bias_selectors())


def _attn_body(q_ref, k_hbm, v_hbm, km_ref, vm_ref, o_ref,
               m_scr, alpha_scr, acc_scr, s_scr, sm_scr, p_scr, pm_scr, k_ref, v_ref, kv_sems,
               *, tq, tk, rows, pairs):
    i = pl.program_id(2)

    seq = k_ref.shape[0]
    part_rows = seq // KV_PARTS

    def kv_copy(part, src, dst, sem):
        src_rows = pl.ds(pl.multiple_of(pl.program_id(0) * seq + part * part_rows, part_rows),
                         part_rows)
        return pltpu.make_async_copy(src.at[src_rows, :],
                                     dst.at[pl.ds(part * part_rows, part_rows), :],
                                     kv_sems.at[sem])

    copies = [[kv_copy(part, k_hbm, k_ref, 2 * part), kv_copy(part, v_hbm, v_ref, 2 * part + 1)]
              for part in range(KV_PARTS)]

    @pl.when(i == 0)
    def _():
        for part in copies:
            for copy in part:
                copy.start()

    for part in range(KV_PARTS):
        @pl.when(i == part * (part_rows // tq))
        def _(part=part):
            for copy in copies[part]:
                copy.wait()

    low_half = lax.broadcasted_iota(jnp.int32, (tq, LANES), 1) < HEAD_DIM
    head_lanes = HEADS_PER_BLOCK * LANES
    m_scr[...] = jnp.full(m_scr.shape, MASK_VALUE, f32)
    acc_scr[...] = jnp.zeros(acc_scr.shape, f32)
    contract_last = (((1,), (1,)), ((), ()))
    lane = lax.broadcasted_iota(jnp.int32, (rows, LANES), 1)
    sub = lax.broadcasted_iota(jnp.int32, (rows, LANES), 0)
    n_heads = pairs * HEADS_PER_BLOCK
    assert n_heads % 2 == 0

    def scores(hh, keys_ref, key_rows):
        pair = hh // HEADS_PER_BLOCK
        q_head = q_ref[:, hh * head_lanes:(hh + 1) * head_lanes]
        k_pair = keys_ref[key_rows, pair * head_lanes:(pair + 1) * head_lanes]
        return lax.dot_general(q_head, k_pair, contract_last, preferred_element_type=f32)

    def softmax_rows(hh, r0, parts):
        rs = slice(r0, r0 + rows)
        blocks = [blk if valid is None else jnp.where(valid, blk, MASK_VALUE)
                  for blk, valid, _, _ in parts]
        m_old = m_scr[hh, rs, :]
        m_blk = functools.reduce(jnp.maximum, blocks)
        m_new = jnp.maximum(
            m_old, jnp.broadcast_to(jnp.max(m_blk, axis=-1, keepdims=True), (rows, LANES)))
        for blk, (_, _, dst, c0) in zip(blocks, parts):
            dst[hh % 2, rs, c0:c0 + LANES] = jnp.exp2(blk - m_new).astype(bf16)
        alpha_scr[hh, rs, :] = jnp.exp2(m_old - m_new)
        m_scr[hh, rs, :] = m_new

    def accumulate(hh, pv):
        acc_scr[hh] = alpha_scr[hh] * acc_scr[hh] + pv

    s_scr[0] = scores(0, k_ref, pl.ds(0, tk))

    def full_step(j, carry):
        start = pl.multiple_of(j * tk, tk)
        for hh in range(n_heads):
            slot = hh % 2
            if hh + 1 < n_heads:
                s_scr[1 - slot] = scores(hh + 1, k_ref, pl.ds(start, tk))
            else:
                s_scr[1 - slot] = scores(0, k_ref, pl.ds(pl.multiple_of(start + tk, tk), tk))
            for r0 in range(0, tq, rows):
                softmax_rows(hh, r0, [
                    (s_scr[slot, r0:r0 + rows, c * LANES:(c + 1) * LANES], None, p_scr, c * LANES)
                    for c in range(tk // LANES)])
            accumulate(hh, jnp.dot(p_scr[slot], v_ref[pl.ds(start, tk), hh * LANES:(hh + 1) * LANES],
                                   preferred_element_type=f32))
        return carry

    lax.fori_loop(0, lax.shift_right_logical(i, 1),
                  lambda jj, c: full_step(2 * jj + 1, full_step(2 * jj, c)), 0)

    @pl.when(jnp.bitwise_and(i, 1) == 1)
    def _():
        full_step(i - 1, 0)

    start = pl.multiple_of(i * tk, tk)
    meta_valid = lane < N_META
    sm_scr[0] = scores(0, km_ref, slice(None))
    for hh in range(n_heads):
        slot = hh % 2
        if hh + 1 < n_heads:
            s_scr[1 - slot] = scores(hh + 1, k_ref, pl.ds(start, tk))
            sm_scr[1 - slot] = scores(hh + 1, km_ref, slice(None))
        for r0 in range(0, tq, rows):
            n_cols = r0 + rows
            parts = [(s_scr[slot, r0:r0 + rows, c * LANES:(c + 1) * LANES],
                      None if (c + 1) * LANES - 1 <= r0 else (lane + c * LANES <= sub + r0),
                      p_scr, c * LANES) for c in range(n_cols // LANES)]
            parts.append((sm_scr[slot, r0:r0 + rows, :], meta_valid, pm_scr, 0))
            softmax_rows(hh, r0, parts)
            if n_cols < tk:
                p_scr[slot, r0:r0 + rows, n_cols:tk] = jnp.zeros((rows, tk - n_cols), bf16)
        accumulate(hh, jnp.dot(p_scr[slot], v_ref[pl.ds(start, tk), hh * LANES:(hh + 1) * LANES],
                               preferred_element_type=f32)
                   + jnp.dot(pm_scr[slot], vm_ref[:, hh * LANES:(hh + 1) * LANES],
                             preferred_element_type=f32))

    for pair in range(pairs):
        acc_a, acc_b = acc_scr[HEADS_PER_BLOCK * pair], acc_scr[HEADS_PER_BLOCK * pair + 1]
        out = jnp.where(low_half, acc_a / acc_a[:, HEAD_DIM:HEAD_DIM + 1], acc_b / acc_b[:, 0:1])
        o_ref[:, pair * LANES:(pair + 1) * LANES] = out.astype(bf16)


def _attention(qx, kx, v_aug, kx_meta, v_meta, batch, seq, tq, tk, rows, pairs):
    nq = seq // tq
    n_heads = pairs * HEADS_PER_BLOCK
    n_blk = ATTN_HEADS // n_heads
    pair_lanes = n_heads * LANES
    q_spec = pl.BlockSpec((tq, HEADS_PER_BLOCK * pair_lanes), lambda b, p, i: (b * nq + i, p))
    o_spec = pl.BlockSpec((tq, pairs * LANES), lambda b, p, i: (b * nq + i, p))
    assert n_blk == 1 and seq % (KV_PARTS * tq) == 0
    kv_spec = pl.BlockSpec(memory_space=pl.ANY)
    meta_spec = pl.BlockSpec((LANES, pair_lanes), lambda b, p, i: (0, p))
    per_head = lambda width, dt: pltpu.VMEM((n_heads, tq, width), dt)
    ring = lambda width, dt: pltpu.VMEM((2, tq, width), dt)
    return pl.pallas_call(
        functools.partial(_attn_body, tq=tq, tk=tk, rows=rows, pairs=pairs),
        grid=(batch, n_blk, nq),
        in_specs=[q_spec, kv_spec, kv_spec, meta_spec, meta_spec],
        out_specs=o_spec,
        out_shape=jax.ShapeDtypeStruct((batch * seq, ATTN_WIDTH), bf16),
        scratch_shapes=[per_head(LANES, f32), per_head(LANES, f32), per_head(LANES, f32),
                        ring(tk, f32), ring(LANES, f32), ring(tk, bf16), ring(LANES, bf16),
                        pltpu.VMEM((seq, pair_lanes), bf16), pltpu.VMEM((seq, pair_lanes), bf16),
                        pltpu.SemaphoreType.DMA((2 * KV_PARTS,))],
        compiler_params=pltpu.CompilerParams(
            dimension_semantics=("arbitrary", "arbitrary", "arbitrary"),
            vmem_limit_bytes=VMEM_LIMIT_BYTES),
        name="forgetting_attention",
    )(qx, kx, v_aug, kx_meta, v_meta)


def _mix_out_ffn_body(h_ref, attn_ref, sga_ref, gcv_ref, wa_ref, wo_ref, g_mix_ref,
                      g_pre_ref, g_post_ref, w_in_ref, w_out_ref, o_ref, act_ref):
    y_attn = jnp.dot(attn_ref[...], wa_ref[...], preferred_element_type=f32)
    gated = sga_ref[...].astype(f32) * y_attn + gcv_ref[...].astype(f32)
    mixed = jnp.dot(gated.astype(bf16), wo_ref[...], preferred_element_type=f32)
    h2 = h_ref[...] + _rms_norm(mixed, g_mix_ref[...])
    o_ref[...] = _swiglu_residual(h2, g_pre_ref[...], g_post_ref[...],
                                  w_in_ref, w_out_ref, act_ref)


def _mix_out_ffn(h, attn, sga, gcv, w_attn_branch, w_out, g_mix, g_pre, g_post, w_in, w_ffn_out, tm):
    rows = h.shape[0]
    row_blk = lambda width: pl.BlockSpec((tm, width), lambda i: (i, 0))
    return pl.pallas_call(
        _mix_out_ffn_body,
        grid=(rows // tm,),
        in_specs=[row_blk(D_MODEL), row_blk(ATTN_WIDTH), row_blk(D_MODEL), row_blk(D_MODEL),
                  _resident((ATTN_WIDTH, D_MODEL)), _resident((D_MODEL, D_MODEL)),
                  _resident((1, D_MODEL)), _resident((1, D_MODEL)), _resident((1, D_MODEL)),
                  _resident((D_MODEL, 2 * D_FF)), _resident((D_FF, D_MODEL))],
        out_specs=row_blk(D_MODEL),
        out_shape=jax.ShapeDtypeStruct((rows, D_MODEL), f32),
        scratch_shapes=[pltpu.VMEM((tm, D_FF), bf16)],
        compiler_params=pltpu.CompilerParams(
            dimension_semantics=("arbitrary",), vmem_limit_bytes=VMEM_LIMIT_BYTES),
        name="mixer_output_ffn",
    )(h, attn, sga, gcv, w_attn_branch, w_out, g_mix, g_pre, g_post, w_in, w_ffn_out)


def kernel(x, meta_tokens, w_in, b_forget, conv_w, w_attn_branch, w_conv_branch, w_out,
           g_ffn1_pre, g_ffn1_post, w_ffn1_in, w_ffn1_out, g_mix_pre, g_mix_post,
           g_ffn2_pre, g_ffn2_post, w_ffn2_in, w_ffn2_out):
    batch, seq, d = x.shape
    assert d == D_MODEL and w_in.shape[0] == 1 and meta_tokens.shape == (N_META, D_MODEL)
    assert seq % ROW_TILE == 0 and seq % FFN_TILE == 0 and seq % ATTN_TQ == 0
    assert ATTN_TQ == ATTN_TK and ATTN_TQ % ATTN_ROWS == 0 and ATTN_ROWS % LANES == 0

    gain = lambda g: g[0].reshape(1, D_MODEL).astype(f32)
    wt = w_in[0].T
    w_f = jnp.pad(wt[N_QKV:N_QKV + ATTN_HEADS].astype(bf16), ((0, LANES - ATTN_HEADS), (0, 0)))
    b_f = jnp.pad(b_forget[0].astype(f32), (0, LANES - ATTN_HEADS)).reshape(1, LANES)
    cw = conv_w[0].astype(f32)
    w1_in, w1_out = w_ffn1_in[0].astype(bf16), w_ffn1_out[0].astype(bf16)

    rows = batch * seq
    later_weights = (
        (wt, 0, N_QKV), (wt, N_QKV + ATTN_HEADS, N_MIX),
        (w_conv_branch[0], 0, CONV_WIDTH), (w_attn_branch[0], 0, ATTN_WIDTH),
        (w_out[0], 0, D_MODEL), (w_ffn2_in[0], 0, D_MODEL), (w_ffn2_out[0], 0, D_FF))
    h1, hm, w_qkv, w_mix, wc, wa, wo, w2_in, w2_out = _ffn(
        x.reshape(rows, D_MODEL), meta_tokens.astype(f32), gain(g_ffn1_pre), gain(g_ffn1_post),
        w1_in, w1_out, FFN_TILE, later_weights)

    zeros_p = jnp.zeros((SUBLANES, CONV_WIDTH), f32)
    zeros_f = jnp.zeros((1, LANES), f32)
    _, kxm, vm, fm, _, _, pm_tail = _mix_in(hm, gain(g_mix_pre), w_qkv, w_mix, w_f, b_f, cw, wc,
                                           zeros_p, zeros_f, 1, N_META)
    pad_rows = lambda a: jnp.pad(a, ((0, LANES - N_META), (0, 0)))
    f_init = jnp.pad(fm[N_META - 1:N_META, :], ((0, 0), (0, LANES - ATTN_HEADS)))

    qx, kx, v, _, sga, gcv, _ = _mix_in(h1, gain(g_mix_pre), w_qkv, w_mix, w_f, b_f, cw, wc,
                                        pm_tail, f_init, batch, ROW_TILE)
    attn = _attention(qx, kx, v, pad_rows(kxm), pad_rows(vm),
                      batch, seq, ATTN_TQ, ATTN_TK, ATTN_ROWS, ATTN_PAIRS)
    h3 = _mix_out_ffn(h1, attn, sga, gcv, wa, wo, gain(g_mix_post), gain(g_ffn2_pre),
                      gain(g_ffn2_post), w2_in, w2_out, ROW_TILE)
    return h3.reshape(batch, seq, D_MODEL)
```

```python
import functools

import jax
import jax.numpy as jnp
import numpy as np
from jax import lax
from jax.experimental import pallas as pl
from jax.experimental.pallas import tpu as pltpu

D_MODEL = 1024
D_FF = 2816
N_META = 16
ATTN_HEADS = 8
HEAD_DIM = 64
ATTN_WIDTH = ATTN_HEADS * HEAD_DIM
CONV_WIDTH = 512
CONV_K = 3
NORM_EPS = 1e-6

LANES = 128
SUBLANES = 8
HEADS_PER_BLOCK = LANES // HEAD_DIM
VMEM_LIMIT_BYTES = 56 * 1024 * 1024
MASK_VALUE = -1e30
LOG2E = 1.4426950408889634
N_SPLIT = 3

ROW_TILE = 512
FFN_TILE = 1024
FF_CHUNK = 256
ATTN_TQ = 512
ATTN_TK = 512
ATTN_ROWS = 128
ATTN_PAIRS = 4
KV_PARTS = 16

C_Q, C_K, C_V = 0, 512, 1024
C_CB, C_CC, C_CIN, C_GA, C_GC = 0, 512, 1024, 1536, 2560
N_QKV = 3 * ATTN_WIDTH
N_MIX = 3 * CONV_WIDTH + 2 * D_MODEL

bf16 = jnp.bfloat16
f32 = jnp.float32


def _rms_norm(x, g):
    ms = jnp.mean(x * x, axis=-1, keepdims=True)
    return x * lax.rsqrt(ms + NORM_EPS) * g


def _resident(shape):
    nd = len(shape)
    return pl.BlockSpec(shape, lambda *_: (0,) * nd, pipeline_mode=pl.Buffered(1))


def _swiglu_residual(h, g_pre, g_post, w_in_ref, w_out_ref, act_ref):
    u = _rms_norm(h, g_pre).astype(bf16)
    for c in range(D_FF // FF_CHUNK):
        lo = c * FF_CHUNK
        a = jnp.dot(u, w_in_ref[:, lo:lo + FF_CHUNK], preferred_element_type=f32)
        b = jnp.dot(u, w_in_ref[:, D_FF + lo:D_FF + lo + FF_CHUNK], preferred_element_type=f32)
        act_ref[:, lo:lo + FF_CHUNK] = (a * jax.nn.sigmoid(a) * b).astype(bf16)
    y = jnp.dot(act_ref[...], w_out_ref[...], preferred_element_type=f32)
    return h + 0.5 * _rms_norm(y, g_post)


def _ffn_body(h_ref, meta_ref, g_pre_ref, g_post_ref, w_in_ref, w_out_ref, *rest):
    n_cast = (len(rest) - 4) // 2
    o_ref, meta_o_ref = rest[n_cast], rest[n_cast + 1]
    act_ref, meta_act_ref = rest[-2], rest[-1]
    for src, dst in zip(rest[:n_cast], rest[n_cast + 2:-2]):
        dst[...] = src[...].astype(bf16)

    @pl.when(pl.program_id(0) == 0)
    def _():
        meta_o_ref[...] = _swiglu_residual(meta_ref[...], g_pre_ref[...], g_post_ref[...],
                                           w_in_ref, w_out_ref, meta_act_ref)

    o_ref[...] = _swiglu_residual(h_ref[...], g_pre_ref[...], g_post_ref[...],
                                  w_in_ref, w_out_ref, act_ref)


def _ffn(h, meta, g_pre, g_post, w_in, w_out, tm, cast_along):
    rows = h.shape[0]
    steps = rows // tm
    row_blk = pl.BlockSpec((tm, D_MODEL), lambda i: (i, 0))
    cast_in, cast_out = [], []
    for w, first, count in cast_along:
        per, width = count // steps, w.shape[1]
        assert count % steps == 0 and per % (2 * SUBLANES) == 0 and first % SUBLANES == 0
        cast_in.append(pl.BlockSpec(
            (pl.Element(per), pl.Element(width)),
            functools.partial(lambda f, p, i: (pl.multiple_of(f + i * p, SUBLANES), 0), first, per)))
        cast_out.append(pl.BlockSpec((per, width), lambda i: (i, 0)))
    meta_blk = pl.BlockSpec(meta.shape, lambda i: (0, 0))
    return pl.pallas_call(
        _ffn_body,
        grid=(steps,),
        in_specs=[
            row_blk,
            _resident(meta.shape),
            _resident((1, D_MODEL)),
            _resident((1, D_MODEL)),
            _resident((D_MODEL, 2 * D_FF)),
            _resident((D_FF, D_MODEL)),
        ] + cast_in,
        out_specs=[row_blk, meta_blk] + cast_out,
        out_shape=[jax.ShapeDtypeStruct((rows, D_MODEL), f32),
                   jax.ShapeDtypeStruct(meta.shape, f32)]
        + [jax.ShapeDtypeStruct((count, w.shape[1]), bf16) for w, _, count in cast_along],
        scratch_shapes=[pltpu.VMEM((tm, D_FF), bf16), pltpu.VMEM((meta.shape[0], D_FF), bf16)],
        compiler_params=pltpu.CompilerParams(
            dimension_semantics=("arbitrary",), vmem_limit_bytes=VMEM_LIMIT_BYTES),
        name="ffn_half_step",
    )(h, meta, g_pre, g_post, w_in, w_out, *[w for w, _, _ in cast_along])


def _log_sigmoid(x):
    return jnp.minimum(x, 0.0) - jnp.log1p(jnp.exp(-jnp.abs(x)))


def _split3(x):
    hi = x.astype(bf16)
    r = x - hi.astype(f32)
    mid = r.astype(bf16)
    lo = (r - mid.astype(f32)).astype(bf16)
    return hi, mid, lo


def _pack3(x, head_lanes):
    hi, mid, lo = _split3(jnp.where(head_lanes, x, 0.0))
    packed = (hi.astype(f32) + pltpu.roll(mid.astype(f32), ATTN_HEADS, 1)
              + pltpu.roll(lo.astype(f32), 2 * ATTN_HEADS, 1))
    return packed.astype(bf16)


def _mix_in_body(h_ref, g_ref, w_qkv_ref, w_mix_ref, w_f_ref, bf_ref, cw_ref, wc_ref,
                 p_init_ref, f_init_ref, ksel_ref, qsel_ref,
                 qx_ref, kx_ref, v_ref, fsum_ref, sga_ref, gcv_ref, p_tail_ref,
                 p_scr, f_carry, *, tm):
    @pl.when(pl.program_id(1) == 0)
    def _():
        p_scr[0:SUBLANES, :] = p_init_ref[...]
        f_carry[...] = f_init_ref[...]

    u = _rms_norm(h_ref[...], g_ref[...]).astype(bf16)

    def proj(wt_ref, lo, width):
        return lax.dot_general(u, wt_ref[lo:lo + width, :], (((1,), (1,)), ((), ())),
                               preferred_element_type=f32)

    rb = min(tm, LANES)
    n_blk = tm // rb
    row_blk = [slice(blk * rb, (blk + 1) * rb) for blk in range(n_blk)]
    n_pairs = ATTN_HEADS // HEADS_PER_BLOCK
    head_lanes = lax.broadcasted_iota(jnp.int32, (rb, LANES), 1) < ATTN_HEADS
    log_f = _log_sigmoid(proj(w_f_ref, 0, LANES) + bf_ref[...]) * LOG2E

    q = (proj(w_qkv_ref, C_Q, ATTN_WIDTH) * (LOG2E * HEAD_DIM ** -0.5)).astype(bf16)
    lane = lax.broadcasted_iota(jnp.int32, (tm, LANES), 1)
    for pair in range(n_pairs):
        q2 = q[:, pair * LANES:(pair + 1) * LANES]
        zeros = jnp.zeros_like(q2)
        c0 = 2 * pair * HEADS_PER_BLOCK * LANES
        qx_ref[:, c0:c0 + LANES] = jnp.where(lane < HEAD_DIM, q2, zeros)
        qx_ref[:, c0 + 2 * LANES:c0 + 3 * LANES] = jnp.where(lane < HEAD_DIM, zeros, q2)

    tri = (lax.broadcasted_iota(jnp.int32, (rb, rb), 0)
           >= lax.broadcasted_iota(jnp.int32, (rb, rb), 1)).astype(bf16)
    c3 = [jnp.dot(tri, _pack3(log_f[rs, :], head_lanes), preferred_element_type=f32)
          for rs in row_blk]

    k = proj(w_qkv_ref, C_K, ATTN_WIDTH)
    for pair in range(n_pairs):
        c0 = pair * HEADS_PER_BLOCK * LANES
        kx_ref[:, c0:c0 + LANES] = k[:, pair * LANES:(pair + 1) * LANES].astype(bf16)

    carry = f_carry[...]
    pieces = []
    for rs, c in zip(row_blk, c3):
        cs = c + pltpu.roll(c, LANES - ATTN_HEADS, 1) + pltpu.roll(c, LANES - 2 * ATTN_HEADS, 1)
        cs = jnp.where(head_lanes, cs, 0.0) + carry
        carry = cs[rb - 1:rb, :]
        fsum_ref[rs, :] = cs[:, :ATTN_HEADS]
        pieces.append(_pack3(cs, head_lanes))
    f_carry[...] = carry

    v = proj(w_qkv_ref, C_V, ATTN_WIDTH)
    ones_a = jnp.where(lane == HEAD_DIM, 1.0, 0.0)
    ones_b = jnp.where(lane == 0, 1.0, 0.0)
    for pair in range(n_pairs):
        v2 = v[:, pair * LANES:(pair + 1) * LANES]
        c0 = pair * HEADS_PER_BLOCK * LANES
        v_ref[:, c0:c0 + LANES] = jnp.where(lane < HEAD_DIM, v2, ones_a).astype(bf16)
        v_ref[:, c0 + LANES:c0 + 2 * LANES] = jnp.where(lane < HEAD_DIM, ones_b, v2).astype(bf16)

    k_lane = lax.broadcasted_iota(jnp.int32, (rb, n_pairs * LANES), 1)
    k_ones = jnp.bitwise_and(k_lane, LANES - 1) < N_SPLIT
    q_lane = lax.broadcasted_iota(jnp.int32, (rb, ATTN_HEADS * LANES), 1)
    q_head = jnp.right_shift(q_lane, LANES.bit_length() - 1)
    q_ones_lo = N_SPLIT * (1 + jnp.bitwise_and(q_head, HEADS_PER_BLOCK - 1))
    q_in_blk = jnp.bitwise_and(q_lane, LANES - 1)
    q_ones = (q_in_blk >= q_ones_lo) & (q_in_blk < q_ones_lo + N_SPLIT)
    for rs, pc in zip(row_blk, pieces):
        k_bias = jnp.dot(pc, ksel_ref[...], preferred_element_type=f32)
        k_bias = jnp.where(k_ones, 1.0, k_bias).astype(bf16)
        for pair in range(n_pairs):
            c0 = pair * HEADS_PER_BLOCK * LANES + LANES
            kx_ref[rs, c0:c0 + LANES] = k_bias[:, pair * LANES:(pair + 1) * LANES]
        q_bias = jnp.dot(pc, qsel_ref[...], preferred_element_type=f32)
        q_bias = jnp.where(q_ones, 1.0, q_bias).astype(bf16)
        for head in range(ATTN_HEADS):
            c0 = head * HEADS_PER_BLOCK * LANES + LANES
            qx_ref[rs, c0:c0 + LANES] = q_bias[:, head * LANES:(head + 1) * LANES]

    p = proj(w_mix_ref, C_CC, CONV_WIDTH) * proj(w_mix_ref, C_CIN, CONV_WIDTH)
    p_scr[SUBLANES:SUBLANES + tm, :] = p
    conv = (p_scr[SUBLANES - 2:SUBLANES - 2 + tm, :] * cw_ref[0:1, :]
            + p_scr[SUBLANES - 1:SUBLANES - 1 + tm, :] * cw_ref[1:2, :]
            + p * cw_ref[2:3, :])
    tail = p_scr[tm:tm + SUBLANES, :]
    p_scr[0:SUBLANES, :] = tail
    p_tail_ref[...] = tail
    conv_in = (proj(w_mix_ref, C_CB, CONV_WIDTH) * conv).astype(bf16)
    y_conv = jnp.dot(conv_in, wc_ref[...], preferred_element_type=f32)
    sga_ref[...] = jax.nn.sigmoid(proj(w_mix_ref, C_GA, D_MODEL)).astype(bf16)
    gcv_ref[...] = (jax.nn.sigmoid(proj(w_mix_ref, C_GC, D_MODEL)) * y_conv).astype(bf16)


def _bias_selectors():
    ksel = np.zeros((LANES, ATTN_HEADS // HEADS_PER_BLOCK * LANES), np.float32)
    qsel = np.zeros((LANES, ATTN_HEADS * LANES), np.float32)
    for h in range(ATTN_HEADS):
        for t in range(N_SPLIT):
            col = (h // HEADS_PER_BLOCK) * LANES + N_SPLIT * (1 + h % HEADS_PER_BLOCK) + t
            ksel[t * ATTN_HEADS + h, col] = -1.0
            qsel[t * ATTN_HEADS + h, h * LANES + t] = 1.0
    return jnp.asarray(ksel, bf16), jnp.asarray(qsel, bf16)


def _mix_in(h, g, w_qkv, w_mix, w_f, b_f, conv_w, w_conv_branch, p_init, f_init, batch, tm):
    rows = h.shape[0]
    nt = rows // (batch * tm)
    row_blk = lambda width: pl.BlockSpec((tm, width), lambda b, t: (b * nt + t, 0))
    out_rows = lambda width, dt: jax.ShapeDtypeStruct((rows, width), dt)
    return pl.pallas_call(
        functools.partial(_mix_in_body, tm=tm),
        grid=(batch, nt),
        in_specs=[
            row_blk(D_MODEL),
            _resident((1, D_MODEL)),
            _resident((N_QKV, D_MODEL)),
            _resident((N_MIX, D_MODEL)),
            _resident((LANES, D_MODEL)),
            _resident((1, LANES)),
            _resident((CONV_K, CONV_WIDTH)),
            _resident((CONV_WIDTH, D_MODEL)),
            _resident((SUBLANES, CONV_WIDTH)),
            _resident((1, LANES)),
            _resident((LANES, ATTN_HEADS // HEADS_PER_BLOCK * LANES)),
            _resident((LANES, ATTN_HEADS * LANES)),
        ],
        out_specs=[
            row_blk(2 * ATTN_HEADS * LANES), row_blk(ATTN_HEADS * LANES),
            row_blk(ATTN_HEADS * LANES),
            row_blk(ATTN_HEADS), row_blk(D_MODEL), row_blk(D_MODEL),
            pl.BlockSpec((SUBLANES, CONV_WIDTH), lambda b, t: (b * nt + t, 0)),
        ],
        out_shape=[
            out_rows(2 * ATTN_HEADS * LANES, bf16), out_rows(ATTN_HEADS * LANES, bf16),
            out_rows(ATTN_HEADS * LANES, bf16),
            out_rows(ATTN_HEADS, f32), out_rows(D_MODEL, bf16), out_rows(D_MODEL, bf16),
            jax.ShapeDtypeStruct((batch * nt * SUBLANES, CONV_WIDTH), f32),
        ],
        scratch_shapes=[pltpu.VMEM((tm + SUBLANES, CONV_WIDTH), f32),
                        pltpu.VMEM((1, LANES), f32)],
        compiler_params=pltpu.CompilerParams(
            dimension_semantics=("arbitrary", "arbitrary"), vmem_limit_bytes=VMEM_LIMIT_BYTES),
        name="mixer_input",
    )(h, g, w_qkv, w_mix, w_f, b_f, conv_w, w_conv_branch, p_init, f_init, *_bias_selectors())


def _attn_body(q_ref, k_hbm, v_hbm, km_ref, vm_ref, o_ref,
               m_scr, alpha_scr, acc_scr, s_scr, sm_scr, p_scr, pm_scr, k_ref, v_ref, kv_sems,
               *, tq, tk, rows, pairs):
    i = pl.program_id(2)

    seq = k_ref.shape[0]
    part_rows = seq // KV_PARTS

    def kv_copy(part, src, dst, sem):
        src_rows = pl.ds(pl.multiple_of(pl.program_id(0) * seq + part * part_rows, part_rows),
                         part_rows)
        return pltpu.make_async_copy(src.at[src_rows, :],
                                     dst.at[pl.ds(part * part_rows, part_rows), :],
                                     kv_sems.at[sem])

    copies = [[kv_copy(part, k_hbm, k_ref, 2 * part), kv_copy(part, v_hbm, v_ref, 2 * part + 1)]
              for part in range(KV_PARTS)]

    @pl.when(i == 0)
    def _():
        for part in copies:
            for queue, copy in enumerate(part):
                copy.start(priority=queue)

    for part in range(KV_PARTS):
        @pl.when(i == part * (part_rows // tq))
        def _(part=part):
            for copy in copies[part]:
                copy.wait()

    low_half = lax.broadcasted_iota(jnp.int32, (tq, LANES), 1) < HEAD_DIM
    head_lanes = HEADS_PER_BLOCK * LANES
    m_scr[...] = jnp.full(m_scr.shape, MASK_VALUE, f32)
    acc_scr[...] = jnp.zeros(acc_scr.shape, f32)
    contract_last = (((1,), (1,)), ((), ()))
    lane = lax.broadcasted_iota(jnp.int32, (rows, LANES), 1)
    sub = lax.broadcasted_iota(jnp.int32, (rows, LANES), 0)
    n_heads = pairs * HEADS_PER_BLOCK
    assert n_heads % 2 == 0

    def scores(hh, keys_ref, key_rows):
        pair = hh // HEADS_PER_BLOCK
        q_head = q_ref[:, hh * head_lanes:(hh + 1) * head_lanes]
        k_pair = keys_ref[key_rows, pair * head_lanes:(pair + 1) * head_lanes]
        return lax.dot_general(q_head, k_pair, contract_last, preferred_element_type=f32)

    def softmax_rows(hh, r0, parts):
        rs = slice(r0, r0 + rows)
        blocks = [blk if valid is None else jnp.where(valid, blk, MASK_VALUE)
                  for blk, valid, _, _ in parts]
        m_old = m_scr[hh, rs, :]
        m_blk = functools.reduce(jnp.maximum, blocks)
        m_new = jnp.maximum(
            m_old, jnp.broadcast_to(jnp.max(m_blk, axis=-1, keepdims=True), (rows, LANES)))
        for blk, (_, _, dst, c0) in zip(blocks, parts):
            dst[hh % 2, rs, c0:c0 + LANES] = jnp.exp2(blk - m_new).astype(bf16)
        alpha_scr[hh, rs, :] = jnp.exp2(m_old - m_new)
        m_scr[hh, rs, :] = m_new

    def accumulate(hh, pv):
        acc_scr[hh] = alpha_scr[hh] * acc_scr[hh] + pv

    s_scr[0] = scores(0, k_ref, pl.ds(0, tk))

    def full_step(j, carry):
        start = pl.multiple_of(j * tk, tk)
        for hh in range(n_heads):
            slot = hh % 2
            if hh + 1 < n_heads:
                s_scr[1 - slot] = scores(hh + 1, k_ref, pl.ds(start, tk))
            else:
                s_scr[1 - slot] = scores(0, k_ref, pl.ds(pl.multiple_of(start + tk, tk), tk))
            for r0 in range(0, tq, rows):
                softmax_rows(hh, r0, [
                    (s_scr[slot, r0:r0 + rows, c * LANES:(c + 1) * LANES], None, p_scr, c * LANES)
                    for c in range(tk // LANES)])
            accumulate(hh, jnp.dot(p_scr[slot], v_ref[pl.ds(start, tk), hh * LANES:(hh + 1) * LANES],
                                   preferred_element_type=f32))
        return carry

    lax.fori_loop(0, lax.shift_right_logical(i, 1),
                  lambda jj, c: full_step(2 * jj + 1, full_step(2 * jj, c)), 0)

    @pl.when(jnp.bitwise_and(i, 1) == 1)
    def _():
        full_step(i - 1, 0)

    start = pl.multiple_of(i * tk, tk)
    meta_valid = lane < N_META
    sm_scr[0] = scores(0, km_ref, slice(None))
    for hh in range(n_heads):
        slot = hh % 2
        if hh + 1 < n_heads:
            s_scr[1 - slot] = scores(hh + 1, k_ref, pl.ds(start, tk))
            sm_scr[1 - slot] = scores(hh + 1, km_ref, slice(None))
        for r0 in range(0, tq, rows):
            n_cols = r0 + rows
            parts = [(s_scr[slot, r0:r0 + rows, c * LANES:(c + 1) * LANES],
                      None if (c + 1) * LANES - 1 <= r0 else (lane + c * LANES <= sub + r0),
                      p_scr, c * LANES) for c in range(n_cols // LANES)]
            parts.append((sm_scr[slot, r0:r0 + rows, :], meta_valid, pm_scr, 0))
            softmax_rows(hh, r0, parts)
            if n_cols < tk:
                p_scr[slot, r0:r0 + rows, n_cols:tk] = jnp.zeros((rows, tk - n_cols), bf16)
        accumulate(hh, jnp.dot(p_scr[slot], v_ref[pl.ds(start, tk), hh * LANES:(hh + 1) * LANES],
                               preferred_element_type=f32)
                   + jnp.dot(pm_scr[slot], vm_ref[:, hh * LANES:(hh + 1) * LANES],
                             preferred_element_type=f32))

    for pair in range(pairs):
        acc_a, acc_b = acc_scr[HEADS_PER_BLOCK * pair], acc_scr[HEADS_PER_BLOCK * pair + 1]
        out = jnp.where(low_half, acc_a / acc_a[:, HEAD_DIM:HEAD_DIM + 1], acc_b / acc_b[:, 0:1])
        o_ref[:, pair * LANES:(pair + 1) * LANES] = out.astype(bf16)


def _attention(qx, kx, v_aug, kx_meta, v_meta, batch, seq, tq, tk, rows, pairs):
    nq = seq // tq
    n_heads = pairs * HEADS_PER_BLOCK
    n_blk = ATTN_HEADS // n_heads
    pair_lanes = n_heads * LANES
    q_spec = pl.BlockSpec((tq, HEADS_PER_BLOCK * pair_lanes), lambda b, p, i: (b * nq + i, p))
    o_spec = pl.BlockSpec((tq, pairs * LANES), lambda b, p, i: (b * nq + i, p))
    assert n_blk == 1 and seq % (KV_PARTS * tq) == 0
    kv_spec = pl.BlockSpec(memory_space=pl.ANY)
    meta_spec = pl.BlockSpec((LANES, pair_lanes), lambda b, p, i: (0, p))
    per_head = lambda width, dt: pltpu.VMEM((n_heads, tq, width), dt)
    ring = lambda width, dt: pltpu.VMEM((2, tq, width), dt)
    return pl.pallas_call(
        functools.partial(_attn_body, tq=tq, tk=tk, rows=rows, pairs=pairs),
        grid=(batch, n_blk, nq),
        in_specs=[q_spec, kv_spec, kv_spec, meta_spec, meta_spec],
        out_specs=o_spec,
        out_shape=jax.ShapeDtypeStruct((batch * seq, ATTN_WIDTH), bf16),
        scratch_shapes=[per_head(LANES, f32), per_head(LANES, f32), per_head(LANES, f32),
                        ring(tk, f32), ring(LANES, f32), ring(tk, bf16), ring(LANES, bf16),
                        pltpu.VMEM((seq, pair_lanes), bf16), pltpu.VMEM((seq, pair_lanes), bf16),
                        pltpu.SemaphoreType.DMA((2 * KV_PARTS,))],
        compiler_params=pltpu.CompilerParams(
            dimension_semantics=("arbitrary", "arbitrary", "arbitrary"),
            vmem_limit_bytes=VMEM_LIMIT_BYTES),
        name="forgetting_attention",
    )(qx, kx, v_aug, kx_meta, v_meta)


def _mix_out_ffn_body(h_ref, attn_ref, sga_ref, gcv_ref, wa_ref, wo_ref, g_mix_ref,
                      g_pre_ref, g_post_ref, w_in_ref, w_out_ref, o_ref, act_ref):
    y_attn = jnp.dot(attn_ref[...], wa_ref[...], preferred_element_type=f32)
    gated = sga_ref[...].astype(f32) * y_attn + gcv_ref[...].astype(f32)
    mixed = jnp.dot(gated.astype(bf16), wo_ref[...], preferred_element_type=f32)
    h2 = h_ref[...] + _rms_norm(mixed, g_mix_ref[...])
    o_ref[...] = _swiglu_residual(h2, g_pre_ref[...], g_post_ref[...],
                                  w_in_ref, w_out_ref, act_ref)


def _mix_out_ffn(h, attn, sga, gcv, w_attn_branch, w_out, g_mix, g_pre, g_post, w_in, w_ffn_out, tm):
    rows = h.shape[0]
    row_blk = lambda width: pl.BlockSpec((tm, width), lambda i: (i, 0))
    return pl.pallas_call(
        _mix_out_ffn_body,
        grid=(rows // tm,),
        in_specs=[row_blk(D_MODEL), row_blk(ATTN_WIDTH), row_blk(D_MODEL), row_blk(D_MODEL),
                  _resident((ATTN_WIDTH, D_MODEL)), _resident((D_MODEL, D_MODEL)),
                  _resident((1, D_MODEL)), _resident((1, D_MODEL)), _resident((1, D_MODEL)),
                  _resident((D_MODEL, 2 * D_FF)), _resident((D_FF, D_MODEL))],
        out_specs=row_blk(D_MODEL),
        out_shape=jax.ShapeDtypeStruct((rows, D_MODEL), f32),
        scratch_shapes=[pltpu.VMEM((tm, D_FF), bf16)],
        compiler_params=pltpu.CompilerParams(
            dimension_semantics=("arbitrary",), vmem_limit_bytes=VMEM_LIMIT_BYTES),
        name="mixer_output_ffn",
    )(h, attn, sga, gcv, w_attn_branch, w_out, g_mix, g_pre, g_post, w_in, w_ffn_out)


def kernel(x, meta_tokens, w_in, b_forget, conv_w, w_attn_branch, w_conv_branch, w_out,
           g_ffn1_pre, g_ffn1_post, w_ffn1_in, w_ffn1_out, g_mix_pre, g_mix_post,
           g_ffn2_pre, g_ffn2_post, w_ffn2_in, w_ffn2_out):
    batch, seq, d = x.shape
    assert d == D_MODEL and w_in.shape[0] == 1 and meta_tokens.shape == (N_META, D_MODEL)
    assert seq % ROW_TILE == 0 and seq % FFN_TILE == 0 and seq % ATTN_TQ == 0
    assert ATTN_TQ == ATTN_TK and ATTN_TQ % ATTN_ROWS == 0 and ATTN_ROWS % LANES == 0

    gain = lambda g: g[0].reshape(1, D_MODEL).astype(f32)
    wt = w_in[0].T
    w_f = jnp.pad(wt[N_QKV:N_QKV + ATTN_HEADS].astype(bf16), ((0, LANES - ATTN_HEADS), (0, 0)))
    b_f = jnp.pad(b_forget[0].astype(f32), (0, LANES - ATTN_HEADS)).reshape(1, LANES)
    cw = conv_w[0].astype(f32)
    w1_in, w1_out = w_ffn1_in[0].astype(bf16), w_ffn1_out[0].astype(bf16)

    rows = batch * seq
    later_weights = (
        (wt, 0, N_QKV), (wt, N_QKV + ATTN_HEADS, N_MIX),
        (w_conv_branch[0], 0, CONV_WIDTH), (w_attn_branch[0], 0, ATTN_WIDTH),
        (w_out[0], 0, D_MODEL), (w_ffn2_in[0], 0, D_MODEL), (w_ffn2_out[0], 0, D_FF))
    h1, hm, w_qkv, w_mix, wc, wa, wo, w2_in, w2_out = _ffn(
        x.reshape(rows, D_MODEL), meta_tokens.astype(f32), gain(g_ffn1_pre), gain(g_ffn1_post),
        w1_in, w1_out, FFN_TILE, later_weights)

    zeros_p = jnp.zeros((SUBLANES, CONV_WIDTH), f32)
    zeros_f = jnp.zeros((1, LANES), f32)
    _, kxm, vm, fm, _, _, pm_tail = _mix_in(hm, gain(g_mix_pre), w_qkv, w_mix, w_f, b_f, cw, wc,
                                           zeros_p, zeros_f, 1, N_META)
    pad_rows = lambda a: jnp.pad(a, ((0, LANES - N_META), (0, 0)))
    f_init = jnp.pad(fm[N_META - 1:N_META, :], ((0, 0), (0, LANES - ATTN_HEADS)))

    qx, kx, v, _, sga, gcv, _ = _mix_in(h1, gain(g_mix_pre), w_qkv, w_mix, w_f, b_f, cw, wc,
                                        pm_tail, f_init, batch, ROW_TILE)
    attn = _attention(qx, kx, v, pad_rows(kxm), pad_rows(vm),
                      batch, seq, ATTN_TQ, ATTN_TK, ATTN_ROWS, ATTN_PAIRS)
    h3 = _mix_out_ffn(h1, attn, sga, gcv, wa, wo, gain(g_mix_post), gain(g_ffn2_pre),
                      gain(g_ffn2_post), w2_in, w2_out, ROW_TILE)
    return h3.reshape(batch, seq, D_MODEL)
```
